```python
import jax
import jax.numpy as jnp
from jax import lax
import numpy as np


D_MODEL = 2048
BATCH = 4
SEQ = 4096
DEPTH = 2

HEAD_DIM = 128
NSA_HEADS = 8
NSA_KV_HEADS = 2
NSA_GROUP = NSA_HEADS // NSA_KV_HEADS
NSA_WIDTH = NSA_HEADS * HEAD_DIM
NSA_KV_WIDTH = NSA_KV_HEADS * HEAD_DIM
CMP_LEN = 32
CMP_STRIDE = 16
SLC_LEN = 64
SLC_TOPK = 16
WIN = 512
SEL_QBLOCK = 64
WIN_QBLOCK = 128
GMLP_GROUPS = 4
GMLP_CHUNK = 128
GMLP_WIDTH = GMLP_GROUPS * HEAD_DIM
RET_HEADS = 4
RET_CHUNK = 128
RET_WIDTH = RET_HEADS * HEAD_DIM
MIX_WIDTH = NSA_WIDTH + GMLP_WIDTH + RET_WIDTH
IN_SPLITS = (NSA_WIDTH,) + (NSA_KV_WIDTH,) * 6 + (3 * NSA_HEADS, 2 * GMLP_WIDTH) + (RET_WIDTH,) * 4
IN_WIDTH = NSA_WIDTH + 6 * NSA_KV_WIDTH + 3 * NSA_HEADS + 2 * GMLP_WIDTH + 4 * RET_WIDTH
DENSE_FF = 5632
N_EXPERTS = 8
TOP_K = 2
MOE_FF = 2816
EPS = 1e-6
NEG_INF = -1e30
FORCE_SCORE = 1e4

kernel_name = 'hybrid_nsa_gmlp_retention_moe'


def rmsnorm(x, g):
    xf = x.astype(jnp.float32)
    y = xf * lax.rsqrt(jnp.mean(xf * xf, axis=-1, keepdims=True) + EPS)
    return (y * g.astype(jnp.float32)).astype(x.dtype)


def layernorm(x, g, b):
    xf = x.astype(jnp.float32)
    mu = jnp.mean(xf, axis=-1, keepdims=True)
    var = jnp.mean(jnp.square(xf - mu), axis=-1, keepdims=True)
    y = (xf - mu) * lax.rsqrt(var + EPS)
    return (y * g.astype(jnp.float32) + b.astype(jnp.float32)).astype(x.dtype)


def masked_softmax(s, mask):
    p = jax.nn.softmax(jnp.where(mask, s, NEG_INF), axis=-1)
    return jnp.where(mask, p, 0.0)


def alibi_slopes():
    h = jnp.arange(NSA_HEADS, dtype=jnp.float32)
    return jnp.exp2(-8.0 * (h + 1.0) / NSA_HEADS).reshape(NSA_KV_HEADS, NSA_GROUP)


def to_heads(z, n):
    b, t, _ = z.shape
    return z.reshape(b, t, n, HEAD_DIM).transpose(0, 2, 1, 3)


def nsa_compress(k, pe, w1, w2):
    b, g, t, d = k.shape
    nc = (t - CMP_LEN) // CMP_STRIDE + 1
    idx = jnp.arange(nc)[:, None] * CMP_STRIDE + jnp.arange(CMP_LEN)[None, :]
    blk = (k[:, :, idx] + pe).reshape(b, g, nc, CMP_LEN * d)
    return jax.nn.gelu(blk @ w1) @ w2


def nsa_compressed_attn(q, kc, vc, slopes):
    t = q.shape[3]
    nc = kc.shape[2]
    end = jnp.arange(nc) * CMP_STRIDE + (CMP_LEN - 1)
    dist = (jnp.arange(t)[:, None] - end[None, :]).astype(jnp.float32)
    s = jnp.einsum('bgrtd,bgcd->bgrtc', q, kc).astype(jnp.float32) * (HEAD_DIM ** -0.5)
    s = s - slopes[None, :, :, None, None] * dist
    p = masked_softmax(s, dist >= 0)
    o = jnp.einsum('bgrtc,bgcd->bgrtd', p.astype(vc.dtype), vc)
    return o, p


def nsa_select_blocks(p_cmp, t):
    nc = p_cmp.shape[-1]
    nb = t // SLC_LEN
    c0 = jnp.arange(nc)[:, None] * CMP_STRIDE
    s0 = jnp.arange(nb)[None, :] * SLC_LEN
    overlap = jnp.minimum(c0 + CMP_LEN, s0 + SLC_LEN) - jnp.maximum(c0, s0)
    share = jnp.clip(overlap, 0, CMP_LEN).astype(jnp.float32) / CMP_LEN
    imp = jnp.einsum('bgrtc,cj->bgtj', p_cmp, share)
    cur = (jnp.arange(t) // SLC_LEN)[:, None]
    j = jnp.arange(nb)[None, :]
    forced = (j == 0) | (j == cur) | (j == cur - 1)
    score = jnp.where(forced, FORCE_SCORE, imp)
    score = jnp.where(j > cur, -jnp.inf, score)
    _, idx = lax.top_k(score, min(SLC_TOPK, nb))
    return idx


def nsa_selected_attn(q, k, v, idx, slopes):
    b, g, r, t, d = q.shape
    nb = t // SLC_LEN
    n = idx.shape[-1]
    nq = t // SEL_QBLOCK
    kb = k.reshape(b, g, nb, SLC_LEN, d)
    vb = v.reshape(b, g, nb, SLC_LEN, d)
    qs = q.reshape(b, g, r, nq, SEL_QBLOCK, d).transpose(3, 0, 1, 2, 4, 5)
    ids = idx.reshape(b, g, nq, SEL_QBLOCK, n).transpose(2, 0, 1, 3, 4)
    ts = jnp.arange(t).reshape(nq, SEL_QBLOCK)
    bi = jnp.arange(b)[:, None, None, None]
    gi = jnp.arange(g)[None, :, None, None]
    scale = HEAD_DIM ** -0.5

    def one_block(args):
        qb, ib, tb = args
        kg = kb[bi, gi, ib]
        vg = vb[bi, gi, ib].reshape(b, g, SEL_QBLOCK, n * SLC_LEN, d)
        pos = ib[..., None] * SLC_LEN + jnp.arange(SLC_LEN)
        dist = (tb[None, None, :, None, None] - pos).reshape(b, g, 1, SEL_QBLOCK, n * SLC_LEN)
        s = jnp.einsum('bgrqd,bgqnld->bgrqnl', qb, kg).astype(jnp.float32) * scale
        s = s.reshape(b, g, r, SEL_QBLOCK, n * SLC_LEN)
        s = s - slopes[None, :, :, None, None] * dist.astype(jnp.float32)
        p = masked_softmax(s, dist >= 0)
        return jnp.einsum('bgrqk,bgqkd->bgrqd', p.astype(vg.dtype), vg)

    out = lax.map(one_block, (qs, ids, ts))
    return out.transpose(1, 2, 3, 0, 4, 5).reshape(b, g, r, t, d)


def nsa_window_attn(q, k, v, slopes):
    b, g, r, t, d = q.shape
    nq = t // WIN_QBLOCK
    span = WIN + WIN_QBLOCK
    band = jnp.arange(nq)[:, None] * WIN_QBLOCK + jnp.arange(span)[None, :]
    pad = ((0, 0), (0, 0), (WIN, 0), (0, 0))
    kb = jnp.pad(k, pad)[:, :, band]
    vb = jnp.pad(v, pad)[:, :, band]
    spos = band - WIN
    tpos = jnp.arange(t).reshape(nq, WIN_QBLOCK)
    dist = tpos[:, :, None] - spos[:, None, :]
    mask = (spos[:, None, :] >= 0) & (dist >= 0) & (dist < WIN)
    qb = q.reshape(b, g, r, nq, WIN_QBLOCK, d)
    s = jnp.einsum('bgrnqd,bgnkd->bgrnqk', qb, kb).astype(jnp.float32) * (HEAD_DIM ** -0.5)
    s = s - slopes[None, :, :, None, None, None] * dist.astype(jnp.float32)
    p = masked_softmax(s, mask)
    o = jnp.einsum('bgrnqk,bgnkd->bgrnqd', p.astype(vb.dtype), vb)
    return o.reshape(b, g, r, t, d)


def nsa_mixer(q, k_cmp, v_cmp, k_slc, v_slc, k_win, v_win, gates, pe_k, w1_k, w2_k, pe_v, w1_v, w2_v):
    slopes = alibi_slopes()
    kc = nsa_compress(k_cmp, pe_k, w1_k, w2_k)
    vc = nsa_compress(v_cmp, pe_v, w1_v, w2_v)
    o_cmp, p_cmp = nsa_compressed_attn(q, kc, vc, slopes)
    idx = nsa_select_blocks(p_cmp, q.shape[3])
    o_slc = nsa_selected_attn(q, k_slc, v_slc, idx, slopes)
    o_win = nsa_window_attn(q, k_win, v_win, slopes)
    return gates[..., 0:1] * o_cmp + gates[..., 1:2] * o_slc + gates[..., 2:3] * o_win


def gmlp_gate(uv, ln_g, ln_b, ws, bs):
    b, t, _ = uv.shape
    u, v = jnp.split(jax.nn.gelu(uv), 2, axis=-1)
    v = layernorm(v, ln_g, ln_b).reshape(b, t // GMLP_CHUNK, GMLP_CHUNK, GMLP_GROUPS, HEAD_DIM)
    w = ws * jnp.tril(jnp.ones((GMLP_CHUNK, GMLP_CHUNK), ws.dtype))
    s = jnp.einsum('gts,bcsgd->bctgd', w, v) + bs.T[None, None, :, :, None]
    return u * s.reshape(b, t, GMLP_WIDTH)


def retention(q, k, v, gate, gn_g):
    b, t, _ = q.shape
    h, d, c = RET_HEADS, HEAD_DIM, RET_CHUNK
    nc = t // c
    dt = q.dtype
    qc = to_heads(q, h).reshape(b, h, nc, c, d)
    kc = (to_heads(k, h) * (d ** -0.5)).reshape(b, h, nc, c, d)
    vc = to_heads(v, h).reshape(b, h, nc, c, d)
    lg = jnp.log1p(-jnp.exp2(-5.0 - jnp.arange(h, dtype=jnp.float32)))
    n = jnp.arange(c, dtype=jnp.float32)
    rel = n[:, None] - n[None, :]
    decay_in = jnp.where(rel >= 0, jnp.exp(lg[:, None, None] * jnp.maximum(rel, 0.0)), 0.0).astype(dt)
    zeta = jnp.exp(lg[:, None] * (c - 1.0 - n)).astype(dt)
    xi = jnp.exp(lg[:, None] * (n + 1.0)).astype(dt)
    decay_chunk = jnp.exp(lg * c).astype(dt)
    scores = jnp.einsum('bhcnd,bhcmd->bhcnm', qc, kc) * decay_in[None, :, None]
    inner = jnp.einsum('bhcnm,bhcme->bhcne', scores, vc)
    kv = jnp.einsum('bhcmd,bhcme->cbhde', kc * zeta[None, :, None, :, None], vc)

    def step(state, kv_c):
        return decay_chunk[None, :, None, None] * state + kv_c, state

    _, prev = lax.scan(step, jnp.zeros((b, h, d, d), dt), kv)
    cross = jnp.einsum('bhcnd,cbhde->bhcne', qc, prev) * xi[None, :, None, :, None]
    y = (inner + cross).reshape(b, h, t, d).astype(jnp.float32)
    mu = jnp.mean(y, axis=-1, keepdims=True)
    var = jnp.mean(jnp.square(y - mu), axis=-1, keepdims=True)
    y = (y - mu) * lax.rsqrt(var + EPS)
    y = y.transpose(0, 2, 1, 3).reshape(b, t, h * d) * gn_g.astype(jnp.float32)
    return jax.nn.silu(gate) * y.astype(dt)


def token_mixer(a, w_in, pe_k, w1_k, w2_k, pe_v, w1_v, w2_v, g_ln_g, g_ln_b, g_ws, g_bs, ret_gn_g, w_out):
    b, t, _ = a.shape
    proj = a @ w_in
    parts = []
    off = 0
    for width in IN_SPLITS:
        parts.append(proj[..., off:off + width])
        off += width
    q, kcmp, vcmp, kslc, vslc, kwin, vwin, gate_logits, uv, rq, rk, rv, rg = parts
    qh = to_heads(q, NSA_HEADS).reshape(b, NSA_KV_HEADS, NSA_GROUP, t, HEAD_DIM)
    gates = jax.nn.sigmoid(gate_logits).reshape(b, t, NSA_KV_HEADS, NSA_GROUP, 3).transpose(0, 2, 3, 1, 4)
    o = nsa_mixer(qh, to_heads(kcmp, NSA_KV_HEADS), to_heads(vcmp, NSA_KV_HEADS),
                  to_heads(kslc, NSA_KV_HEADS), to_heads(vslc, NSA_KV_HEADS),
                  to_heads(kwin, NSA_KV_HEADS), to_heads(vwin, NSA_KV_HEADS),
                  gates, pe_k, w1_k, w2_k, pe_v, w1_v, w2_v)
    o_nsa = o.reshape(b, NSA_HEADS, t, HEAD_DIM).transpose(0, 2, 1, 3).reshape(b, t, NSA_WIDTH)
    o_gmlp = gmlp_gate(uv, g_ln_g, g_ln_b, g_ws, g_bs)
    o_ret = retention(rq, rk, rv, rg, ret_gn_g)
    return jnp.concatenate([o_nsa, o_gmlp, o_ret], axis=-1) @ w_out


def swiglu(x, w1, w3, w2):
    return (jax.nn.silu(x @ w1) * (x @ w3)) @ w2


def moe_swiglu(x, w_r, b_r, w1, w3, w2):
    b, t, d = x.shape
    xf = x.reshape(b * t, d)
    logits = (xf @ w_r + b_r).astype(jnp.float32)
    top_v, top_i = lax.top_k(logits, TOP_K)
    gate = jax.nn.softmax(top_v, axis=-1)
    combine = jnp.sum(jax.nn.one_hot(top_i, N_EXPERTS, dtype=jnp.float32) * gate[..., None], axis=1).astype(x.dtype)
    out = jnp.zeros_like(xf)
    for e in range(N_EXPERTS):
        out = out + combine[:, e:e + 1] * swiglu(xf, w1[e], w3[e], w2[e])
    return out.reshape(b, t, d)


def setup_inputs(seed: int = 0) -> dict:
    key = jax.random.key(seed)
    ks = jax.random.split(key, 32)
    counter = [0]

    def nrm(shape, scale):
        k = ks[counter[0]]
        counter[0] += 1
        return jax.random.normal(k, shape, jnp.float32) * scale

    n_dense = (DEPTH + 1) // 2
    n_moe = DEPTH // 2
    L = DEPTH
    cf = CMP_LEN * HEAD_DIM
    return {
        'x': nrm((BATCH, SEQ, D_MODEL), 1.0),
        'ln1_g': 1.0 + nrm((L, D_MODEL), 0.01),
        'w_in': nrm((L, D_MODEL, IN_WIDTH), D_MODEL ** -0.5),
        'cmp_pe_k': nrm((L, CMP_LEN, HEAD_DIM), 0.1),
        'cmp_w1_k': nrm((L, cf, HEAD_DIM), cf ** -0.5),
        'cmp_w2_k': nrm((L, HEAD_DIM, HEAD_DIM), HEAD_DIM ** -0.5),
        'cmp_pe_v': nrm((L, CMP_LEN, HEAD_DIM), 0.1),
        'cmp_w1_v': nrm((L, cf, HEAD_DIM), cf ** -0.5),
        'cmp_w2_v': nrm((L, HEAD_DIM, HEAD_DIM), HEAD_DIM ** -0.5),
        'gmlp_ln_g': 1.0 + nrm((L, GMLP_WIDTH), 0.01),
        'gmlp_ln_b': nrm((L, GMLP_WIDTH), 0.01),
        'gmlp_ws': nrm((L, GMLP_GROUPS, GMLP_CHUNK, GMLP_CHUNK), GMLP_CHUNK ** -0.5),
        'gmlp_bs': 1.0 + nrm((L, GMLP_GROUPS, GMLP_CHUNK), 0.01),
        'ret_gn_g': 1.0 + nrm((L, RET_WIDTH), 0.01),
        'w_out': nrm((L, MIX_WIDTH, D_MODEL), MIX_WIDTH ** -0.5),
        'ln2_g': 1.0 + nrm((L, D_MODEL), 0.01),
        'ffn_w1': nrm((n_dense, D_MODEL, DENSE_FF), D_MODEL ** -0.5),
        'ffn_w3': nrm((n_dense, D_MODEL, DENSE_FF), D_MODEL ** -0.5),
        'ffn_w2': nrm((n_dense, DENSE_FF, D_MODEL), DENSE_FF ** -0.5),
        'moe_wr': nrm((n_moe, D_MODEL, N_EXPERTS), D_MODEL ** -0.5),
        'moe_br': nrm((n_moe, N_EXPERTS), 0.01),
        'moe_w1': nrm((n_moe, N_EXPERTS, D_MODEL, MOE_FF), D_MODEL ** -0.5),
        'moe_w3': nrm((n_moe, N_EXPERTS, D_MODEL, MOE_FF), D_MODEL ** -0.5),
        'moe_w2': nrm((n_moe, N_EXPERTS, MOE_FF, D_MODEL), MOE_FF ** -0.5),
        'final_g': 1.0 + nrm((D_MODEL,), 0.01),
    }


def reference(x, ln1_g, w_in, cmp_pe_k, cmp_w1_k, cmp_w2_k, cmp_pe_v, cmp_w1_v, cmp_w2_v,
              gmlp_ln_g, gmlp_ln_b, gmlp_ws, gmlp_bs, ret_gn_g, w_out, ln2_g,
              ffn_w1, ffn_w3, ffn_w2, moe_wr, moe_br, moe_w1, moe_w3, moe_w2, final_g):
    h = x
    for layer in range(DEPTH):
        a = rmsnorm(h, ln1_g[layer])
        h = h + token_mixer(a, w_in[layer], cmp_pe_k[layer], cmp_w1_k[layer], cmp_w2_k[layer],
                            cmp_pe_v[layer], cmp_w1_v[layer], cmp_w2_v[layer],
                            gmlp_ln_g[layer], gmlp_ln_b[layer], gmlp_ws[layer], gmlp_bs[layer],
                            ret_gn_g[layer], w_out[layer])
        f = rmsnorm(h, ln2_g[layer])
        i = layer // 2
        if layer % 2 == 0:
            h = h + swiglu(f, ffn_w1[i], ffn_w3[i], ffn_w2[i])
        else:
            h = h + moe_swiglu(f, moe_wr[i], moe_br[i], moe_w1[i], moe_w3[i], moe_w2[i])
    return rmsnorm(h, final_g)
```

```python
import functools

import numpy as np
import jax
import jax.numpy as jnp
from jax import lax
from jax.experimental import pallas as pl
from jax.experimental.pallas import tpu as pltpu

F32 = jnp.float32
BF16 = jnp.bfloat16

HEAD_DIM = 128
NSA_HEADS = 8
NSA_KV_HEADS = 2
NSA_GROUP = NSA_HEADS // NSA_KV_HEADS
NSA_WIDTH = NSA_HEADS * HEAD_DIM
NSA_KV_WIDTH = NSA_KV_HEADS * HEAD_DIM
CMP_LEN = 32
CMP_STRIDE = 16
SLC_LEN = 64
SLC_TOPK = 16
WIN = 512
GMLP_GROUPS = 4
GMLP_CHUNK = 128
GMLP_WIDTH = GMLP_GROUPS * HEAD_DIM
RET_HEADS = 4
RET_CHUNK = 128
RET_WIDTH = RET_HEADS * HEAD_DIM
N_EXPERTS = 8
TOP_K = 2
EPS = 1e-6
NEG_INF = -1e30
FORCE_SCORE = 1e4

LANES = 128
VMEM_LIMIT = 56 * 1024 * 1024

COL_Q = 0
COL_KCMP = 8
COL_VCMP = 10
COL_KSLC = 12
COL_VSLC = 14
COL_KWIN = 16
COL_VWIN = 18
COL_U = 20
COL_V = 24
COL_RQ = 28
COL_RK = 32
COL_RV = 36
COL_RG = 40
COL_GATE = 44
PROJ_COLS = 48 * LANES

XL_HI = 64
XL_LO = 65
XL_ONE_A = 66
XL_ONE_B = 67


def _cparams(*sem):
    return pltpu.CompilerParams(dimension_semantics=sem, vmem_limit_bytes=VMEM_LIMIT)


def _nt_dot(a, b):
    return lax.dot_general(a, b, (((1,), (1,)), ((), ())), preferred_element_type=F32)


def _dot(a, b):
    return jnp.dot(a, b, preferred_element_type=F32)


def _in_proj_kernel(x_ref, g_ref, w_ref, cs_ref, o_ref, xn_ref):
    @pl.when(pl.program_id(1) == 0)
    def _():
        x = x_ref[...]
        ms = jnp.mean(x * x, axis=-1, keepdims=True)
        xn_ref[...] = (x * lax.rsqrt(ms + EPS) * g_ref[...]).astype(BF16)

    acc = _dot(xn_ref[...], w_ref[...])
    o_ref[...] = (acc * cs_ref[...]).astype(o_ref.dtype)


def in_proj(h2, g, w, colscale, tm=1024, tn=1024):
    n, d = h2.shape
    cols = w.shape[1]
    tm = min(tm, n)
    return pl.pallas_call(
        _in_proj_kernel,
        grid=(n // tm, cols // tn),
        in_specs=[
            pl.BlockSpec((tm, d), lambda i, j: (i, 0)),
            pl.BlockSpec((1, d), lambda i, j: (0, 0)),
            pl.BlockSpec((d, tn), lambda i, j: (0, j)),
            pl.BlockSpec((1, tn), lambda i, j: (0, j)),
        ],
        out_specs=pl.BlockSpec((tm, tn), lambda i, j: (i, j)),
        out_shape=jax.ShapeDtypeStruct((n, cols), BF16),
        scratch_shapes=[pltpu.VMEM((tm, d), BF16)],
        compiler_params=_cparams("parallel", "arbitrary"),
        name="in_proj",
    )(h2, g, w, colscale)


def _compress_kernel(r_ref, pe_ref, w1_ref, w2_ref, o_ref):
    half = CMP_STRIDE * HEAD_DIM
    r = r_ref[0, 0].astype(F32)
    nrow = r.shape[0]
    pe = pe_ref[0]
    a = _dot((r + pe[:, :half]).astype(BF16), w1_ref[0, :half, :])
    b = _dot((r + pe[:, half:]).astype(BF16), w1_ref[0, half:, :])
    pre = a + pltpu.roll(b, nrow - 1, 0)
    y = _dot(jax.nn.gelu(pre).astype(BF16), w2_ref[0])
    row = lax.broadcasted_iota(jnp.int32, y.shape, 0)
    o_ref[0, 0] = jnp.where(row < nrow - 1, y, 0.0).astype(o_ref.dtype)


def compress(rows, pe, w1, w2):
    _, bg, nrow, width = rows.shape
    return pl.pallas_call(
        _compress_kernel,
        grid=(2, bg),
        in_specs=[
            pl.BlockSpec((1, 1, nrow, width), lambda s, i: (s, i, 0, 0)),
            pl.BlockSpec((1, 1, 2 * width), lambda s, i: (s, 0, 0)),
            pl.BlockSpec((1, 2 * width, HEAD_DIM), lambda s, i: (s, 0, 0)),
            pl.BlockSpec((1, HEAD_DIM, HEAD_DIM), lambda s, i: (s, 0, 0)),
        ],
        out_specs=pl.BlockSpec((1, 1, nrow, HEAD_DIM), lambda s, i: (s, i, 0, 0)),
        out_shape=jax.ShapeDtypeStruct((2, bg, nrow, HEAD_DIM), BF16),
        compiler_params=_cparams("arbitrary", "arbitrary"),
        name="compress",
    )(rows, pe, w1, w2)


def _group_slope(g, r):
    return jnp.where(g == 0, F32(2.0 ** -(r + 1)), F32(2.0 ** -(r + 1 + NSA_GROUP)))


def _cmp_attn_kernel(q_ref, kc_ref, vc_ref, sht_ref, o_ref, x_ref, *, tq, nb, topk):
    g = pl.program_id(0) % NSA_KV_HEADS
    t0 = pl.program_id(1) * tq
    kc = kc_ref[0, 0]
    vc = vc_ref[0, 0]
    nrow = kc.shape[0]
    tpos = t0 + lax.broadcasted_iota(jnp.int32, (tq, nrow), 0)
    col = lax.broadcasted_iota(jnp.int32, (tq, nrow), 1)
    dist = tpos - (col * CMP_STRIDE + (CMP_LEN - 1))
    valid = (dist >= 0) & (col < nrow - 1)
    distf = dist.astype(F32)
    psum = jnp.zeros((tq, nrow), F32)
    for r in range(NSA_GROUP):
        q = q_ref[:, r * HEAD_DIM:(r + 1) * HEAD_DIM]
        s = _nt_dot(q, kc) - _group_slope(g, r) * distf
        s = jnp.where(valid, s, NEG_INF)
        e = jnp.exp(s - jnp.max(s, axis=-1, keepdims=True))
        p = e / jnp.sum(e, axis=-1, keepdims=True)
        p = jnp.where(valid, p, 0.0)
        o_ref[:, r * HEAD_DIM:(r + 1) * HEAD_DIM] = _dot(p.astype(BF16), vc).astype(o_ref.dtype)
        psum = psum + p
    p_hi = psum.astype(BF16)
    p_lo = (psum - p_hi.astype(F32)).astype(BF16)
    sht = sht_ref[...]
    imp = _nt_dot(sht, p_hi) + _nt_dot(sht, p_lo)
    j = lax.broadcasted_iota(jnp.int32, (nb, tq), 0)
    cur = (t0 + lax.broadcasted_iota(jnp.int32, (nb, tq), 1)) // SLC_LEN
    forced = (j == 0) | (j == cur) | (j == cur - 1)
    score = jnp.where(forced, FORCE_SCORE, imp)
    score = jnp.where(j > cur, -jnp.inf, score)
    rank = jnp.zeros((nb, tq), F32)
    for k in range(nb):
        sk = score[k:k + 1, :]
        rank = rank + jnp.where(j > k, jnp.where(sk >= score, 1.0, 0.0), jnp.where(sk > score, 1.0, 0.0))
    neg = jnp.where(rank < topk, 0.0, NEG_INF)
    neg = jnp.concatenate([neg, jnp.zeros((LANES - nb, tq), F32)], axis=0)
    x_ref[...] = neg.T.astype(x_ref.dtype)


def cmp_attn(proj, kvc, sht, batch, seq, tq=256):
    n = proj.shape[0]
    nb = seq // SLC_LEN
    bg = batch * NSA_KV_HEADS
    nrow = kvc.shape[2]
    qt = seq // tq
    rowblk = lambda i, t: ((i // NSA_KV_HEADS) * qt + t, i % NSA_KV_HEADS)
    kern = functools.partial(_cmp_attn_kernel, tq=tq, nb=nb, topk=min(SLC_TOPK, nb))
    return pl.pallas_call(
        kern,
        grid=(bg, qt),
        in_specs=[
            pl.BlockSpec((tq, NSA_GROUP * HEAD_DIM), rowblk),
            pl.BlockSpec((1, 1, nrow, HEAD_DIM), lambda i, t: (0, i, 0, 0)),
            pl.BlockSpec((1, 1, nrow, HEAD_DIM), lambda i, t: (1, i, 0, 0)),
            pl.BlockSpec((nb, nrow), lambda i, t: (0, 0)),
        ],
        out_specs=[
            pl.BlockSpec((tq, NSA_GROUP * HEAD_DIM), rowblk),
            pl.BlockSpec((tq, LANES), rowblk),
        ],
        out_shape=[
            jax.ShapeDtypeStruct((n, NSA_WIDTH), BF16),
            jax.ShapeDtypeStruct((n, NSA_KV_HEADS * LANES), BF16),
        ],
        compiler_params=_cparams("parallel", "parallel"),
        name="cmp_attn",
    )(proj, kvc, kvc, sht)


def _alibi_lanes(slope, tpos_lane, lane):
    hi_t = (tpos_lane // SLC_LEN).astype(F32)
    lo_t = (tpos_lane % SLC_LEN).astype(F32)
    ax = jnp.where(lane == XL_HI, slope * SLC_LEN, 0.0)
    ax = jnp.where(lane == XL_LO, slope, ax)
    ax = jnp.where(lane == XL_ONE_A, -(slope * SLC_LEN) * hi_t, ax)
    ax = jnp.where(lane == XL_ONE_B, -slope * lo_t, ax)
    return ax


def _sel_attn_kernel(q_ref, x_ref, k_ref, v_ref, kx_ref, o_ref, m_ref, l_ref, acc_ref, *, tq, tk):
    g = pl.program_id(0) % NSA_KV_HEADS
    t0 = pl.program_id(1) * tq
    n_kt = (t0 + tq - 1) // tk + 1
    lane = lax.broadcasted_iota(jnp.int32, (tq, LANES), 1)
    tpos_lane = t0 + lax.broadcasted_iota(jnp.int32, (tq, LANES), 0)
    xsel = x_ref[...].astype(F32)
    tpos = t0 + lax.broadcasted_iota(jnp.int32, (tq, tk), 0)
    kcol = lax.broadcasted_iota(jnp.int32, (tq, tk), 1)

    for r in range(NSA_GROUP):
        ax = _alibi_lanes(_group_slope(g, r), tpos_lane, lane)
        q2 = jnp.concatenate(
            [q_ref[:, r * HEAD_DIM:(r + 1) * HEAD_DIM],
             jnp.where(lane < SLC_LEN, xsel, ax).astype(BF16)], axis=1)
        m_ref[...] = jnp.full(m_ref.shape, -jnp.inf, F32)
        l_ref[...] = jnp.zeros(l_ref.shape, F32)
        acc_ref[...] = jnp.zeros(acc_ref.shape, F32)

        def step(kt, causal):
            ks = pl.multiple_of(kt * tk, tk)
            k2 = jnp.concatenate([k_ref[pl.ds(ks, tk), :], kx_ref[pl.ds(ks, tk), :]], axis=1)
            s = _nt_dot(q2, k2)
            if causal:
                s = jnp.where(ks + kcol <= tpos, s, NEG_INF)
            m_prev = m_ref[...]
            m_new = jnp.maximum(m_prev, jnp.max(s, axis=-1, keepdims=True))
            alpha = jnp.exp(m_prev - m_new)
            p = jnp.exp(s - m_new)
            l_ref[...] = alpha * l_ref[...] + jnp.sum(p, axis=-1, keepdims=True)
            acc_ref[...] = alpha * acc_ref[...] + _dot(p.astype(BF16), v_ref[pl.ds(ks, tk), :])
            m_ref[...] = m_new

        def body(kt, carry):
            step(kt, False)
            return carry

        lax.fori_loop(0, n_kt - 1, body, 0)
        step(n_kt - 1, True)
        o_ref[:, r * HEAD_DIM:(r + 1) * HEAD_DIM] = (acc_ref[...] / l_ref[...]).astype(o_ref.dtype)


def sel_attn(proj, xsel, kx, batch, seq, tq=256, tk=512):
    n = proj.shape[0]
    tk = min(tk, seq)
    bg = batch * NSA_KV_HEADS
    qt = seq // tq
    rowblk = lambda i, t: ((i // NSA_KV_HEADS) * qt + t, i % NSA_KV_HEADS)
    kern = functools.partial(_sel_attn_kernel, tq=tq, tk=tk)
    return pl.pallas_call(
        kern,
        grid=(bg, qt),
        in_specs=[
            pl.BlockSpec((tq, NSA_GROUP * HEAD_DIM), rowblk),
            pl.BlockSpec((tq, LANES), rowblk),
            pl.BlockSpec((seq, HEAD_DIM), lambda i, t: (i // NSA_KV_HEADS, COL_KSLC + i % NSA_KV_HEADS)),
            pl.BlockSpec((seq, HEAD_DIM), lambda i, t: (i // NSA_KV_HEADS, COL_VSLC + i % NSA_KV_HEADS)),
            pl.BlockSpec((seq, LANES), lambda i, t: (0, 0)),
        ],
        out_specs=pl.BlockSpec((tq, NSA_GROUP * HEAD_DIM), rowblk),
        out_shape=jax.ShapeDtypeStruct((n, NSA_WIDTH), BF16),
        scratch_shapes=[
            pltpu.VMEM((tq, 1), F32),
            pltpu.VMEM((tq, 1), F32),
            pltpu.VMEM((tq, HEAD_DIM), F32),
        ],
        compiler_params=_cparams("parallel", "parallel"),
        name="sel_attn",
    )(proj, xsel, proj, proj, kx)


def _win_attn_kernel(q_ref, k_ref, v_ref, kx_ref, gl_ref, oc_ref, os_ref, o_ref, *, tq):
    g = pl.program_id(0) % NSA_KV_HEADS
    t0 = pl.program_id(1) * tq
    p0 = pl.multiple_of(jnp.maximum(t0 - WIN, 0), LANES)
    d0 = pl.multiple_of(t0, LANES)
    k2 = jnp.concatenate([
        jnp.concatenate([k_ref[pl.ds(p0, WIN), :], kx_ref[pl.ds(p0, WIN), :]], axis=1),
        jnp.concatenate([k_ref[pl.ds(d0, tq), :], kx_ref[pl.ds(d0, tq), :]], axis=1)], axis=0)
    v2 = jnp.concatenate([v_ref[pl.ds(p0, WIN), :], v_ref[pl.ds(d0, tq), :]], axis=0)
    span = WIN + tq
    tpos = t0 + lax.broadcasted_iota(jnp.int32, (tq, span), 0)
    kpos = (t0 - WIN) + lax.broadcasted_iota(jnp.int32, (tq, span), 1)
    dist = tpos - kpos
    bias = jnp.where((kpos >= 0) & (dist >= 0) & (dist < WIN), 0.0, NEG_INF)
    lane = lax.broadcasted_iota(jnp.int32, (tq, LANES), 1)
    tpos_lane = t0 + lax.broadcasted_iota(jnp.int32, (tq, LANES), 0)
    gates = jax.nn.sigmoid(gl_ref[...].astype(F32))
    for r in range(NSA_GROUP):
        ax = _alibi_lanes(_group_slope(g, r), tpos_lane, lane)
        cols = slice(r * HEAD_DIM, (r + 1) * HEAD_DIM)
        q2 = jnp.concatenate([q_ref[:, cols], ax.astype(BF16)], axis=1)
        s = _nt_dot(q2, k2) + bias
        e = jnp.exp(s - jnp.max(s, axis=-1, keepdims=True))
        o_win = _dot(e.astype(BF16), v2) / jnp.sum(e, axis=-1, keepdims=True)
        o = (gates[:, 3 * r:3 * r + 1] * oc_ref[:, cols].astype(F32)
             + gates[:, 3 * r + 1:3 * r + 2] * os_ref[:, cols].astype(F32)
             + gates[:, 3 * r + 2:3 * r + 3] * o_win)
        o_ref[:, cols] = o.astype(o_ref.dtype)


def win_attn(proj, kx, o_cmp, o_slc, batch, seq, tq=512):
    n = proj.shape[0]
    bg = batch * NSA_KV_HEADS
    qt = seq // tq
    rowblk = lambda i, t: ((i // NSA_KV_HEADS) * qt + t, i % NSA_KV_HEADS)
    kern = functools.partial(_win_attn_kernel, tq=tq)
    return pl.pallas_call(
        kern,
        grid=(bg, qt),
        in_specs=[
            pl.BlockSpec((tq, NSA_GROUP * HEAD_DIM), rowblk),
            pl.BlockSpec((seq, HEAD_DIM), lambda i, t: (i // NSA_KV_HEADS, COL_KWIN + i % NSA_KV_HEADS)),
            pl.BlockSpec((seq, HEAD_DIM), lambda i, t: (i // NSA_KV_HEADS, COL_VWIN + i % NSA_KV_HEADS)),
            pl.BlockSpec((seq, LANES), lambda i, t: (0, 0)),
            pl.BlockSpec((tq, LANES), lambda i, t: ((i // NSA_KV_HEADS) * qt + t, COL_GATE + i % NSA_KV_HEADS)),
            pl.BlockSpec((tq, NSA_GROUP * HEAD_DIM), rowblk),
            pl.BlockSpec((tq, NSA_GROUP * HEAD_DIM), rowblk),
        ],
        out_specs=pl.BlockSpec((tq, NSA_GROUP * HEAD_DIM), rowblk),
        out_shape=jax.ShapeDtypeStruct((n, NSA_WIDTH), BF16),
        compiler_params=_cparams("parallel", "parallel"),
        name="win_attn",
    )(proj, proj, proj, kx, proj, o_cmp, o_slc)


def _gmlp_kernel(u_ref, v_ref, lg_ref, lb_ref, ws_ref, bsx_ref, o_ref, *, tm):
    u = jax.nn.gelu(u_ref[...].astype(F32))
    v = jax.nn.gelu(v_ref[...].astype(F32))
    mu = jnp.mean(v, axis=-1, keepdims=True)
    vc = v - mu
    var = jnp.mean(vc * vc, axis=-1, keepdims=True)
    vn = (vc * lax.rsqrt(var + EPS) * lg_ref[...] + lb_ref[...]).astype(BF16)
    c = GMLP_CHUNK
    tri = (lax.broadcasted_iota(jnp.int32, (c, c), 0) >= lax.broadcasted_iota(jnp.int32, (c, c), 1))
    bsx = bsx_ref[...]
    for gi in range(GMLP_GROUPS):
        w = jnp.where(tri, ws_ref[gi], 0.0).astype(BF16)
        cols = slice(gi * HEAD_DIM, (gi + 1) * HEAD_DIM)
        for ci in range(tm // c):
            rows = slice(ci * c, (ci + 1) * c)
            s = _dot(w, vn[rows, cols]) + bsx[:, cols]
            o_ref[rows, cols] = (u[rows, cols] * s).astype(o_ref.dtype)


def gmlp(proj, ln_g, ln_b, ws, bsx, tm=512):
    n = proj.shape[0]
    ublk = GMLP_WIDTH // LANES
    kern = functools.partial(_gmlp_kernel, tm=tm)
    return pl.pallas_call(
        kern,
        grid=(n // tm,),
        in_specs=[
            pl.BlockSpec((tm, GMLP_WIDTH), lambda i: (i, COL_U // ublk)),
            pl.BlockSpec((tm, GMLP_WIDTH), lambda i: (i, COL_V // ublk)),
            pl.BlockSpec((1, GMLP_WIDTH), lambda i: (0, 0)),
            pl.BlockSpec((1, GMLP_WIDTH), lambda i: (0, 0)),
            pl.BlockSpec((GMLP_GROUPS, GMLP_CHUNK, GMLP_CHUNK), lambda i: (0, 0, 0)),
            pl.BlockSpec((GMLP_CHUNK, GMLP_WIDTH), lambda i: (0, 0)),
        ],
        out_specs=pl.BlockSpec((tm, GMLP_WIDTH), lambda i: (i, 0)),
        out_shape=jax.ShapeDtypeStruct((n, GMLP_WIDTH), BF16),
        compiler_params=_cparams("parallel"),
        name="gmlp",
    )(proj, proj, ln_g, ln_b, ws, bsx)


def _retention_kernel(q_ref, k_ref, v_ref, g_ref, dec_ref, zeta_ref, xi_ref, dch_ref, gn_ref,
                      o_ref, s_ref, *, seq):
    c = RET_CHUNK
    s_ref[...] = jnp.zeros(s_ref.shape, F32)

    def body(ci, carry):
        rows = pl.ds(pl.multiple_of(ci * c, c), c)
        for hh in range(RET_HEADS):
            cols = slice(hh * HEAD_DIM, (hh + 1) * HEAD_DIM)
            q = q_ref[rows, cols]
            k = k_ref[rows, cols]
            v = v_ref[rows, cols]
            state = s_ref[hh]
            scores = _nt_dot(q, k) * dec_ref[hh]
            y = _dot(scores.astype(BF16), v) + _dot(q, state.astype(BF16)) * xi_ref[hh]
            kz = (k.astype(F32) * zeta_ref[hh]).astype(BF16)
            kv = lax.dot_general(kz, v, (((0,), (0,)), ((), ())), preferred_element_type=F32)
            s_ref[hh] = dch_ref[hh] * state + kv
            mu = jnp.mean(y, axis=-1, keepdims=True)
            yc = y - mu
            var = jnp.mean(yc * yc, axis=-1, keepdims=True)
            yn = yc * lax.rsqrt(var + EPS) * gn_ref[:, cols]
            o_ref[rows, cols] = (jax.nn.silu(g_ref[rows, cols].astype(F32)) * yn).astype(o_ref.dtype)
        return carry

    lax.fori_loop(0, seq // c, body, 0)


def retention(proj, tabs, gn_g, batch, seq):
    n = proj.shape[0]
    dec, zeta, xi, dch = tabs
    rblk = RET_WIDTH // LANES
    kern = functools.partial(_retention_kernel, seq=seq)
    full = lambda shape: pl.BlockSpec(shape, lambda b: (0,) * len(shape))
    return pl.pallas_call(
        kern,
        grid=(batch,),
        in_specs=[
            pl.BlockSpec((seq, RET_WIDTH), lambda b: (b, COL_RQ // rblk)),
            pl.BlockSpec((seq, RET_WIDTH), lambda b: (b, COL_RK // rblk)),
            pl.BlockSpec((seq, RET_WIDTH), lambda b: (b, COL_RV // rblk)),
            pl.BlockSpec((seq, RET_WIDTH), lambda b: (b, COL_RG // rblk)),
            full(dec.shape), full(zeta.shape), full(xi.shape), full(dch.shape),
            full((1, RET_WIDTH)),
        ],
        out_specs=pl.BlockSpec((seq, RET_WIDTH), lambda b: (b, 0)),
        out_shape=jax.ShapeDtypeStruct((n, RET_WIDTH), BF16),
        scratch_shapes=[pltpu.VMEM((RET_HEADS, HEAD_DIM, HEAD_DIM), F32)],
        compiler_params=_cparams("parallel"),
        name="retention",
    )(proj, proj, proj, proj, dec, zeta, xi, dch, gn_g)


def _top2_combine(logits):
    lane = lax.broadcasted_iota(jnp.int32, logits.shape, 1)
    lg = jnp.where(lane < N_EXPERTS, logits, -jnp.inf)
    v1 = jnp.max(lg, axis=-1, keepdims=True)
    i1 = jnp.min(jnp.where(lg == v1, lane, LANES), axis=-1, keepdims=True)
    lg2 = jnp.where(lane == i1, -jnp.inf, lg)
    v2 = jnp.max(lg2, axis=-1, keepdims=True)
    i2 = jnp.min(jnp.where(lg2 == v2, lane, LANES), axis=-1, keepdims=True)
    e2 = jnp.exp(v2 - v1)
    den = 1.0 + e2
    return jnp.where(lane == i1, 1.0 / den, jnp.where(lane == i2, e2 / den, 0.0))


def _out_proj_kernel(*refs, router):
    if router:
        nsa_ref, gm_ref, ret_ref, h_ref, w_ref, g2_ref, wrh_ref, wrl_ref, br_ref, ho_ref, f_ref, cb_ref = refs
    else:
        nsa_ref, gm_ref, ret_ref, h_ref, w_ref, g2_ref, ho_ref, f_ref = refs
    acc = _dot(nsa_ref[...], w_ref[:NSA_WIDTH, :])
    acc = acc + _dot(gm_ref[...], w_ref[NSA_WIDTH:NSA_WIDTH + GMLP_WIDTH, :])
    acc = acc + _dot(ret_ref[...], w_ref[NSA_WIDTH + GMLP_WIDTH:, :])
    hn = h_ref[...] + acc
    ho_ref[...] = hn
    ms = jnp.mean(hn * hn, axis=-1, keepdims=True)
    f = hn * lax.rsqrt(ms + EPS) * g2_ref[...]
    f_hi = f.astype(BF16)
    f_ref[...] = f_hi
    if router:
        f_lo = (f - f_hi.astype(F32)).astype(BF16)
        logits = (_dot(f_hi, wrh_ref[...]) + _dot(f_lo, wrh_ref[...]) + _dot(f_hi, wrl_ref[...])
                  + br_ref[...])
        cb_ref[...] = _top2_combine(logits)


def out_proj(o_nsa, o_gm, o_ret, h2, w, g2, router_w=None, tm=512):
    n, d = h2.shape
    tm = min(tm, n)
    router = router_w is not None
    row = lambda width: pl.BlockSpec((tm, width), lambda i: (i, 0))
    full = lambda shape: pl.BlockSpec(shape, lambda i: (0,) * len(shape))
    in_specs = [row(NSA_WIDTH), row(GMLP_WIDTH), row(RET_WIDTH), row(d), full(w.shape), full((1, d))]
    args = [o_nsa, o_gm, o_ret, h2, w, g2]
    out_specs = [row(d), row(d)]
    out_shape = [jax.ShapeDtypeStruct((n, d), F32), jax.ShapeDtypeStruct((n, d), BF16)]
    if router:
        in_specs += [full((d, LANES)), full((d, LANES)), full((1, LANES))]
        args += list(router_w)
        out_specs.append(row(LANES))
        out_shape.append(jax.ShapeDtypeStruct((n, LANES), F32))
    return pl.pallas_call(
        functools.partial(_out_proj_kernel, router=router),
        grid=(n // tm,),
        in_specs=in_specs,
        out_specs=out_specs,
        out_shape=out_shape,
        compiler_params=_cparams("parallel"),
        name="out_proj_router" if router else "out_proj",
    )(*args)


def _ffn_kernel(*refs, n_exp, final):
    refs = list(refs)
    x_ref, h_ref, w1_ref, w3_ref, w2_ref = refs[:5]
    rest = refs[5:]
    cb_ref = rest.pop(0) if n_exp > 1 else None
    fg_ref = rest.pop(0) if final else None
    (o_ref,) = rest
    e = pl.program_id(1)
    k = pl.program_id(2)

    @pl.when((e == 0) & (k == 0))
    def _():
        o_ref[...] = h_ref[...]

    x = x_ref[...]
    a = _dot(x, w1_ref[0])
    b = _dot(x, w3_ref[0])
    hid = jax.nn.silu(a) * b
    if cb_ref is not None:
        lane = lax.broadcasted_iota(jnp.int32, cb_ref.shape, 1)
        hid = hid * jnp.sum(jnp.where(lane == e, cb_ref[...], 0.0), axis=-1, keepdims=True)
    o_ref[...] += _dot(hid.astype(BF16), w2_ref[0])

    if fg_ref is not None:
        @pl.when((e == n_exp - 1) & (k == pl.num_programs(2) - 1))
        def _():
            y = o_ref[...]
            ms = jnp.mean(y * y, axis=-1, keepdims=True)
            o_ref[...] = y * lax.rsqrt(ms + EPS) * fg_ref[...]


def ffn(f, h2, w1, w3, w2, combine=None, final_g=None, tm=512, tf=None):
    n, d = h2.shape
    n_exp, _, ff = w1.shape
    tm = min(tm, n)
    if tf is None:
        tf = next(c for c in (512, 256, LANES) if ff % c == 0)
    in_specs = [
        pl.BlockSpec((tm, d), lambda i, e, k: (i, 0)),
        pl.BlockSpec((tm, d), lambda i, e, k: (i, 0)),
        pl.BlockSpec((1, d, tf), lambda i, e, k: (e, 0, k)),
        pl.BlockSpec((1, d, tf), lambda i, e, k: (e, 0, k)),
        pl.BlockSpec((1, tf, d), lambda i, e, k: (e, k, 0)),
    ]
    args = [f, h2, w1, w3, w2]
    if n_exp > 1:
        in_specs.append(pl.BlockSpec((tm, LANES), lambda i, e, k: (i, 0)))
        args.append(combine)
    if final_g is not None:
        in_specs.append(pl.BlockSpec((1, d), lambda i, e, k: (0, 0)))
        args.append(final_g)
    return pl.pallas_call(
        functools.partial(_ffn_kernel, n_exp=n_exp, final=final_g is not None),
        grid=(n // tm, n_exp, ff // tf),
        in_specs=in_specs,
        out_specs=pl.BlockSpec((tm, d), lambda i, e, k: (i, 0)),
        out_shape=jax.ShapeDtypeStruct((n, d), F32),
        compiler_params=_cparams("parallel", "arbitrary", "arbitrary"),
        name="moe_ffn" if n_exp > 1 else "ffn",
    )(*args)


def _reorder_w_in(w_in):
    d = w_in.shape[0]
    kvw = NSA_KV_WIDTH
    o_gate = NSA_WIDTH + 6 * kvw
    o_uv = o_gate + 3 * NSA_HEADS
    o_ret = o_uv + 2 * GMLP_WIDTH
    gate = w_in[:, o_gate:o_uv].reshape(d, NSA_KV_HEADS, 3 * NSA_GROUP)
    gate = jnp.pad(gate, ((0, 0), (0, 0), (0, LANES - 3 * NSA_GROUP))).reshape(d, NSA_KV_HEADS * LANES)
    parts = [w_in[:, :o_gate], w_in[:, o_uv:], gate]
    w = jnp.concatenate(parts, axis=1)
    w = jnp.pad(w, ((0, 0), (0, PROJ_COLS - w.shape[1])))
    scale = np.ones((1, PROJ_COLS), np.float32)
    scale[:, COL_Q * LANES:(COL_Q + NSA_HEADS) * LANES] = HEAD_DIM ** -0.5
    scale[:, COL_RK * LANES:(COL_RK + RET_HEADS) * LANES] = HEAD_DIM ** -0.5
    del o_ret
    return w.astype(BF16), jnp.asarray(scale)


def _key_extra_lanes(seq):
    pos = np.arange(seq)
    kx = np.zeros((seq, LANES), np.float32)
    kx[pos, pos // SLC_LEN] = 1.0
    kx[:, XL_HI] = pos // SLC_LEN
    kx[:, XL_LO] = pos % SLC_LEN
    kx[:, XL_ONE_A] = 1.0
    kx[:, XL_ONE_B] = 1.0
    return jnp.asarray(kx, BF16)


def _share_t(seq):
    nrow = seq // CMP_STRIDE
    nb = seq // SLC_LEN
    c0 = np.arange(nrow)[None, :] * CMP_STRIDE
    s0 = np.arange(nb)[:, None] * SLC_LEN
    overlap = np.minimum(c0 + CMP_LEN, s0 + SLC_LEN) - np.maximum(c0, s0)
    share = np.clip(overlap, 0, CMP_LEN).astype(np.float32) / CMP_LEN
    share[:, nrow - 1] = 0.0
    return jnp.asarray(share, BF16)


def _retention_tables():
    hh = np.arange(RET_HEADS, dtype=np.float64)
    lg = np.log1p(-np.exp2(-5.0 - hh))
    nn = np.arange(RET_CHUNK, dtype=np.float64)
    rel = nn[:, None] - nn[None, :]
    dec = np.where(rel >= 0, np.exp(lg[:, None, None] * np.maximum(rel, 0.0)), 0.0)
    zeta = np.exp(lg[:, None] * (RET_CHUNK - 1.0 - nn))[:, :, None]
    xi = np.exp(lg[:, None] * (nn + 1.0))[:, :, None]
    dch = np.broadcast_to(np.exp(lg * RET_CHUNK)[:, None, None], (RET_HEADS, 1, HEAD_DIM))
    return tuple(jnp.asarray(a, F32) for a in (dec, zeta, xi, dch))


def _token_mixer(h2, batch, seq, ln1_g, w_in, pe_k, w1_k, w2_k, pe_v, w1_v, w2_v,
                 g_ln_g, g_ln_b, g_ws, g_bs, ret_gn_g, consts):
    kx, sht, ret_tabs = consts
    w_r, colscale = _reorder_w_in(w_in)
    proj = in_proj(h2, ln1_g[None, :], w_r, colscale)
    nrow = seq // CMP_STRIDE
    kv = proj[:, COL_KCMP * LANES:(COL_VCMP + NSA_KV_HEADS) * LANES]
    kv = kv.reshape(batch, nrow, CMP_STRIDE, 2, NSA_KV_HEADS, HEAD_DIM)
    rows = kv.transpose(3, 0, 4, 1, 2, 5).reshape(2, batch * NSA_KV_HEADS, nrow, CMP_STRIDE * HEAD_DIM)
    pe = jnp.stack([pe_k, pe_v]).reshape(2, 1, CMP_LEN * HEAD_DIM)
    kvc = compress(rows, pe, jnp.stack([w1_k, w1_v]).astype(BF16), jnp.stack([w2_k, w2_v]).astype(BF16))
    o_cmp, xsel = cmp_attn(proj, kvc, sht, batch, seq)
    o_slc = sel_attn(proj, xsel, kx, batch, seq)
    o_nsa = win_attn(proj, kx, o_cmp, o_slc, batch, seq)
    bsx = jnp.repeat(g_bs.T, HEAD_DIM, axis=1)
    o_gm = gmlp(proj, g_ln_g[None, :], g_ln_b[None, :], g_ws, bsx)
    o_ret = retention(proj, ret_tabs, ret_gn_g[None, :], batch, seq)
    return o_nsa, o_gm, o_ret


def kernel(x, ln1_g, w_in, cmp_pe_k, cmp_w1_k, cmp_w2_k, cmp_pe_v, cmp_w1_v, cmp_w2_v, gmlp_ln_g, gmlp_ln_b, gmlp_ws, gmlp_bs, ret_gn_g, w_out, ln2_g, ffn_w1, ffn_w3, ffn_w2, moe_wr, moe_br, moe_w1, moe_w3, moe_w2, final_g):
    batch, seq, d = x.shape
    depth = w_in.shape[0]
    consts = (_key_extra_lanes(seq), _share_t(seq), _retention_tables())
    h2 = x.reshape(batch * seq, d)
    for layer in range(depth):
        o_nsa, o_gm, o_ret = _token_mixer(
            h2, batch, seq, ln1_g[layer], w_in[layer], cmp_pe_k[layer], cmp_w1_k[layer], cmp_w2_k[layer],
            cmp_pe_v[layer], cmp_w1_v[layer], cmp_w2_v[layer], gmlp_ln_g[layer], gmlp_ln_b[layer],
            gmlp_ws[layer], gmlp_bs[layer], ret_gn_g[layer], consts)
        fin = final_g[None, :] if layer == depth - 1 else None
        i = layer // 2
        w_o = w_out[layer].astype(BF16)
        if layer % 2 == 0:
            h2, f = out_proj(o_nsa, o_gm, o_ret, h2, w_o, ln2_g[layer][None, :])
            h2 = ffn(f, h2, ffn_w1[i][None].astype(BF16), ffn_w3[i][None].astype(BF16),
                     ffn_w2[i][None].astype(BF16), final_g=fin)
        else:
            wr = jnp.pad(moe_wr[i], ((0, 0), (0, LANES - N_EXPERTS)))
            wr_hi = wr.astype(BF16)
            wr_lo = (wr - wr_hi.astype(F32)).astype(BF16)
            br = jnp.pad(moe_br[i], (0, LANES - N_EXPERTS))[None, :]
            h2, f, comb = out_proj(o_nsa, o_gm, o_ret, h2, w_o, ln2_g[layer][None, :],
                                   router_w=(wr_hi, wr_lo, br))
            h2 = ffn(f, h2, moe_w1[i].astype(BF16), moe_w3[i].astype(BF16), moe_w2[i].astype(BF16),
                     combine=comb, final_g=fin)
    return h2.reshape(batch, seq, d)
```

```python
import functools

import numpy as np
import jax
import jax.numpy as jnp
from jax import lax
from jax.experimental import pallas as pl
from jax.experimental.pallas import tpu as pltpu

F32 = jnp.float32
BF16 = jnp.bfloat16

HEAD_DIM = 128
NSA_HEADS = 8
NSA_KV_HEADS = 2
NSA_GROUP = NSA_HEADS // NSA_KV_HEADS
NSA_WIDTH = NSA_HEADS * HEAD_DIM
NSA_KV_WIDTH = NSA_KV_HEADS * HEAD_DIM
CMP_LEN = 32
CMP_STRIDE = 16
SLC_LEN = 64
SLC_TOPK = 16
WIN = 512
GMLP_GROUPS = 4
GMLP_CHUNK = 128
GMLP_WIDTH = GMLP_GROUPS * HEAD_DIM
RET_HEADS = 4
RET_CHUNK = 128
RET_WIDTH = RET_HEADS * HEAD_DIM
N_EXPERTS = 8
TOP_K = 2
EPS = 1e-6
NEG_INF = -1e30
FORCE_SCORE = 1e4

LANES = 128
VMEM_LIMIT = 56 * 1024 * 1024

COL_Q = 0
COL_KCMP = 8
COL_VCMP = 10
COL_KSLC = 12
COL_VSLC = 14
COL_KWIN = 16
COL_VWIN = 18
COL_U = 20
COL_V = 24
COL_RQ = 28
COL_RK = 32
COL_RV = 36
COL_RG = 40
COL_GATE = 44
PROJ_COLS = 48 * LANES

XL_HI = 64
XL_LO = 65
XL_ONE_A = 66
XL_ONE_B = 67


def _cparams(*sem):
    return pltpu.CompilerParams(dimension_semantics=sem, vmem_limit_bytes=VMEM_LIMIT)


def _nt_dot(a, b):
    return lax.dot_general(a, b, (((1,), (1,)), ((), ())), preferred_element_type=F32)


def _dot(a, b):
    return jnp.dot(a, b, preferred_element_type=F32)


def _in_proj_kernel(x_ref, g_ref, w_ref, cs_ref, o_ref, xn_ref):
    @pl.when(pl.program_id(1) == 0)
    def _():
        x = x_ref[...]
        ms = jnp.mean(x * x, axis=-1, keepdims=True)
        xn_ref[...] = (x * lax.rsqrt(ms + EPS) * g_ref[...]).astype(BF16)

    acc = _dot(xn_ref[...], w_ref[...])
    o_ref[...] = (acc * cs_ref[...]).astype(o_ref.dtype)


def in_proj(h2, g, w, colscale, tm=1024, tn=1024):
    n, d = h2.shape
    cols = w.shape[1]
    tm = min(tm, n)
    return pl.pallas_call(
        _in_proj_kernel,
        grid=(n // tm, cols // tn),
        in_specs=[
            pl.BlockSpec((tm, d), lambda i, j: (i, 0)),
            pl.BlockSpec((1, d), lambda i, j: (0, 0)),
            pl.BlockSpec((d, tn), lambda i, j: (0, j)),
            pl.BlockSpec((1, tn), lambda i, j: (0, j)),
        ],
        out_specs=pl.BlockSpec((tm, tn), lambda i, j: (i, j)),
        out_shape=jax.ShapeDtypeStruct((n, cols), BF16),
        scratch_shapes=[pltpu.VMEM((tm, d), BF16)],
        compiler_params=_cparams("parallel", "arbitrary"),
        name="in_proj",
    )(h2, g, w, colscale)


def _compress_kernel(r_ref, pe_ref, w1_ref, w2_ref, o_ref):
    half = CMP_STRIDE * HEAD_DIM
    r = r_ref[0, 0].astype(F32)
    nrow = r.shape[0]
    pe = pe_ref[0]
    a = _dot((r + pe[:, :half]).astype(BF16), w1_ref[0, :half, :])
    b = _dot((r + pe[:, half:]).astype(BF16), w1_ref[0, half:, :])
    pre = a + pltpu.roll(b, nrow - 1, 0)
    y = _dot(jax.nn.gelu(pre).astype(BF16), w2_ref[0])
    row = lax.broadcasted_iota(jnp.int32, y.shape, 0)
    o_ref[0, 0] = jnp.where(row < nrow - 1, y, 0.0).astype(o_ref.dtype)


def compress(rows, pe, w1, w2):
    _, bg, nrow, width = rows.shape
    return pl.pallas_call(
        _compress_kernel,
        grid=(2, bg),
        in_specs=[
            pl.BlockSpec((1, 1, nrow, width), lambda s, i: (s, i, 0, 0)),
            pl.BlockSpec((1, 1, 2 * width), lambda s, i: (s, 0, 0)),
            pl.BlockSpec((1, 2 * width, HEAD_DIM), lambda s, i: (s, 0, 0)),
            pl.BlockSpec((1, HEAD_DIM, HEAD_DIM), lambda s, i: (s, 0, 0)),
        ],
        out_specs=pl.BlockSpec((1, 1, nrow, HEAD_DIM), lambda s, i: (s, i, 0, 0)),
        out_shape=jax.ShapeDtypeStruct((2, bg, nrow, HEAD_DIM), BF16),
        compiler_params=_cparams("arbitrary", "arbitrary"),
        name="compress",
    )(rows, pe, w1, w2)


def _group_slope(g, r):
    return jnp.where(g == 0, F32(2.0 ** -(r + 1)), F32(2.0 ** -(r + 1 + NSA_GROUP)))


def _cmp_attn_kernel(q_ref, kc_ref, vc_ref, sht_ref, o_ref, x_ref, *, tq, nb, topk):
    g = pl.program_id(0) % NSA_KV_HEADS
    t0 = pl.program_id(1) * tq
    kc = kc_ref[0, 0]
    vc = vc_ref[0, 0]
    nrow = kc.shape[0]
    tpos = t0 + lax.broadcasted_iota(jnp.int32, (tq, nrow), 0)
    col = lax.broadcasted_iota(jnp.int32, (tq, nrow), 1)
    dist = tpos - (col * CMP_STRIDE + (CMP_LEN - 1))
    valid = (dist >= 0) & (col < nrow - 1)
    distf = dist.astype(F32)
    psum = jnp.zeros((tq, nrow), F32)
    for r in range(NSA_GROUP):
        q = q_ref[:, r * HEAD_DIM:(r + 1) * HEAD_DIM]
        s = _nt_dot(q, kc) - _group_slope(g, r) * distf
        s = jnp.where(valid, s, NEG_INF)
        e = jnp.exp(s - jnp.max(s, axis=-1, keepdims=True))
        p = e / jnp.sum(e, axis=-1, keepdims=True)
        p = jnp.where(valid, p, 0.0)
        o_ref[:, r * HEAD_DIM:(r + 1) * HEAD_DIM] = _dot(p.astype(BF16), vc).astype(o_ref.dtype)
        psum = psum + p
    p_hi = psum.astype(BF16)
    p_lo = (psum - p_hi.astype(F32)).astype(BF16)
    sht = sht_ref[...]
    imp = _nt_dot(sht, p_hi) + _nt_dot(sht, p_lo)
    j = lax.broadcasted_iota(jnp.int32, (nb, tq), 0)
    cur = (t0 + lax.broadcasted_iota(jnp.int32, (nb, tq), 1)) // SLC_LEN
    forced = (j == 0) | (j == cur) | (j == cur - 1)
    score = jnp.where(forced, FORCE_SCORE, imp)
    score = jnp.where(j > cur, -jnp.inf, score)
    rank = jnp.zeros((nb, tq), F32)
    for k in range(nb):
        sk = score[k:k + 1, :]
        rank = rank + jnp.where(j > k, jnp.where(sk >= score, 1.0, 0.0), jnp.where(sk > score, 1.0, 0.0))
    neg = jnp.where(rank < topk, 0.0, NEG_INF)
    neg = jnp.concatenate([neg, jnp.zeros((LANES - nb, tq), F32)], axis=0)
    x_ref[...] = neg.T.astype(x_ref.dtype)


def cmp_attn(proj, kvc, sht, batch, seq, tq=256):
    n = proj.shape[0]
    nb = seq // SLC_LEN
    bg = batch * NSA_KV_HEADS
    nrow = kvc.shape[2]
    qt = seq // tq
    rowblk = lambda i, t: ((i // NSA_KV_HEADS) * qt + t, i % NSA_KV_HEADS)
    kern = functools.partial(_cmp_attn_kernel, tq=tq, nb=nb, topk=min(SLC_TOPK, nb))
    return pl.pallas_call(
        kern,
        grid=(bg, qt),
        in_specs=[
            pl.BlockSpec((tq, NSA_GROUP * HEAD_DIM), rowblk),
            pl.BlockSpec((1, 1, nrow, HEAD_DIM), lambda i, t: (0, i, 0, 0)),
            pl.BlockSpec((1, 1, nrow, HEAD_DIM), lambda i, t: (1, i, 0, 0)),
            pl.BlockSpec((nb, nrow), lambda i, t: (0, 0)),
        ],
        out_specs=[
            pl.BlockSpec((tq, NSA_GROUP * HEAD_DIM), rowblk),
            pl.BlockSpec((tq, LANES), rowblk),
        ],
        out_shape=[
            jax.ShapeDtypeStruct((n, NSA_WIDTH), BF16),
            jax.ShapeDtypeStruct((n, NSA_KV_HEADS * LANES), BF16),
        ],
        compiler_params=_cparams("parallel", "parallel"),
        name="cmp_attn",
    )(proj, kvc, kvc, sht)


def _alibi_lanes(slope, tpos_lane, lane):
    hi_t = (tpos_lane // SLC_LEN).astype(F32)
    lo_t = (tpos_lane % SLC_LEN).astype(F32)
    ax = jnp.where(lane == XL_HI, slope * SLC_LEN, 0.0)
    ax = jnp.where(lane == XL_LO, slope, ax)
    ax = jnp.where(lane == XL_ONE_A, -(slope * SLC_LEN) * hi_t, ax)
    ax = jnp.where(lane == XL_ONE_B, -slope * lo_t, ax)
    return ax


def _sel_attn_kernel(q_ref, x_ref, k_ref, v_ref, kx_ref, o_ref, m_ref, l_ref, acc_ref, *, tq, tk):
    g = pl.program_id(0) % NSA_KV_HEADS
    t0 = pl.program_id(1) * tq
    n_kt = (t0 + tq - 1) // tk + 1
    lane = lax.broadcasted_iota(jnp.int32, (tq, LANES), 1)
    tpos_lane = t0 + lax.broadcasted_iota(jnp.int32, (tq, LANES), 0)
    xsel = x_ref[...].astype(F32)
    rows = NSA_GROUP * tq
    tpos = t0 + (lax.broadcasted_iota(jnp.int32, (rows, tk), 0) & (tq - 1))
    kcol = lax.broadcasted_iota(jnp.int32, (rows, tk), 1)

    parts = []
    for r in range(NSA_GROUP):
        ax = _alibi_lanes(_group_slope(g, r), tpos_lane, lane)
        parts.append(jnp.concatenate(
            [q_ref[:, r * HEAD_DIM:(r + 1) * HEAD_DIM],
             jnp.where(lane < SLC_LEN, xsel, ax).astype(BF16)], axis=1))
    q2 = jnp.concatenate(parts, axis=0)
    m_ref[...] = jnp.full(m_ref.shape, -jnp.inf, F32)
    l_ref[...] = jnp.zeros(l_ref.shape, F32)
    acc_ref[...] = jnp.zeros(acc_ref.shape, F32)

    def step(kt, causal):
        ks = pl.multiple_of(kt * tk, tk)
        k2 = jnp.concatenate([k_ref[pl.ds(ks, tk), :], kx_ref[pl.ds(ks, tk), :]], axis=1)
        s = _nt_dot(q2, k2)
        if causal:
            s = jnp.where(ks + kcol <= tpos, s, NEG_INF)
        m_prev = m_ref[...]
        m_new = jnp.maximum(m_prev, jnp.max(s, axis=-1, keepdims=True))
        alpha = jnp.exp(m_prev - m_new)
        p = jnp.exp(s - m_new)
        l_ref[...] = alpha * l_ref[...] + jnp.sum(p, axis=-1, keepdims=True)
        acc_ref[...] = alpha * acc_ref[...] + _dot(p.astype(BF16), v_ref[pl.ds(ks, tk), :])
        m_ref[...] = m_new

    def body(kt, carry):
        step(kt, False)
        return carry

    lax.fori_loop(0, n_kt - 1, body, 0)
    step(n_kt - 1, True)
    o = acc_ref[...] / l_ref[...]
    for r in range(NSA_GROUP):
        o_ref[:, r * HEAD_DIM:(r + 1) * HEAD_DIM] = o[r * tq:(r + 1) * tq].astype(o_ref.dtype)


def sel_attn(proj, xsel, kx, batch, seq, tq=256, tk=512):
    n = proj.shape[0]
    tk = min(tk, seq)
    bg = batch * NSA_KV_HEADS
    qt = seq // tq
    rowblk = lambda i, t: ((i // NSA_KV_HEADS) * qt + t, i % NSA_KV_HEADS)
    kern = functools.partial(_sel_attn_kernel, tq=tq, tk=tk)
    return pl.pallas_call(
        kern,
        grid=(bg, qt),
        in_specs=[
            pl.BlockSpec((tq, NSA_GROUP * HEAD_DIM), rowblk),
            pl.BlockSpec((tq, LANES), rowblk),
            pl.BlockSpec((seq, HEAD_DIM), lambda i, t: (i // NSA_KV_HEADS, COL_KSLC + i % NSA_KV_HEADS)),
            pl.BlockSpec((seq, HEAD_DIM), lambda i, t: (i // NSA_KV_HEADS, COL_VSLC + i % NSA_KV_HEADS)),
            pl.BlockSpec((seq, LANES), lambda i, t: (0, 0)),
        ],
        out_specs=pl.BlockSpec((tq, NSA_GROUP * HEAD_DIM), rowblk),
        out_shape=jax.ShapeDtypeStruct((n, NSA_WIDTH), BF16),
        scratch_shapes=[
            pltpu.VMEM((NSA_GROUP * tq, 1), F32),
            pltpu.VMEM((NSA_GROUP * tq, 1), F32),
            pltpu.VMEM((NSA_GROUP * tq, HEAD_DIM), F32),
        ],
        compiler_params=_cparams("parallel", "parallel"),
        name="sel_attn",
    )(proj, xsel, proj, proj, kx)


def _win_attn_kernel(q_ref, k_ref, v_ref, kx_ref, gl_ref, oc_ref, os_ref, o_ref, *, tq):
    g = pl.program_id(0) % NSA_KV_HEADS
    t0 = pl.program_id(1) * tq
    p0 = pl.multiple_of(jnp.maximum(t0 - WIN, 0), LANES)
    d0 = pl.multiple_of(t0, LANES)
    k2 = jnp.concatenate([
        jnp.concatenate([k_ref[pl.ds(p0, WIN), :], kx_ref[pl.ds(p0, WIN), :]], axis=1),
        jnp.concatenate([k_ref[pl.ds(d0, tq), :], kx_ref[pl.ds(d0, tq), :]], axis=1)], axis=0)
    v2 = jnp.concatenate([v_ref[pl.ds(p0, WIN), :], v_ref[pl.ds(d0, tq), :]], axis=0)
    span = WIN + tq
    tpos = t0 + lax.broadcasted_iota(jnp.int32, (tq, span), 0)
    kpos = (t0 - WIN) + lax.broadcasted_iota(jnp.int32, (tq, span), 1)
    dist = tpos - kpos
    bias = jnp.where((kpos >= 0) & (dist >= 0) & (dist < WIN), 0.0, NEG_INF)
    lane = lax.broadcasted_iota(jnp.int32, (tq, LANES), 1)
    tpos_lane = t0 + lax.broadcasted_iota(jnp.int32, (tq, LANES), 0)
    gates = jax.nn.sigmoid(gl_ref[...].astype(F32))
    for r in range(NSA_GROUP):
        ax = _alibi_lanes(_group_slope(g, r), tpos_lane, lane)
        cols = slice(r * HEAD_DIM, (r + 1) * HEAD_DIM)
        q2 = jnp.concatenate([q_ref[:, cols], ax.astype(BF16)], axis=1)
        s = _nt_dot(q2, k2) + bias
        e = jnp.exp(s - jnp.max(s, axis=-1, keepdims=True))
        o_win = _dot(e.astype(BF16), v2) / jnp.sum(e, axis=-1, keepdims=True)
        o = (gates[:, 3 * r:3 * r + 1] * oc_ref[:, cols].astype(F32)
             + gates[:, 3 * r + 1:3 * r + 2] * os_ref[:, cols].astype(F32)
             + gates[:, 3 * r + 2:3 * r + 3] * o_win)
        o_ref[:, cols] = o.astype(o_ref.dtype)


def win_attn(proj, kx, o_cmp, o_slc, batch, seq, tq=512):
    n = proj.shape[0]
    bg = batch * NSA_KV_HEADS
    qt = seq // tq
    rowblk = lambda i, t: ((i // NSA_KV_HEADS) * qt + t, i % NSA_KV_HEADS)
    kern = functools.partial(_win_attn_kernel, tq=tq)
    return pl.pallas_call(
        kern,
        grid=(bg, qt),
        in_specs=[
            pl.BlockSpec((tq, NSA_GROUP * HEAD_DIM), rowblk),
            pl.BlockSpec((seq, HEAD_DIM), lambda i, t: (i // NSA_KV_HEADS, COL_KWIN + i % NSA_KV_HEADS)),
            pl.BlockSpec((seq, HEAD_DIM), lambda i, t: (i // NSA_KV_HEADS, COL_VWIN + i % NSA_KV_HEADS)),
            pl.BlockSpec((seq, LANES), lambda i, t: (0, 0)),
            pl.BlockSpec((tq, LANES), lambda i, t: ((i // NSA_KV_HEADS) * qt + t, COL_GATE + i % NSA_KV_HEADS)),
            pl.BlockSpec((tq, NSA_GROUP * HEAD_DIM), rowblk),
            pl.BlockSpec((tq, NSA_GROUP * HEAD_DIM), rowblk),
        ],
        out_specs=pl.BlockSpec((tq, NSA_GROUP * HEAD_DIM), rowblk),
        out_shape=jax.ShapeDtypeStruct((n, NSA_WIDTH), BF16),
        compiler_params=_cparams("parallel", "parallel"),
        name="win_attn",
    )(proj, proj, proj, kx, proj, o_cmp, o_slc)


def _gmlp_kernel(u_ref, v_ref, lg_ref, lb_ref, ws_ref, bsx_ref, o_ref, *, tm):
    u = jax.nn.gelu(u_ref[...].astype(F32))
    v = jax.nn.gelu(v_ref[...].astype(F32))
    mu = jnp.mean(v, axis=-1, keepdims=True)
    vc = v - mu
    var = jnp.mean(vc * vc, axis=-1, keepdims=True)
    vn = (vc * lax.rsqrt(var + EPS) * lg_ref[...] + lb_ref[...]).astype(BF16)
    c = GMLP_CHUNK
    tri = (lax.broadcasted_iota(jnp.int32, (c, c), 0) >= lax.broadcasted_iota(jnp.int32, (c, c), 1))
    bsx = bsx_ref[...]
    for gi in range(GMLP_GROUPS):
        w = jnp.where(tri, ws_ref[gi], 0.0).astype(BF16)
        cols = slice(gi * HEAD_DIM, (gi + 1) * HEAD_DIM)
        for ci in range(tm // c):
            rows = slice(ci * c, (ci + 1) * c)
            s = _dot(w, vn[rows, cols]) + bsx[:, cols]
            o_ref[rows, cols] = (u[rows, cols] * s).astype(o_ref.dtype)


def gmlp(proj, ln_g, ln_b, ws, bsx, tm=512):
    n = proj.shape[0]
    ublk = GMLP_WIDTH // LANES
    kern = functools.partial(_gmlp_kernel, tm=tm)
    return pl.pallas_call(
        kern,
        grid=(n // tm,),
        in_specs=[
            pl.BlockSpec((tm, GMLP_WIDTH), lambda i: (i, COL_U // ublk)),
            pl.BlockSpec((tm, GMLP_WIDTH), lambda i: (i, COL_V // ublk)),
            pl.BlockSpec((1, GMLP_WIDTH), lambda i: (0, 0)),
            pl.BlockSpec((1, GMLP_WIDTH), lambda i: (0, 0)),
            pl.BlockSpec((GMLP_GROUPS, GMLP_CHUNK, GMLP_CHUNK), lambda i: (0, 0, 0)),
            pl.BlockSpec((GMLP_CHUNK, GMLP_WIDTH), lambda i: (0, 0)),
        ],
        out_specs=pl.BlockSpec((tm, GMLP_WIDTH), lambda i: (i, 0)),
        out_shape=jax.ShapeDtypeStruct((n, GMLP_WIDTH), BF16),
        compiler_params=_cparams("parallel"),
        name="gmlp",
    )(proj, proj, ln_g, ln_b, ws, bsx)


def _retention_kernel(q_ref, k_ref, v_ref, g_ref, dec_ref, zeta_ref, xi_ref, dch_ref, gn_ref,
                      o_ref, s_ref, *, seq):
    c = RET_CHUNK
    s_ref[...] = jnp.zeros(s_ref.shape, F32)

    def body(ci, carry):
        rows = pl.ds(pl.multiple_of(ci * c, c), c)
        for hh in range(RET_HEADS):
            cols = slice(hh * HEAD_DIM, (hh + 1) * HEAD_DIM)
            q = q_ref[rows, cols]
            k = k_ref[rows, cols]
            v = v_ref[rows, cols]
            state = s_ref[hh]
            scores = _nt_dot(q, k) * dec_ref[hh]
            y = _dot(scores.astype(BF16), v) + _dot(q, state.astype(BF16)) * xi_ref[hh]
            kz = (k.astype(F32) * zeta_ref[hh]).astype(BF16)
            kv = lax.dot_general(kz, v, (((0,), (0,)), ((), ())), preferred_element_type=F32)
            s_ref[hh] = dch_ref[hh] * state + kv
            mu = jnp.mean(y, axis=-1, keepdims=True)
            yc = y - mu
            var = jnp.mean(yc * yc, axis=-1, keepdims=True)
            yn = yc * lax.rsqrt(var + EPS) * gn_ref[:, cols]
            o_ref[rows, cols] = (jax.nn.silu(g_ref[rows, cols].astype(F32)) * yn).astype(o_ref.dtype)
        return carry

    lax.fori_loop(0, seq // c, body, 0)


def retention(proj, tabs, gn_g, batch, seq):
    n = proj.shape[0]
    dec, zeta, xi, dch = tabs
    rblk = RET_WIDTH // LANES
    kern = functools.partial(_retention_kernel, seq=seq)
    full = lambda shape: pl.BlockSpec(shape, lambda b: (0,) * len(shape))
    return pl.pallas_call(
        kern,
        grid=(batch,),
        in_specs=[
            pl.BlockSpec((seq, RET_WIDTH), lambda b: (b, COL_RQ // rblk)),
            pl.BlockSpec((seq, RET_WIDTH), lambda b: (b, COL_RK // rblk)),
            pl.BlockSpec((seq, RET_WIDTH), lambda b: (b, COL_RV // rblk)),
            pl.BlockSpec((seq, RET_WIDTH), lambda b: (b, COL_RG // rblk)),
            full(dec.shape), full(zeta.shape), full(xi.shape), full(dch.shape),
            full((1, RET_WIDTH)),
        ],
        out_specs=pl.BlockSpec((seq, RET_WIDTH), lambda b: (b, 0)),
        out_shape=jax.ShapeDtypeStruct((n, RET_WIDTH), BF16),
        scratch_shapes=[pltpu.VMEM((RET_HEADS, HEAD_DIM, HEAD_DIM), F32)],
        compiler_params=_cparams("parallel"),
        name="retention",
    )(proj, proj, proj, proj, dec, zeta, xi, dch, gn_g)


ROUTE_E1, ROUTE_E2, ROUTE_G1, ROUTE_G2 = 0, 1, 2, 3


def _top2_route(logits):
    lane = lax.broadcasted_iota(jnp.int32, logits.shape, 1)
    lg = jnp.where(lane < N_EXPERTS, logits, -jnp.inf)
    v1 = jnp.max(lg, axis=-1, keepdims=True)
    i1 = jnp.min(jnp.where(lg == v1, lane, LANES), axis=-1, keepdims=True)
    lg2 = jnp.where(lane == i1, -jnp.inf, lg)
    v2 = jnp.max(lg2, axis=-1, keepdims=True)
    i2 = jnp.min(jnp.where(lg2 == v2, lane, LANES), axis=-1, keepdims=True)
    e2 = jnp.exp(v2 - v1)
    den = 1.0 + e2
    out = jnp.where(lane == ROUTE_E1, i1.astype(F32), 0.0)
    out = jnp.where(lane == ROUTE_E2, i2.astype(F32), out)
    out = jnp.where(lane == ROUTE_G1, 1.0 / den, out)
    return jnp.where(lane == ROUTE_G2, e2 / den, out)


def _pack_bf16_pair(lo, hi):
    lo_b = lax.bitcast_convert_type(lo.astype(BF16).astype(F32), jnp.uint32)
    hi_b = lax.bitcast_convert_type(hi.astype(BF16).astype(F32), jnp.uint32)
    return (lo_b >> 16) | hi_b


def _unpack_bf16_pair(word):
    lo = lax.bitcast_convert_type(word << 16, F32)
    hi = lax.bitcast_convert_type(word & jnp.uint32(0xFFFF0000), F32)
    return lo, hi


def _out_proj_kernel(*refs, router):
    if router:
        nsa_ref, gm_ref, ret_ref, h_ref, w_ref, g2_ref, wrh_ref, wrl_ref, br_ref, ho_ref, f_ref, rt_ref = refs
    else:
        nsa_ref, gm_ref, ret_ref, h_ref, w_ref, g2_ref, ho_ref, f_ref = refs
    acc = _dot(nsa_ref[...], w_ref[:NSA_WIDTH, :])
    acc = acc + _dot(gm_ref[...], w_ref[NSA_WIDTH:NSA_WIDTH + GMLP_WIDTH, :])
    acc = acc + _dot(ret_ref[...], w_ref[NSA_WIDTH + GMLP_WIDTH:, :])
    hn = h_ref[...] + acc
    ho_ref[...] = hn
    ms = jnp.mean(hn * hn, axis=-1, keepdims=True)
    f = hn * lax.rsqrt(ms + EPS) * g2_ref[...]
    f_hi = f.astype(BF16)
    if router:
        half = f.shape[1] // 2
        f_ref[...] = _pack_bf16_pair(f[:, :half], f[:, half:])
        f_lo = (f - f_hi.astype(F32)).astype(BF16)
        logits = (_dot(f_hi, wrh_ref[...]) + _dot(f_lo, wrh_ref[...]) + _dot(f_hi, wrl_ref[...])
                  + br_ref[...])
        rt_ref[...] = _top2_route(logits)
    else:
        f_ref[...] = f_hi


def out_proj(o_nsa, o_gm, o_ret, h2, w, g2, router_w=None, tm=512):
    n, d = h2.shape
    tm = min(tm, n)
    router = router_w is not None
    row = lambda width: pl.BlockSpec((tm, width), lambda i: (i, 0))
    full = lambda shape: pl.BlockSpec(shape, lambda i: (0,) * len(shape))
    in_specs = [row(NSA_WIDTH), row(GMLP_WIDTH), row(RET_WIDTH), row(d), full(w.shape), full((1, d))]
    args = [o_nsa, o_gm, o_ret, h2, w, g2]
    if router:
        in_specs += [full((d, LANES)), full((d, LANES)), full((1, LANES))]
        args += list(router_w)
        out_specs = [row(d), row(d // 2), row(LANES)]
        out_shape = [jax.ShapeDtypeStruct((n, d), F32), jax.ShapeDtypeStruct((n, d // 2), jnp.uint32),
                     jax.ShapeDtypeStruct((n, LANES), F32)]
    else:
        out_specs = [row(d), row(d)]
        out_shape = [jax.ShapeDtypeStruct((n, d), F32), jax.ShapeDtypeStruct((n, d), BF16)]
    return pl.pallas_call(
        functools.partial(_out_proj_kernel, router=router),
        grid=(n // tm,),
        in_specs=in_specs,
        out_specs=out_specs,
        out_shape=out_shape,
        compiler_params=_cparams("parallel"),
        name="out_proj_router" if router else "out_proj",
    )(*args)


def _ffn_kernel(*refs, final):
    if final:
        x_ref, h_ref, w1_ref, w3_ref, w2_ref, fg_ref, o_ref = refs
    else:
        x_ref, h_ref, w1_ref, w3_ref, w2_ref, o_ref = refs
    k = pl.program_id(1)

    @pl.when(k == 0)
    def _():
        o_ref[...] = h_ref[...]

    x = x_ref[...]
    hid = jax.nn.silu(_dot(x, w1_ref[...])) * _dot(x, w3_ref[...])
    o_ref[...] += _dot(hid.astype(BF16), w2_ref[...])

    if final:
        @pl.when(k == pl.num_programs(1) - 1)
        def _():
            y = o_ref[...]
            ms = jnp.mean(y * y, axis=-1, keepdims=True)
            o_ref[...] = y * lax.rsqrt(ms + EPS) * fg_ref[...]


def ffn(f, h2, w1, w3, w2, final_g=None, tm=512, tf=512):
    n, d = h2.shape
    ff = w1.shape[1]
    tm = min(tm, n)
    tf = min(tf, ff)
    in_specs = [
        pl.BlockSpec((tm, d), lambda i, k: (i, 0)),
        pl.BlockSpec((tm, d), lambda i, k: (i, 0)),
        pl.BlockSpec((d, tf), lambda i, k: (0, k)),
        pl.BlockSpec((d, tf), lambda i, k: (0, k)),
        pl.BlockSpec((tf, d), lambda i, k: (k, 0)),
    ]
    args = [f, h2, w1, w3, w2]
    if final_g is not None:
        in_specs.append(pl.BlockSpec((1, d), lambda i, k: (0, 0)))
        args.append(final_g)
    return pl.pallas_call(
        functools.partial(_ffn_kernel, final=final_g is not None),
        grid=(n // tm, ff // tf),
        in_specs=in_specs,
        out_specs=pl.BlockSpec((tm, d), lambda i, k: (i, 0)),
        out_shape=jax.ShapeDtypeStruct((n, d), F32),
        compiler_params=_cparams("parallel", "arbitrary"),
        name="ffn",
    )(*args)


MOE_TM = 512


def _row_copy(src_ref, src_row, dst_ref, dst_row, sem):
    return pltpu.make_async_copy(src_ref.at[pl.ds(src_row, 1)], dst_ref.at[pl.ds(dst_row, 1)], sem)


def _dispatch_kernel(dest_ref, f_ref, xg_in_ref, xg_ref, sem, *, tm):
    del xg_in_ref
    base = pl.program_id(0) * (tm * TOP_K)

    def issue(j, carry):
        for s in range(TOP_K):
            _row_copy(f_ref, j, xg_ref, dest_ref[base + j * TOP_K + s], sem).start()
        return carry

    lax.fori_loop(0, tm, issue, 0)

    def drain(j, carry):
        for s in range(TOP_K):
            _row_copy(f_ref, 0, xg_ref, 0, sem).wait()
        return carry

    lax.fori_loop(0, tm, drain, 0)


def moe_dispatch(dest, f_packed, rows, tm=256):
    n, half = f_packed.shape
    tm = min(tm, n)
    xg0 = jnp.zeros((rows, half), jnp.uint32)
    return pl.pallas_call(
        functools.partial(_dispatch_kernel, tm=tm),
        grid_spec=pltpu.PrefetchScalarGridSpec(
            num_scalar_prefetch=1,
            grid=(n // tm,),
            in_specs=[
                pl.BlockSpec((tm, half), lambda i, dest: (i, 0)),
                pl.BlockSpec(memory_space=pl.ANY),
            ],
            out_specs=pl.BlockSpec(memory_space=pl.ANY),
            scratch_shapes=[pltpu.SemaphoreType.DMA(())],
        ),
        out_shape=jax.ShapeDtypeStruct((rows, half), jnp.uint32),
        input_output_aliases={2: 0},
        compiler_params=_cparams("arbitrary"),
        name="moe_dispatch",
    )(dest, f_packed, xg0)


def _moe_ffn_kernel(te_ref, nu_ref, x_ref, w1_ref, w3_ref, w2_ref, o_ref, xb_ref, acc_ref):
    del te_ref
    i = pl.program_id(0)
    k = pl.program_id(1)
    nk = pl.num_programs(1)
    used = i < nu_ref[0]
    half = x_ref.shape[1]

    @pl.when(used & (k == 0))
    def _():
        lo, hi = _unpack_bf16_pair(x_ref[...])
        xb_ref[:, :half] = lo.astype(BF16)
        xb_ref[:, half:] = hi.astype(BF16)

    @pl.when(used)
    def _():
        x = xb_ref[...]
        hid = jax.nn.silu(_dot(x, w1_ref[0])) * _dot(x, w3_ref[0])
        y = _dot(hid.astype(BF16), w2_ref[0])

        @pl.when(k == 0)
        def _():
            acc_ref[...] = y

        @pl.when(k > 0)
        def _():
            acc_ref[...] += y

    @pl.when(used & (k == nk - 1))
    def _():
        y = acc_ref[...]
        o_ref[...] = _pack_bf16_pair(y[:, :half], y[:, half:])

    @pl.when(jnp.logical_not(used) & (k == nk - 1))
    def _():
        o_ref[...] = jnp.zeros(o_ref.shape, o_ref.dtype)


def moe_ffn(tile_expert, n_used, xg, w1, w3, w2, tm=MOE_TM, tf=256):
    rows, half = xg.shape
    _, d, ff = w1.shape
    tf = min(tf, ff)
    nk = ff // tf
    last = lambda i, nu: jnp.minimum(i, nu[0] - 1)
    kk = lambda i, k, nu: jnp.where(i < nu[0], k, nk - 1)
    return pl.pallas_call(
        _moe_ffn_kernel,
        grid_spec=pltpu.PrefetchScalarGridSpec(
            num_scalar_prefetch=2,
            grid=(rows // tm, nk),
            in_specs=[
                pl.BlockSpec((tm, half), lambda i, k, te, nu: (last(i, nu), 0)),
                pl.BlockSpec((1, d, tf), lambda i, k, te, nu: (te[i], 0, kk(i, k, nu))),
                pl.BlockSpec((1, d, tf), lambda i, k, te, nu: (te[i], 0, kk(i, k, nu))),
                pl.BlockSpec((1, tf, d), lambda i, k, te, nu: (te[i], kk(i, k, nu), 0)),
            ],
            out_specs=pl.BlockSpec((tm, half), lambda i, k, te, nu: (i, 0)),
            scratch_shapes=[pltpu.VMEM((tm, d), BF16), pltpu.VMEM((tm, d), F32)],
        ),
        out_shape=jax.ShapeDtypeStruct((rows, half), jnp.uint32),
        compiler_params=_cparams("arbitrary", "arbitrary"),
        name="moe_ffn",
    )(tile_expert, n_used, xg, w1, w3, w2)


def _combine_kernel(*refs, tm, final):
    if final:
        dest_ref, h_ref, rt_ref, yg_ref, fg_ref, o_ref, buf_ref, sem = refs
    else:
        dest_ref, h_ref, rt_ref, yg_ref, o_ref, buf_ref, sem = refs
    base = pl.program_id(0) * (tm * TOP_K)

    def issue(j, carry):
        for s in range(TOP_K):
            _row_copy(yg_ref, dest_ref[base + j * TOP_K + s], buf_ref.at[s], j, sem).start()
        return carry

    lax.fori_loop(0, tm, issue, 0)

    def drain(j, carry):
        for s in range(TOP_K):
            _row_copy(yg_ref, 0, buf_ref.at[s], 0, sem).wait()
        return carry

    lax.fori_loop(0, tm, drain, 0)

    rt = rt_ref[...]
    half = buf_ref.shape[2]
    h = h_ref[...]
    lo1, hi1 = _unpack_bf16_pair(buf_ref[0])
    lo2, hi2 = _unpack_bf16_pair(buf_ref[1])
    g1 = rt[:, ROUTE_G1:ROUTE_G1 + 1]
    g2 = rt[:, ROUTE_G2:ROUTE_G2 + 1]
    y_lo = h[:, :half] + g1 * lo1 + g2 * lo2
    y_hi = h[:, half:] + g1 * hi1 + g2 * hi2
    if final:
        ms = (jnp.sum(y_lo * y_lo, axis=-1, keepdims=True)
              + jnp.sum(y_hi * y_hi, axis=-1, keepdims=True)) / (2 * half)
        inv = lax.rsqrt(ms + EPS)
        y_lo = y_lo * inv * fg_ref[:, :half]
        y_hi = y_hi * inv * fg_ref[:, half:]
    o_ref[:, :half] = y_lo
    o_ref[:, half:] = y_hi


def moe_combine(dest, h2, route, yg, final_g=None, tm=256):
    n, d = h2.shape
    tm = min(tm, n)
    half = d // 2
    in_specs = [
        pl.BlockSpec((tm, d), lambda i, dest: (i, 0)),
        pl.BlockSpec((tm, LANES), lambda i, dest: (i, 0)),
        pl.BlockSpec(memory_space=pl.ANY),
    ]
    args = [dest, h2, route, yg]
    if final_g is not None:
        in_specs.append(pl.BlockSpec((1, d), lambda i, dest: (0, 0)))
        args.append(final_g)
    return pl.pallas_call(
        functools.partial(_combine_kernel, tm=tm, final=final_g is not None),
        grid_spec=pltpu.PrefetchScalarGridSpec(
            num_scalar_prefetch=1,
            grid=(n // tm,),
            in_specs=in_specs,
            out_specs=pl.BlockSpec((tm, d), lambda i, dest: (i, 0)),
            scratch_shapes=[pltpu.VMEM((TOP_K, tm, half), jnp.uint32), pltpu.SemaphoreType.DMA(())],
        ),
        out_shape=jax.ShapeDtypeStruct((n, d), F32),
        compiler_params=_cparams("arbitrary"),
        name="moe_combine",
    )(*args)


def _routing_tables(route, tm):
    n = route.shape[0]
    flat_e = route[:, ROUTE_E1:ROUTE_E2 + 1].astype(jnp.int32).reshape(-1)
    onehot = (flat_e[:, None] == jnp.arange(N_EXPERTS, dtype=jnp.int32)[None, :]).astype(jnp.int32)
    csum = jnp.cumsum(onehot, axis=0)
    rank = jnp.sum(csum * onehot, axis=1) - 1
    counts = csum[-1]
    padded = ((counts + tm - 1) // tm) * tm
    off_end = jnp.cumsum(padded)
    dest = (off_end - padded)[flat_e] + rank
    n_tiles = (n * TOP_K) // tm + N_EXPERTS
    n_used = (off_end[-1] // tm).astype(jnp.int32)
    tile_start = jnp.arange(n_tiles, dtype=jnp.int32) * tm
    tile_start = jnp.minimum(tile_start, off_end[-1] - tm)
    tile_expert = jnp.sum((tile_start[:, None] >= off_end[None, :]).astype(jnp.int32), axis=1)
    return dest.astype(jnp.int32), tile_expert.astype(jnp.int32), n_used.reshape(1), n_tiles * tm


def moe_layer(h2, f_packed, route, w1, w3, w2, final_g=None):
    dest, tile_expert, n_used, rows = _routing_tables(route, MOE_TM)
    xg = moe_dispatch(dest, f_packed, rows)
    yg = moe_ffn(tile_expert, n_used, xg, w1, w3, w2)
    return moe_combine(dest, h2, route, yg, final_g=final_g)


def _reorder_w_in(w_in):
    d = w_in.shape[0]
    kvw = NSA_KV_WIDTH
    o_gate = NSA_WIDTH + 6 * kvw
    o_uv = o_gate + 3 * NSA_HEADS
    o_ret = o_uv + 2 * GMLP_WIDTH
    gate = w_in[:, o_gate:o_uv].reshape(d, NSA_KV_HEADS, 3 * NSA_GROUP)
    gate = jnp.pad(gate, ((0, 0), (0, 0), (0, LANES - 3 * NSA_GROUP))).reshape(d, NSA_KV_HEADS * LANES)
    parts = [w_in[:, :o_gate], w_in[:, o_uv:], gate]
    w = jnp.concatenate(parts, axis=1)
    w = jnp.pad(w, ((0, 0), (0, PROJ_COLS - w.shape[1])))
    scale = np.ones((1, PROJ_COLS), np.float32)
    scale[:, COL_Q * LANES:(COL_Q + NSA_HEADS) * LANES] = HEAD_DIM ** -0.5
    scale[:, COL_RK * LANES:(COL_RK + RET_HEADS) * LANES] = HEAD_DIM ** -0.5
    del o_ret
    return w.astype(BF16), jnp.asarray(scale)


def _key_extra_lanes(seq):
    pos = np.arange(seq)
    kx = np.zeros((seq, LANES), np.float32)
    kx[pos, pos // SLC_LEN] = 1.0
    kx[:, XL_HI] = pos // SLC_LEN
    kx[:, XL_LO] = pos % SLC_LEN
    kx[:, XL_ONE_A] = 1.0
    kx[:, XL_ONE_B] = 1.0
    return jnp.asarray(kx, BF16)


def _share_t(seq):
    nrow = seq // CMP_STRIDE
    nb = seq // SLC_LEN
    c0 = np.arange(nrow)[None, :] * CMP_STRIDE
    s0 = np.arange(nb)[:, None] * SLC_LEN
    overlap = np.minimum(c0 + CMP_LEN, s0 + SLC_LEN) - np.maximum(c0, s0)
    share = np.clip(overlap, 0, CMP_LEN).astype(np.float32) / CMP_LEN
    share[:, nrow - 1] = 0.0
    return jnp.asarray(share, BF16)


def _retention_tables():
    hh = np.arange(RET_HEADS, dtype=np.float64)
    lg = np.log1p(-np.exp2(-5.0 - hh))
    nn = np.arange(RET_CHUNK, dtype=np.float64)
    rel = nn[:, None] - nn[None, :]
    dec = np.where(rel >= 0, np.exp(lg[:, None, None] * np.maximum(rel, 0.0)), 0.0)
    zeta = np.exp(lg[:, None] * (RET_CHUNK - 1.0 - nn))[:, :, None]
    xi = np.exp(lg[:, None] * (nn + 1.0))[:, :, None]
    dch = np.broadcast_to(np.exp(lg * RET_CHUNK)[:, None, None], (RET_HEADS, 1, HEAD_DIM))
    return tuple(jnp.asarray(a, F32) for a in (dec, zeta, xi, dch))


def _token_mixer(h2, batch, seq, ln1_g, w_in, pe_k, w1_k, w2_k, pe_v, w1_v, w2_v,
                 g_ln_g, g_ln_b, g_ws, g_bs, ret_gn_g, consts):
    kx, sht, ret_tabs = consts
    w_r, colscale = _reorder_w_in(w_in)
    proj = in_proj(h2, ln1_g[None, :], w_r, colscale)
    nrow = seq // CMP_STRIDE
    kv = proj[:, COL_KCMP * LANES:(COL_VCMP + NSA_KV_HEADS) * LANES]
    kv = kv.reshape(batch, nrow, CMP_STRIDE, 2, NSA_KV_HEADS, HEAD_DIM)
    rows = kv.transpose(3, 0, 4, 1, 2, 5).reshape(2, batch * NSA_KV_HEADS, nrow, CMP_STRIDE * HEAD_DIM)
    pe = jnp.stack([pe_k, pe_v]).reshape(2, 1, CMP_LEN * HEAD_DIM)
    kvc = compress(rows, pe, jnp.stack([w1_k, w1_v]).astype(BF16), jnp.stack([w2_k, w2_v]).astype(BF16))
    o_cmp, xsel = cmp_attn(proj, kvc, sht, batch, seq)
    o_slc = sel_attn(proj, xsel, kx, batch, seq)
    o_nsa = win_attn(proj, kx, o_cmp, o_slc, batch, seq)
    bsx = jnp.repeat(g_bs.T, HEAD_DIM, axis=1)
    o_gm = gmlp(proj, g_ln_g[None, :], g_ln_b[None, :], g_ws, bsx)
    o_ret = retention(proj, ret_tabs, ret_gn_g[None, :], batch, seq)
    return o_nsa, o_gm, o_ret


def kernel(x, ln1_g, w_in, cmp_pe_k, cmp_w1_k, cmp_w2_k, cmp_pe_v, cmp_w1_v, cmp_w2_v, gmlp_ln_g, gmlp_ln_b, gmlp_ws, gmlp_bs, ret_gn_g, w_out, ln2_g, ffn_w1, ffn_w3, ffn_w2, moe_wr, moe_br, moe_w1, moe_w3, moe_w2, final_g):
    batch, seq, d = x.shape
    depth = w_in.shape[0]
    consts = (_key_extra_lanes(seq), _share_t(seq), _retention_tables())
    h2 = x.reshape(batch * seq, d)
    for layer in range(depth):
        o_nsa, o_gm, o_ret = _token_mixer(
            h2, batch, seq, ln1_g[layer], w_in[layer], cmp_pe_k[layer], cmp_w1_k[layer], cmp_w2_k[layer],
            cmp_pe_v[layer], cmp_w1_v[layer], cmp_w2_v[layer], gmlp_ln_g[layer], gmlp_ln_b[layer],
            gmlp_ws[layer], gmlp_bs[layer], ret_gn_g[layer], consts)
        fin = final_g[None, :] if layer == depth - 1 else None
        i = layer // 2
        w_o = w_out[layer].astype(BF16)
        if layer % 2 == 0:
            h2, f = out_proj(o_nsa, o_gm, o_ret, h2, w_o, ln2_g[layer][None, :])
            h2 = ffn(f, h2, ffn_w1[i].astype(BF16), ffn_w3[i].astype(BF16), ffn_w2[i].astype(BF16),
                     final_g=fin)
        else:
            wr = jnp.pad(moe_wr[i], ((0, 0), (0, LANES - N_EXPERTS)))
            wr_hi = wr.astype(BF16)
            wr_lo = (wr - wr_hi.astype(F32)).astype(BF16)
            br = jnp.pad(moe_br[i], (0, LANES - N_EXPERTS))[None, :]
            h2, f_packed, route = out_proj(o_nsa, o_gm, o_ret, h2, w_o, ln2_g[layer][None, :],
                                           router_w=(wr_hi, wr_lo, br))
            h2 = moe_layer(h2, f_packed, route, moe_w1[i].astype(BF16), moe_w3[i].astype(BF16),
                           moe_w2[i].astype(BF16), final_g=fin)
    return h2.reshape(batch, seq, d)
```

```python
import functools

import numpy as np
import jax
import jax.numpy as jnp
from jax import lax
from jax.experimental import pallas as pl
from jax.experimental.pallas import tpu as pltpu

F32 = jnp.float32
BF16 = jnp.bfloat16

HEAD_DIM = 128
NSA_HEADS = 8
NSA_KV_HEADS = 2
NSA_GROUP = NSA_HEADS // NSA_KV_HEADS
NSA_WIDTH = NSA_HEADS * HEAD_DIM
NSA_KV_WIDTH = NSA_KV_HEADS * HEAD_DIM
CMP_LEN = 32
CMP_STRIDE = 16
SLC_LEN = 64
SLC_TOPK = 16
WIN = 512
GMLP_GROUPS = 4
GMLP_CHUNK = 128
GMLP_WIDTH = GMLP_GROUPS * HEAD_DIM
RET_HEADS = 4
RET_CHUNK = 128
RET_WIDTH = RET_HEADS * HEAD_DIM
N_EXPERTS = 8
TOP_K = 2
EPS = 1e-6
NEG_INF = -1e30
FORCE_SCORE = 1e4

LANES = 128
VMEM_LIMIT = 56 * 1024 * 1024

COL_Q = 0
COL_KCMP = 8
COL_VCMP = 10
COL_KSLC = 12
COL_VSLC = 14
COL_KWIN = 16
COL_VWIN = 18
COL_U = 20
COL_V = 24
COL_RQ = 28
COL_RK = 32
COL_RV = 36
COL_RG = 40
COL_GATE = 44
PROJ_COLS = 48 * LANES

XL_HI = 64
XL_LO = 65
XL_ONE_A = 66
XL_ONE_B = 67


def _cparams(*sem):
    return pltpu.CompilerParams(dimension_semantics=sem, vmem_limit_bytes=VMEM_LIMIT)


def _nt_dot(a, b):
    return lax.dot_general(a, b, (((1,), (1,)), ((), ())), preferred_element_type=F32)


def _dot(a, b):
    return jnp.dot(a, b, preferred_element_type=F32)


def _in_proj_kernel(x_ref, g_ref, w_ref, cs_ref, o_ref, xn_ref):
    @pl.when(pl.program_id(1) == 0)
    def _():
        x = x_ref[...]
        ms = jnp.mean(x * x, axis=-1, keepdims=True)
        xn_ref[...] = (x * lax.rsqrt(ms + EPS) * g_ref[...]).astype(BF16)

    acc = _dot(xn_ref[...], w_ref[...])
    o_ref[...] = (acc * cs_ref[...]).astype(o_ref.dtype)


def in_proj(h2, g, w, colscale, tm=1024, tn=1024):
    n, d = h2.shape
    cols = w.shape[1]
    tm = min(tm, n)
    return pl.pallas_call(
        _in_proj_kernel,
        grid=(n // tm, cols // tn),
        in_specs=[
            pl.BlockSpec((tm, d), lambda i, j: (i, 0)),
            pl.BlockSpec((1, d), lambda i, j: (0, 0)),
            pl.BlockSpec((d, tn), lambda i, j: (0, j)),
            pl.BlockSpec((1, tn), lambda i, j: (0, j)),
        ],
        out_specs=pl.BlockSpec((tm, tn), lambda i, j: (i, j)),
        out_shape=jax.ShapeDtypeStruct((n, cols), BF16),
        scratch_shapes=[pltpu.VMEM((tm, d), BF16)],
        compiler_params=_cparams("parallel", "arbitrary"),
        name="in_proj",
    )(h2, g, w, colscale)


def _compress_kernel(r_ref, pe_ref, w1_ref, w2_ref, o_ref):
    half = CMP_STRIDE * HEAD_DIM
    r = r_ref[0, 0].astype(F32)
    nrow = r.shape[0]
    pe = pe_ref[0]
    a = _dot((r + pe[:, :half]).astype(BF16), w1_ref[0, :half, :])
    b = _dot((r + pe[:, half:]).astype(BF16), w1_ref[0, half:, :])
    pre = a + pltpu.roll(b, nrow - 1, 0)
    y = _dot(jax.nn.gelu(pre).astype(BF16), w2_ref[0])
    row = lax.broadcasted_iota(jnp.int32, y.shape, 0)
    o_ref[0, 0] = jnp.where(row < nrow - 1, y, 0.0).astype(o_ref.dtype)


def compress(rows, pe, w1, w2):
    _, bg, nrow, width = rows.shape
    return pl.pallas_call(
        _compress_kernel,
        grid=(2, bg),
        in_specs=[
            pl.BlockSpec((1, 1, nrow, width), lambda s, i: (s, i, 0, 0)),
            pl.BlockSpec((1, 1, 2 * width), lambda s, i: (s, 0, 0)),
            pl.BlockSpec((1, 2 * width, HEAD_DIM), lambda s, i: (s, 0, 0)),
            pl.BlockSpec((1, HEAD_DIM, HEAD_DIM), lambda s, i: (s, 0, 0)),
        ],
        out_specs=pl.BlockSpec((1, 1, nrow, HEAD_DIM), lambda s, i: (s, i, 0, 0)),
        out_shape=jax.ShapeDtypeStruct((2, bg, nrow, HEAD_DIM), BF16),
        compiler_params=_cparams("arbitrary", "arbitrary"),
        name="compress",
    )(rows, pe, w1, w2)


def _group_slope(g, r):
    return jnp.where(g == 0, F32(2.0 ** -(r + 1)), F32(2.0 ** -(r + 1 + NSA_GROUP)))


def _cmp_attn_kernel(q_ref, kc_ref, vc_ref, sht_ref, o_ref, x_ref, *, tq, nb, topk):
    g = pl.program_id(0) % NSA_KV_HEADS
    t0 = pl.program_id(1) * tq
    kc = kc_ref[0, 0]
    vc = vc_ref[0, 0]
    nrow = kc.shape[0]
    tpos = t0 + lax.broadcasted_iota(jnp.int32, (tq, nrow), 0)
    col = lax.broadcasted_iota(jnp.int32, (tq, nrow), 1)
    dist = tpos - (col * CMP_STRIDE + (CMP_LEN - 1))
    valid = (dist >= 0) & (col < nrow - 1)
    distf = dist.astype(F32)
    psum = jnp.zeros((tq, nrow), F32)
    for r in range(NSA_GROUP):
        q = q_ref[:, r * HEAD_DIM:(r + 1) * HEAD_DIM]
        s = _nt_dot(q, kc) - _group_slope(g, r) * distf
        s = jnp.where(valid, s, NEG_INF)
        e = jnp.exp(s - jnp.max(s, axis=-1, keepdims=True))
        p = e / jnp.sum(e, axis=-1, keepdims=True)
        p = jnp.where(valid, p, 0.0)
        o_ref[:, r * HEAD_DIM:(r + 1) * HEAD_DIM] = _dot(p.astype(BF16), vc).astype(o_ref.dtype)
        psum = psum + p
    p_hi = psum.astype(BF16)
    p_lo = (psum - p_hi.astype(F32)).astype(BF16)
    sht = sht_ref[...]
    imp = _nt_dot(sht, p_hi) + _nt_dot(sht, p_lo)
    j = lax.broadcasted_iota(jnp.int32, (nb, tq), 0)
    cur = (t0 + lax.broadcasted_iota(jnp.int32, (nb, tq), 1)) // SLC_LEN
    forced = (j == 0) | (j == cur) | (j == cur - 1)
    score = jnp.where(forced, FORCE_SCORE, imp)
    score = jnp.where(j > cur, -jnp.inf, score)
    rank = jnp.zeros((nb, tq), F32)
    for k in range(nb):
        sk = score[k:k + 1, :]
        rank = rank + jnp.where(j > k, jnp.where(sk >= score, 1.0, 0.0), jnp.where(sk > score, 1.0, 0.0))
    neg = jnp.where(rank < topk, 0.0, NEG_INF)
    neg = jnp.concatenate([neg, jnp.zeros((LANES - nb, tq), F32)], axis=0)
    x_ref[...] = neg.T.astype(x_ref.dtype)


def cmp_attn(proj, kvc, sht, batch, seq, tq=256):
    n = proj.shape[0]
    nb = seq // SLC_LEN
    bg = batch * NSA_KV_HEADS
    nrow = kvc.shape[2]
    qt = seq // tq
    rowblk = lambda i, t: ((i // NSA_KV_HEADS) * qt + t, i % NSA_KV_HEADS)
    kern = functools.partial(_cmp_attn_kernel, tq=tq, nb=nb, topk=min(SLC_TOPK, nb))
    return pl.pallas_call(
        kern,
        grid=(bg, qt),
        in_specs=[
            pl.BlockSpec((tq, NSA_GROUP * HEAD_DIM), rowblk),
            pl.BlockSpec((1, 1, nrow, HEAD_DIM), lambda i, t: (0, i, 0, 0)),
            pl.BlockSpec((1, 1, nrow, HEAD_DIM), lambda i, t: (1, i, 0, 0)),
            pl.BlockSpec((nb, nrow), lambda i, t: (0, 0)),
        ],
        out_specs=[
            pl.BlockSpec((tq, NSA_GROUP * HEAD_DIM), rowblk),
            pl.BlockSpec((tq, LANES), rowblk),
        ],
        out_shape=[
            jax.ShapeDtypeStruct((n, NSA_WIDTH), BF16),
            jax.ShapeDtypeStruct((n, NSA_KV_HEADS * LANES), BF16),
        ],
        compiler_params=_cparams("parallel", "parallel"),
        name="cmp_attn",
    )(proj, kvc, kvc, sht)


def _alibi_lanes(slope, tpos_lane, lane):
    hi_t = (tpos_lane // SLC_LEN).astype(F32)
    lo_t = (tpos_lane % SLC_LEN).astype(F32)
    ax = jnp.where(lane == XL_HI, slope * SLC_LEN, 0.0)
    ax = jnp.where(lane == XL_LO, slope, ax)
    ax = jnp.where(lane == XL_ONE_A, -(slope * SLC_LEN) * hi_t, ax)
    ax = jnp.where(lane == XL_ONE_B, -slope * lo_t, ax)
    return ax


def _sel_attn_kernel(q_ref, x_ref, k_ref, v_ref, kx_ref, o_ref, q2_ref, sa_ref, sb_ref, m_ref, acc_ref, *, tq, tk):
    g = pl.program_id(0) % NSA_KV_HEADS
    t0 = pl.program_id(1) * tq
    n_kt = (t0 + tq - 1) // tk + 1
    lane = lax.broadcasted_iota(jnp.int32, (tq, LANES), 1)
    tpos_lane = t0 + lax.broadcasted_iota(jnp.int32, (tq, LANES), 0)
    xsel = x_ref[...].astype(F32)
    rows = NSA_GROUP * tq
    tpos = t0 + (lax.broadcasted_iota(jnp.int32, (rows, tk), 0) & (tq - 1))
    kcol = lax.broadcasted_iota(jnp.int32, (rows, tk), 1)

    parts = []
    for r in range(NSA_GROUP):
        ax = _alibi_lanes(_group_slope(g, r), tpos_lane, lane)
        parts.append(jnp.concatenate(
            [q_ref[:, r * HEAD_DIM:(r + 1) * HEAD_DIM],
             jnp.where(lane < SLC_LEN, xsel, ax).astype(BF16)], axis=1))
    q2_ref[...] = jnp.concatenate(parts, axis=0)
    m_ref[...] = jnp.full(m_ref.shape, -jnp.inf, F32)
    acc_ref[...] = jnp.zeros(acc_ref.shape, F32)
    ones = jnp.ones((tk, LANES), BF16)

    def scores(kt, dst_ref):
        ks = pl.multiple_of(kt * tk, tk)
        k2 = jnp.concatenate([k_ref[pl.ds(ks, tk), :], kx_ref[pl.ds(ks, tk), :]], axis=1)
        dst_ref[...] = _nt_dot(q2_ref[...], k2)

    def update(kt, src_ref, causal):
        ks = pl.multiple_of(kt * tk, tk)
        v2 = jnp.concatenate([v_ref[pl.ds(ks, tk), :], ones], axis=1)
        s = src_ref[...]
        if causal:
            s = jnp.where(ks + kcol <= tpos, s, NEG_INF)
        m_prev = m_ref[...]
        m_new = jnp.maximum(m_prev, jnp.max(s, axis=-1, keepdims=True))
        alpha = jnp.exp(m_prev - m_new)
        p = jnp.exp(s - jnp.concatenate([m_new] * (tk // LANES), axis=1))
        acc_ref[...] = jnp.concatenate([alpha, alpha], axis=1) * acc_ref[...] + _dot(p.astype(BF16), v2)
        m_ref[...] = m_new

    n_full = n_kt - 1
    scores(0, sa_ref)

    def body(j, carry):
        scores(2 * j + 1, sb_ref)
        update(2 * j, sa_ref, False)
        scores(2 * j + 2, sa_ref)
        update(2 * j + 1, sb_ref, False)
        return carry

    lax.fori_loop(0, n_full // 2, body, 0)

    @pl.when(n_full % 2 == 1)
    def _():
        scores(n_full, sb_ref)
        update(n_full - 1, sa_ref, False)
        update(n_full, sb_ref, True)

    @pl.when(n_full % 2 == 0)
    def _():
        update(n_full, sa_ref, True)

    acc = acc_ref[...]
    o = acc[:, :HEAD_DIM] / acc[:, HEAD_DIM:]
    for r in range(NSA_GROUP):
        o_ref[:, r * HEAD_DIM:(r + 1) * HEAD_DIM] = o[r * tq:(r + 1) * tq].astype(o_ref.dtype)


def sel_attn(proj, xsel, kx, batch, seq, tq=256, tk=512):
    n = proj.shape[0]
    tk = min(tk, seq)
    bg = batch * NSA_KV_HEADS
    qt = seq // tq
    rowblk = lambda i, t: ((i // NSA_KV_HEADS) * qt + t, i % NSA_KV_HEADS)
    kern = functools.partial(_sel_attn_kernel, tq=tq, tk=tk)
    return pl.pallas_call(
        kern,
        grid=(bg, qt),
        in_specs=[
            pl.BlockSpec((tq, NSA_GROUP * HEAD_DIM), rowblk),
            pl.BlockSpec((tq, LANES), rowblk),
            pl.BlockSpec((seq, HEAD_DIM), lambda i, t: (i // NSA_KV_HEADS, COL_KSLC + i % NSA_KV_HEADS)),
            pl.BlockSpec((seq, HEAD_DIM), lambda i, t: (i // NSA_KV_HEADS, COL_VSLC + i % NSA_KV_HEADS)),
            pl.BlockSpec((seq, LANES), lambda i, t: (0, 0)),
        ],
        out_specs=pl.BlockSpec((tq, NSA_GROUP * HEAD_DIM), rowblk),
        out_shape=jax.ShapeDtypeStruct((n, NSA_WIDTH), BF16),
        scratch_shapes=[
            pltpu.VMEM((NSA_GROUP * tq, 2 * HEAD_DIM), BF16),
            pltpu.VMEM((NSA_GROUP * tq, tk), F32),
            pltpu.VMEM((NSA_GROUP * tq, tk), F32),
            pltpu.VMEM((NSA_GROUP * tq, LANES), F32),
            pltpu.VMEM((NSA_GROUP * tq, 2 * HEAD_DIM), F32),
        ],
        compiler_params=_cparams("parallel", "parallel"),
        name="sel_attn",
    )(proj, xsel, proj, proj, kx)


def _win_attn_kernel(q_ref, k_ref, v_ref, kx_ref, gl_ref, oc_ref, os_ref, o_ref, *, tq):
    g = pl.program_id(0) % NSA_KV_HEADS
    t0 = pl.program_id(1) * tq
    p0 = pl.multiple_of(jnp.maximum(t0 - WIN, 0), LANES)
    d0 = pl.multiple_of(t0, LANES)
    k2 = jnp.concatenate([
        jnp.concatenate([k_ref[pl.ds(p0, WIN), :], kx_ref[pl.ds(p0, WIN), :]], axis=1),
        jnp.concatenate([k_ref[pl.ds(d0, tq), :], kx_ref[pl.ds(d0, tq), :]], axis=1)], axis=0)
    span = WIN + tq
    v2 = jnp.concatenate([
        jnp.concatenate([v_ref[pl.ds(p0, WIN), :], v_ref[pl.ds(d0, tq), :]], axis=0),
        jnp.ones((span, LANES), BF16)], axis=1)
    tpos = t0 + lax.broadcasted_iota(jnp.int32, (tq, span), 0)
    kpos = (t0 - WIN) + lax.broadcasted_iota(jnp.int32, (tq, span), 1)
    dist = tpos - kpos
    bias = jnp.where((kpos >= 0) & (dist >= 0) & (dist < WIN), 0.0, NEG_INF)
    lane = lax.broadcasted_iota(jnp.int32, (tq, LANES), 1)
    tpos_lane = t0 + lax.broadcasted_iota(jnp.int32, (tq, LANES), 0)
    gates = jax.nn.sigmoid(gl_ref[...].astype(F32))
    for r in range(NSA_GROUP):
        ax = _alibi_lanes(_group_slope(g, r), tpos_lane, lane)
        cols = slice(r * HEAD_DIM, (r + 1) * HEAD_DIM)
        q2 = jnp.concatenate([q_ref[:, cols], ax.astype(BF16)], axis=1)
        s = _nt_dot(q2, k2) + bias
        e = jnp.exp(s - jnp.max(s, axis=-1, keepdims=True))
        ov = _dot(e.astype(BF16), v2)
        o_win = ov[:, :HEAD_DIM] / ov[:, HEAD_DIM:]
        o = (gates[:, 3 * r:3 * r + 1] * oc_ref[:, cols].astype(F32)
             + gates[:, 3 * r + 1:3 * r + 2] * os_ref[:, cols].astype(F32)
             + gates[:, 3 * r + 2:3 * r + 3] * o_win)
        o_ref[:, cols] = o.astype(o_ref.dtype)


def win_attn(proj, kx, o_cmp, o_slc, batch, seq, tq=512):
    n = proj.shape[0]
    bg = batch * NSA_KV_HEADS
    qt = seq // tq
    rowblk = lambda i, t: ((i // NSA_KV_HEADS) * qt + t, i % NSA_KV_HEADS)
    kern = functools.partial(_win_attn_kernel, tq=tq)
    return pl.pallas_call(
        kern,
        grid=(bg, qt),
        in_specs=[
            pl.BlockSpec((tq, NSA_GROUP * HEAD_DIM), rowblk),
            pl.BlockSpec((seq, HEAD_DIM), lambda i, t: (i // NSA_KV_HEADS, COL_KWIN + i % NSA_KV_HEADS)),
            pl.BlockSpec((seq, HEAD_DIM), lambda i, t: (i // NSA_KV_HEADS, COL_VWIN + i % NSA_KV_HEADS)),
            pl.BlockSpec((seq, LANES), lambda i, t: (0, 0)),
            pl.BlockSpec((tq, LANES), lambda i, t: ((i // NSA_KV_HEADS) * qt + t, COL_GATE + i % NSA_KV_HEADS)),
            pl.BlockSpec((tq, NSA_GROUP * HEAD_DIM), rowblk),
            pl.BlockSpec((tq, NSA_GROUP * HEAD_DIM), rowblk),
        ],
        out_specs=pl.BlockSpec((tq, NSA_GROUP * HEAD_DIM), rowblk),
        out_shape=jax.ShapeDtypeStruct((n, NSA_WIDTH), BF16),
        compiler_params=_cparams("parallel", "parallel"),
        name="win_attn",
    )(proj, proj, proj, kx, proj, o_cmp, o_slc)


def _gmlp_kernel(u_ref, v_ref, lg_ref, lb_ref, ws_ref, bsx_ref, o_ref, *, tm):
    u = jax.nn.gelu(u_ref[...].astype(F32))
    v = jax.nn.gelu(v_ref[...].astype(F32))
    mu = jnp.mean(v, axis=-1, keepdims=True)
    vc = v - mu
    var = jnp.mean(vc * vc, axis=-1, keepdims=True)
    vn = (vc * lax.rsqrt(var + EPS) * lg_ref[...] + lb_ref[...]).astype(BF16)
    c = GMLP_CHUNK
    tri = (lax.broadcasted_iota(jnp.int32, (c, c), 0) >= lax.broadcasted_iota(jnp.int32, (c, c), 1))
    bsx = bsx_ref[...]
    for gi in range(GMLP_GROUPS):
        w = jnp.where(tri, ws_ref[gi], 0.0).astype(BF16)
        cols = slice(gi * HEAD_DIM, (gi + 1) * HEAD_DIM)
        for ci in range(tm // c):
            rows = slice(ci * c, (ci + 1) * c)
            s = _dot(w, vn[rows, cols]) + bsx[:, cols]
            o_ref[rows, cols] = (u[rows, cols] * s).astype(o_ref.dtype)


def gmlp(proj, ln_g, ln_b, ws, bsx, tm=512):
    n = proj.shape[0]
    ublk = GMLP_WIDTH // LANES
    kern = functools.partial(_gmlp_kernel, tm=tm)
    return pl.pallas_call(
        kern,
        grid=(n // tm,),
        in_specs=[
            pl.BlockSpec((tm, GMLP_WIDTH), lambda i: (i, COL_U // ublk)),
            pl.BlockSpec((tm, GMLP_WIDTH), lambda i: (i, COL_V // ublk)),
            pl.BlockSpec((1, GMLP_WIDTH), lambda i: (0, 0)),
            pl.BlockSpec((1, GMLP_WIDTH), lambda i: (0, 0)),
            pl.BlockSpec((GMLP_GROUPS, GMLP_CHUNK, GMLP_CHUNK), lambda i: (0, 0, 0)),
            pl.BlockSpec((GMLP_CHUNK, GMLP_WIDTH), lambda i: (0, 0)),
        ],
        out_specs=pl.BlockSpec((tm, GMLP_WIDTH), lambda i: (i, 0)),
        out_shape=jax.ShapeDtypeStruct((n, GMLP_WIDTH), BF16),
        compiler_params=_cparams("parallel"),
        name="gmlp",
    )(proj, proj, ln_g, ln_b, ws, bsx)


def _retention_kernel(q_ref, k_ref, v_ref, g_ref, dec_ref, zeta_ref, xi_ref, dch_ref, gn_ref,
                      o_ref, s_ref, *, seq):
    c = RET_CHUNK
    s_ref[...] = jnp.zeros(s_ref.shape, F32)

    def body(ci, carry):
        rows = pl.ds(pl.multiple_of(ci * c, c), c)
        for hh in range(RET_HEADS):
            cols = slice(hh * HEAD_DIM, (hh + 1) * HEAD_DIM)
            q = q_ref[rows, cols]
            k = k_ref[rows, cols]
            v = v_ref[rows, cols]
            state = s_ref[hh]
            scores = _nt_dot(q, k) * dec_ref[hh]
            y = _dot(scores.astype(BF16), v) + _dot(q, state.astype(BF16)) * xi_ref[hh]
            kz = (k.astype(F32) * zeta_ref[hh]).astype(BF16)
            kv = lax.dot_general(kz, v, (((0,), (0,)), ((), ())), preferred_element_type=F32)
            s_ref[hh] = dch_ref[hh] * state + kv
            mu = jnp.mean(y, axis=-1, keepdims=True)
            yc = y - mu
            var = jnp.mean(yc * yc, axis=-1, keepdims=True)
            yn = yc * lax.rsqrt(var + EPS) * gn_ref[:, cols]
            o_ref[rows, cols] = (jax.nn.silu(g_ref[rows, cols].astype(F32)) * yn).astype(o_ref.dtype)
        return carry

    lax.fori_loop(0, seq // c, body, 0)


def retention(proj, tabs, gn_g, batch, seq):
    n = proj.shape[0]
    dec, zeta, xi, dch = tabs
    rblk = RET_WIDTH // LANES
    kern = functools.partial(_retention_kernel, seq=seq)
    full = lambda shape: pl.BlockSpec(shape, lambda b: (0,) * len(shape))
    return pl.pallas_call(
        kern,
        grid=(batch,),
        in_specs=[
            pl.BlockSpec((seq, RET_WIDTH), lambda b: (b, COL_RQ // rblk)),
            pl.BlockSpec((seq, RET_WIDTH), lambda b: (b, COL_RK // rblk)),
            pl.BlockSpec((seq, RET_WIDTH), lambda b: (b, COL_RV // rblk)),
            pl.BlockSpec((seq, RET_WIDTH), lambda b: (b, COL_RG // rblk)),
            full(dec.shape), full(zeta.shape), full(xi.shape), full(dch.shape),
            full((1, RET_WIDTH)),
        ],
        out_specs=pl.BlockSpec((seq, RET_WIDTH), lambda b: (b, 0)),
        out_shape=jax.ShapeDtypeStruct((n, RET_WIDTH), BF16),
        scratch_shapes=[pltpu.VMEM((RET_HEADS, HEAD_DIM, HEAD_DIM), F32)],
        compiler_params=_cparams("parallel"),
        name="retention",
    )(proj, proj, proj, proj, dec, zeta, xi, dch, gn_g)


ROUTE_E1, ROUTE_E2, ROUTE_G1, ROUTE_G2 = 0, 1, 2, 3


def _top2_route(logits):
    lane = lax.broadcasted_iota(jnp.int32, logits.shape, 1)
    lg = jnp.where(lane < N_EXPERTS, logits, -jnp.inf)
    v1 = jnp.max(lg, axis=-1, keepdims=True)
    i1 = jnp.min(jnp.where(lg == v1, lane, LANES), axis=-1, keepdims=True)
    lg2 = jnp.where(lane == i1, -jnp.inf, lg)
    v2 = jnp.max(lg2, axis=-1, keepdims=True)
    i2 = jnp.min(jnp.where(lg2 == v2, lane, LANES), axis=-1, keepdims=True)
    e2 = jnp.exp(v2 - v1)
    den = 1.0 + e2
    out = jnp.where(lane == ROUTE_E1, i1.astype(F32), 0.0)
    out = jnp.where(lane == ROUTE_E2, i2.astype(F32), out)
    out = jnp.where(lane == ROUTE_G1, 1.0 / den, out)
    return jnp.where(lane == ROUTE_G2, e2 / den, out)


def _pack_bf16_pair(lo, hi):
    lo_b = lax.bitcast_convert_type(lo.astype(BF16).astype(F32), jnp.uint32)
    hi_b = lax.bitcast_convert_type(hi.astype(BF16).astype(F32), jnp.uint32)
    return (lo_b >> 16) | hi_b


def _unpack_bf16_pair(word):
    lo = lax.bitcast_convert_type(word << 16, F32)
    hi = lax.bitcast_convert_type(word & jnp.uint32(0xFFFF0000), F32)
    return lo, hi


def _out_proj_kernel(*refs, router):
    if router:
        nsa_ref, gm_ref, ret_ref, h_ref, w_ref, g2_ref, wrh_ref, wrl_ref, br_ref, ho_ref, f_ref, rt_ref = refs
    else:
        nsa_ref, gm_ref, ret_ref, h_ref, w_ref, g2_ref, ho_ref, f_ref = refs
    acc = _dot(nsa_ref[...], w_ref[:NSA_WIDTH, :])
    acc = acc + _dot(gm_ref[...], w_ref[NSA_WIDTH:NSA_WIDTH + GMLP_WIDTH, :])
    acc = acc + _dot(ret_ref[...], w_ref[NSA_WIDTH + GMLP_WIDTH:, :])
    hn = h_ref[...] + acc
    ho_ref[...] = hn
    ms = jnp.mean(hn * hn, axis=-1, keepdims=True)
    f = hn * lax.rsqrt(ms + EPS) * g2_ref[...]
    f_hi = f.astype(BF16)
    if router:
        half = f.shape[1] // 2
        f_ref[...] = _pack_bf16_pair(f[:, :half], f[:, half:])
        f_lo = (f - f_hi.astype(F32)).astype(BF16)
        logits = (_dot(f_hi, wrh_ref[...]) + _dot(f_lo, wrh_ref[...]) + _dot(f_hi, wrl_ref[...])
                  + br_ref[...])
        rt_ref[...] = _top2_route(logits)
    else:
        f_ref[...] = f_hi


def out_proj(o_nsa, o_gm, o_ret, h2, w, g2, router_w=None, tm=512):
    n, d = h2.shape
    tm = min(tm, n)
    router = router_w is not None
    row = lambda width: pl.BlockSpec((tm, width), lambda i: (i, 0))
    full = lambda shape: pl.BlockSpec(shape, lambda i: (0,) * len(shape))
    in_specs = [row(NSA_WIDTH), row(GMLP_WIDTH), row(RET_WIDTH), row(d), full(w.shape), full((1, d))]
    args = [o_nsa, o_gm, o_ret, h2, w, g2]
    if router:
        in_specs += [full((d, LANES)), full((d, LANES)), full((1, LANES))]
        args += list(router_w)
        out_specs = [row(d), row(d // 2), row(LANES)]
        out_shape = [jax.ShapeDtypeStruct((n, d), F32), jax.ShapeDtypeStruct((n, d // 2), jnp.uint32),
                     jax.ShapeDtypeStruct((n, LANES), F32)]
    else:
        out_specs = [row(d), row(d)]
        out_shape = [jax.ShapeDtypeStruct((n, d), F32), jax.ShapeDtypeStruct((n, d), BF16)]
    return pl.pallas_call(
        functools.partial(_out_proj_kernel, router=router),
        grid=(n // tm,),
        in_specs=in_specs,
        out_specs=out_specs,
        out_shape=out_shape,
        compiler_params=_cparams("parallel"),
        name="out_proj_router" if router else "out_proj",
    )(*args)


def _ffn_kernel(*refs, final):
    if final:
        x_ref, h_ref, w1_ref, w3_ref, w2_ref, fg_ref, o_ref = refs
    else:
        x_ref, h_ref, w1_ref, w3_ref, w2_ref, o_ref = refs
    k = pl.program_id(1)

    @pl.when(k == 0)
    def _():
        o_ref[...] = h_ref[...]

    x = x_ref[...]
    hid = jax.nn.silu(_dot(x, w1_ref[...])) * _dot(x, w3_ref[...])
    o_ref[...] += _dot(hid.astype(BF16), w2_ref[...])

    if final:
        @pl.when(k == pl.num_programs(1) - 1)
        def _():
            y = o_ref[...]
            ms = jnp.mean(y * y, axis=-1, keepdims=True)
            o_ref[...] = y * lax.rsqrt(ms + EPS) * fg_ref[...]


def ffn(f, h2, w1, w3, w2, final_g=None, tm=512, tf=512):
    n, d = h2.shape
    ff = w1.shape[1]
    tm = min(tm, n)
    tf = min(tf, ff)
    in_specs = [
        pl.BlockSpec((tm, d), lambda i, k: (i, 0)),
        pl.BlockSpec((tm, d), lambda i, k: (i, 0)),
        pl.BlockSpec((d, tf), lambda i, k: (0, k)),
        pl.BlockSpec((d, tf), lambda i, k: (0, k)),
        pl.BlockSpec((tf, d), lambda i, k: (k, 0)),
    ]
    args = [f, h2, w1, w3, w2]
    if final_g is not None:
        in_specs.append(pl.BlockSpec((1, d), lambda i, k: (0, 0)))
        args.append(final_g)
    return pl.pallas_call(
        functools.partial(_ffn_kernel, final=final_g is not None),
        grid=(n // tm, ff // tf),
        in_specs=in_specs,
        out_specs=pl.BlockSpec((tm, d), lambda i, k: (i, 0)),
        out_shape=jax.ShapeDtypeStruct((n, d), F32),
        compiler_params=_cparams("parallel", "arbitrary"),
        name="ffn",
    )(*args)


MOE_TM = 512


def _row_copy(src_ref, src_row, dst_ref, dst_row, sem):
    return pltpu.make_async_copy(src_ref.at[pl.ds(src_row, 1)], dst_ref.at[pl.ds(dst_row, 1)], sem)


def _dispatch_kernel(dest_ref, f_ref, xg_in_ref, xg_ref, sem, *, tm):
    del xg_in_ref
    base = pl.program_id(0) * (tm * TOP_K)

    def issue(j, carry):
        for s in range(TOP_K):
            _row_copy(f_ref, j, xg_ref, dest_ref[base + j * TOP_K + s], sem).start()
        return carry

    lax.fori_loop(0, tm, issue, 0, unroll=8)

    def drain(j, carry):
        for s in range(TOP_K):
            _row_copy(f_ref, 0, xg_ref, 0, sem).wait()
        return carry

    lax.fori_loop(0, tm, drain, 0, unroll=8)


def moe_dispatch(dest, f_packed, rows, tm=256):
    n, half = f_packed.shape
    tm = min(tm, n)
    xg0 = jnp.zeros((rows, half), jnp.uint32)
    return pl.pallas_call(
        functools.partial(_dispatch_kernel, tm=tm),
        grid_spec=pltpu.PrefetchScalarGridSpec(
            num_scalar_prefetch=1,
            grid=(n // tm,),
            in_specs=[
                pl.BlockSpec((tm, half), lambda i, dest: (i, 0)),
                pl.BlockSpec(memory_space=pl.ANY),
            ],
            out_specs=pl.BlockSpec(memory_space=pl.ANY),
            scratch_shapes=[pltpu.SemaphoreType.DMA(())],
        ),
        out_shape=jax.ShapeDtypeStruct((rows, half), jnp.uint32),
        input_output_aliases={2: 0},
        compiler_params=_cparams("arbitrary"),
        name="moe_dispatch",
    )(dest, f_packed, xg0)


def _moe_ffn_kernel(te_ref, nu_ref, x_ref, w1_ref, w3_ref, w2_ref, o_ref, xb_ref, acc_ref):
    del te_ref
    i = pl.program_id(0)
    k = pl.program_id(1)
    nk = pl.num_programs(1)
    used = i < nu_ref[0]
    half = x_ref.shape[1]

    @pl.when(used & (k == 0))
    def _():
        lo, hi = _unpack_bf16_pair(x_ref[...])
        xb_ref[:, :half] = lo.astype(BF16)
        xb_ref[:, half:] = hi.astype(BF16)
        acc_ref[...] = jnp.zeros(acc_ref.shape, F32)

    @pl.when(used)
    def _():
        x = xb_ref[...]
        hid = jax.nn.silu(_dot(x, w1_ref[0])) * _dot(x, w3_ref[0])
        acc_ref[...] += _dot(hid.astype(BF16), w2_ref[0])

    @pl.when(used & (k == nk - 1))
    def _():
        y = acc_ref[...]
        o_ref[...] = _pack_bf16_pair(y[:, :half], y[:, half:])

    @pl.when(jnp.logical_not(used) & (k == nk - 1))
    def _():
        o_ref[...] = jnp.zeros(o_ref.shape, o_ref.dtype)


def moe_ffn(tile_expert, n_used, xg, w1, w3, w2, tm=MOE_TM, tf=256):
    rows, half = xg.shape
    _, d, ff = w1.shape
    tf = min(tf, ff)
    nk = ff // tf
    last = lambda i, nu: jnp.minimum(i, nu[0] - 1)
    kk = lambda i, k, nu: jnp.where(i < nu[0], k, nk - 1)
    return pl.pallas_call(
        _moe_ffn_kernel,
        grid_spec=pltpu.PrefetchScalarGridSpec(
            num_scalar_prefetch=2,
            grid=(rows // tm, nk),
            in_specs=[
                pl.BlockSpec((tm, half), lambda i, k, te, nu: (last(i, nu), 0)),
                pl.BlockSpec((1, d, tf), lambda i, k, te, nu: (te[i], 0, kk(i, k, nu))),
                pl.BlockSpec((1, d, tf), lambda i, k, te, nu: (te[i], 0, kk(i, k, nu))),
                pl.BlockSpec((1, tf, d), lambda i, k, te, nu: (te[i], kk(i, k, nu), 0)),
            ],
            out_specs=pl.BlockSpec((tm, half), lambda i, k, te, nu: (i, 0)),
            scratch_shapes=[pltpu.VMEM((tm, d), BF16), pltpu.VMEM((tm, d), F32)],
        ),
        out_shape=jax.ShapeDtypeStruct((rows, half), jnp.uint32),
        compiler_params=_cparams("arbitrary", "arbitrary"),
        name="moe_ffn",
    )(tile_expert, n_used, xg, w1, w3, w2)


def _combine_kernel(*refs, tm, final):
    if final:
        dest_ref, h_ref, rt_ref, yg_ref, fg_ref, o_ref, buf_ref, sem = refs
    else:
        dest_ref, h_ref, rt_ref, yg_ref, o_ref, buf_ref, sem = refs
    base = pl.program_id(0) * (tm * TOP_K)

    def issue(j, carry):
        for s in range(TOP_K):
            _row_copy(yg_ref, dest_ref[base + j * TOP_K + s], buf_ref.at[s], j, sem).start()
        return carry

    lax.fori_loop(0, tm, issue, 0, unroll=8)

    def drain(j, carry):
        for s in range(TOP_K):
            _row_copy(yg_ref, 0, buf_ref.at[s], 0, sem).wait()
        return carry

    lax.fori_loop(0, tm, drain, 0, unroll=8)

    rt = rt_ref[...]
    half = buf_ref.shape[2]
    h = h_ref[...]
    lo1, hi1 = _unpack_bf16_pair(buf_ref[0])
    lo2, hi2 = _unpack_bf16_pair(buf_ref[1])
    g1 = rt[:, ROUTE_G1:ROUTE_G1 + 1]
    g2 = rt[:, ROUTE_G2:ROUTE_G2 + 1]
    y_lo = h[:, :half] + g1 * lo1 + g2 * lo2
    y_hi = h[:, half:] + g1 * hi1 + g2 * hi2
    if final:
        ms = (jnp.sum(y_lo * y_lo, axis=-1, keepdims=True)
              + jnp.sum(y_hi * y_hi, axis=-1, keepdims=True)) / (2 * half)
        inv = lax.rsqrt(ms + EPS)
        y_lo = y_lo * inv * fg_ref[:, :half]
        y_hi = y_hi * inv * fg_ref[:, half:]
    o_ref[:, :half] = y_lo
    o_ref[:, half:] = y_hi


def moe_combine(dest, h2, route, yg, final_g=None, tm=256):
    n, d = h2.shape
    tm = min(tm, n)
    half = d // 2
    in_specs = [
        pl.BlockSpec((tm, d), lambda i, dest: (i, 0)),
        pl.BlockSpec((tm, LANES), lambda i, dest: (i, 0)),
        pl.BlockSpec(memory_space=pl.ANY),
    ]
    args = [dest, h2, route, yg]
    if final_g is not None:
        in_specs.append(pl.BlockSpec((1, d), lambda i, dest: (0, 0)))
        args.append(final_g)
    return pl.pallas_call(
        functools.partial(_combine_kernel, tm=tm, final=final_g is not None),
        grid_spec=pltpu.PrefetchScalarGridSpec(
            num_scalar_prefetch=1,
            grid=(n // tm,),
            in_specs=in_specs,
            out_specs=pl.BlockSpec((tm, d), lambda i, dest: (i, 0)),
            scratch_shapes=[pltpu.VMEM((TOP_K, tm, half), jnp.uint32), pltpu.SemaphoreType.DMA(())],
        ),
        out_shape=jax.ShapeDtypeStruct((n, d), F32),
        compiler_params=_cparams("arbitrary"),
        name="moe_combine",
    )(*args)


def _routing_tables(route, tm):
    n = route.shape[0]
    flat_e = route[:, ROUTE_E1:ROUTE_E2 + 1].astype(jnp.int32).reshape(-1)
    onehot = (flat_e[:, None] == jnp.arange(N_EXPERTS, dtype=jnp.int32)[None, :]).astype(jnp.int32)
    csum = jnp.cumsum(onehot, axis=0)
    rank = jnp.sum(csum * onehot, axis=1) - 1
    counts = csum[-1]
    padded = ((counts + tm - 1) // tm) * tm
    off_end = jnp.cumsum(padded)
    dest = (off_end - padded)[flat_e] + rank
    n_tiles = (n * TOP_K) // tm + N_EXPERTS
    n_used = (off_end[-1] // tm).astype(jnp.int32)
    tile_start = jnp.arange(n_tiles, dtype=jnp.int32) * tm
    tile_start = jnp.minimum(tile_start, off_end[-1] - tm)
    tile_expert = jnp.sum((tile_start[:, None] >= off_end[None, :]).astype(jnp.int32), axis=1)
    return dest.astype(jnp.int32), tile_expert.astype(jnp.int32), n_used.reshape(1), n_tiles * tm


def moe_layer(h2, f_packed, route, w1, w3, w2, final_g=None):
    dest, tile_expert, n_used, rows = _routing_tables(route, MOE_TM)
    xg = moe_dispatch(dest, f_packed, rows)
    yg = moe_ffn(tile_expert, n_used, xg, w1, w3, w2)
    return moe_combine(dest, h2, route, yg, final_g=final_g)


def _reorder_w_in(w_in):
    d = w_in.shape[0]
    kvw = NSA_KV_WIDTH
    o_gate = NSA_WIDTH + 6 * kvw
    o_uv = o_gate + 3 * NSA_HEADS
    o_ret = o_uv + 2 * GMLP_WIDTH
    gate = w_in[:, o_gate:o_uv].reshape(d, NSA_KV_HEADS, 3 * NSA_GROUP)
    gate = jnp.pad(gate, ((0, 0), (0, 0), (0, LANES - 3 * NSA_GROUP))).reshape(d, NSA_KV_HEADS * LANES)
    parts = [w_in[:, :o_gate], w_in[:, o_uv:], gate]
    w = jnp.concatenate(parts, axis=1)
    w = jnp.pad(w, ((0, 0), (0, PROJ_COLS - w.shape[1])))
    scale = np.ones((1, PROJ_COLS), np.float32)
    scale[:, COL_Q * LANES:(COL_Q + NSA_HEADS) * LANES] = HEAD_DIM ** -0.5
    scale[:, COL_RK * LANES:(COL_RK + RET_HEADS) * LANES] = HEAD_DIM ** -0.5
    del o_ret
    return w.astype(BF16), jnp.asarray(scale)


def _key_extra_lanes(seq):
    pos = np.arange(seq)
    kx = np.zeros((seq, LANES), np.float32)
    kx[pos, pos // SLC_LEN] = 1.0
    kx[:, XL_HI] = pos // SLC_LEN
    kx[:, XL_LO] = pos % SLC_LEN
    kx[:, XL_ONE_A] = 1.0
    kx[:, XL_ONE_B] = 1.0
    return jnp.asarray(kx, BF16)


def _share_t(seq):
    nrow = seq // CMP_STRIDE
    nb = seq // SLC_LEN
    c0 = np.arange(nrow)[None, :] * CMP_STRIDE
    s0 = np.arange(nb)[:, None] * SLC_LEN
    overlap = np.minimum(c0 + CMP_LEN, s0 + SLC_LEN) - np.maximum(c0, s0)
    share = np.clip(overlap, 0, CMP_LEN).astype(np.float32) / CMP_LEN
    share[:, nrow - 1] = 0.0
    return jnp.asarray(share, BF16)


def _retention_tables():
    hh = np.arange(RET_HEADS, dtype=np.float64)
    lg = np.log1p(-np.exp2(-5.0 - hh))
    nn = np.arange(RET_CHUNK, dtype=np.float64)
    rel = nn[:, None] - nn[None, :]
    dec = np.where(rel >= 0, np.exp(lg[:, None, None] * np.maximum(rel, 0.0)), 0.0)
    zeta = np.exp(lg[:, None] * (RET_CHUNK - 1.0 - nn))[:, :, None]
    xi = np.exp(lg[:, None] * (nn + 1.0))[:, :, None]
    dch = np.broadcast_to(np.exp(lg * RET_CHUNK)[:, None, None], (RET_HEADS, 1, HEAD_DIM))
    return tuple(jnp.asarray(a, F32) for a in (dec, zeta, xi, dch))


def _token_mixer(h2, batch, seq, ln1_g, w_in, pe_k, w1_k, w2_k, pe_v, w1_v, w2_v,
                 g_ln_g, g_ln_b, g_ws, g_bs, ret_gn_g, consts):
    kx, sht, ret_tabs = consts
    w_r, colscale = _reorder_w_in(w_in)
    proj = in_proj(h2, ln1_g[None, :], w_r, colscale)
    nrow = seq // CMP_STRIDE
    kv = proj[:, COL_KCMP * LANES:(COL_VCMP + NSA_KV_HEADS) * LANES]
    kv = kv.reshape(batch, nrow, CMP_STRIDE, 2, NSA_KV_HEADS, HEAD_DIM)
    rows = kv.transpose(3, 0, 4, 1, 2, 5).reshape(2, batch * NSA_KV_HEADS, nrow, CMP_STRIDE * HEAD_DIM)
    pe = jnp.stack([pe_k, pe_v]).reshape(2, 1, CMP_LEN * HEAD_DIM)
    kvc = compress(rows, pe, jnp.stack([w1_k, w1_v]).astype(BF16), jnp.stack([w2_k, w2_v]).astype(BF16))
    o_cmp, xsel = cmp_attn(proj, kvc, sht, batch, seq)
    o_slc = sel_attn(proj, xsel, kx, batch, seq)
    o_nsa = win_attn(proj, kx, o_cmp, o_slc, batch, seq)
    bsx = jnp.repeat(g_bs.T, HEAD_DIM, axis=1)
    o_gm = gmlp(proj, g_ln_g[None, :], g_ln_b[None, :], g_ws, bsx)
    o_ret = retention(proj, ret_tabs, ret_gn_g[None, :], batch, seq)
    return o_nsa, o_gm, o_ret


def kernel(x, ln1_g, w_in, cmp_pe_k, cmp_w1_k, cmp_w2_k, cmp_pe_v, cmp_w1_v, cmp_w2_v, gmlp_ln_g, gmlp_ln_b, gmlp_ws, gmlp_bs, ret_gn_g, w_out, ln2_g, ffn_w1, ffn_w3, ffn_w2, moe_wr, moe_br, moe_w1, moe_w3, moe_w2, final_g):
    batch, seq, d = x.shape
    depth = w_in.shape[0]
    consts = (_key_extra_lanes(seq), _share_t(seq), _retention_tables())
    h2 = x.reshape(batch * seq, d)
    for layer in range(depth):
        o_nsa, o_gm, o_ret = _token_mixer(
            h2, batch, seq, ln1_g[layer], w_in[layer], cmp_pe_k[layer], cmp_w1_k[layer], cmp_w2_k[layer],
            cmp_pe_v[layer], cmp_w1_v[layer], cmp_w2_v[layer], gmlp_ln_g[layer], gmlp_ln_b[layer],
            gmlp_ws[layer], gmlp_bs[layer], ret_gn_g[layer], consts)
        fin = final_g[None, :] if layer == depth - 1 else None
        i = layer // 2
        w_o = w_out[layer].astype(BF16)
        if layer % 2 == 0:
            h2, f = out_proj(o_nsa, o_gm, o_ret, h2, w_o, ln2_g[layer][None, :])
            h2 = ffn(f, h2, ffn_w1[i].astype(BF16), ffn_w3[i].astype(BF16), ffn_w2[i].astype(BF16),
                     final_g=fin)
        else:
            wr = jnp.pad(moe_wr[i], ((0, 0), (0, LANES - N_EXPERTS)))
            wr_hi = wr.astype(BF16)
            wr_lo = (wr - wr_hi.astype(F32)).astype(BF16)
            br = jnp.pad(moe_br[i], (0, LANES - N_EXPERTS))[None, :]
            h2, f_packed, route = out_proj(o_nsa, o_gm, o_ret, h2, w_o, ln2_g[layer][None, :],
                                           router_w=(wr_hi, wr_lo, br))
            h2 = moe_layer(h2, f_packed, route, moe_w1[i].astype(BF16), moe_w3[i].astype(BF16),
                           moe_w2[i].astype(BF16), final_g=fin)
    return h2.reshape(batch, seq, d)
```

```python
import functools

import numpy as np
import jax
import jax.numpy as jnp
from jax import lax
from jax.experimental import pallas as pl
from jax.experimental.pallas import tpu as pltpu

F32 = jnp.float32
BF16 = jnp.bfloat16

HEAD_DIM = 128
NSA_HEADS = 8
NSA_KV_HEADS = 2
NSA_GROUP = NSA_HEADS // NSA_KV_HEADS
NSA_WIDTH = NSA_HEADS * HEAD_DIM
NSA_KV_WIDTH = NSA_KV_HEADS * HEAD_DIM
CMP_LEN = 32
CMP_STRIDE = 16
SLC_LEN = 64
SLC_TOPK = 16
WIN = 512
GMLP_GROUPS = 4
GMLP_CHUNK = 128
GMLP_WIDTH = GMLP_GROUPS * HEAD_DIM
RET_HEADS = 4
RET_CHUNK = 128
RET_WIDTH = RET_HEADS * HEAD_DIM
N_EXPERTS = 8
TOP_K = 2
EPS = 1e-6
NEG_INF = -1e30
FORCE_SCORE = 1e4

LANES = 128
VMEM_LIMIT = 56 * 1024 * 1024

COL_Q = 0
COL_KCMP = 8
COL_VCMP = 10
COL_KSLC = 12
COL_VSLC = 14
COL_KWIN = 16
COL_VWIN = 18
COL_U = 20
COL_V = 24
COL_RQ = 28
COL_RK = 32
COL_RV = 36
COL_RG = 40
COL_GATE = 44
PROJ_COLS = 48 * LANES

XL_HI = 64
XL_LO = 65
XL_ONE_A = 66
XL_ONE_B = 67


def _cparams(*sem):
    return pltpu.CompilerParams(dimension_semantics=sem, vmem_limit_bytes=VMEM_LIMIT)


def _nt_dot(a, b):
    return lax.dot_general(a, b, (((1,), (1,)), ((), ())), preferred_element_type=F32)


def _dot(a, b):
    return jnp.dot(a, b, preferred_element_type=F32)


def _in_proj_kernel(x_ref, g_ref, w_ref, cs_ref, o_ref, xn_ref):
    @pl.when(pl.program_id(1) == 0)
    def _():
        x = x_ref[...]
        ms = jnp.mean(x * x, axis=-1, keepdims=True)
        xn_ref[...] = (x * lax.rsqrt(ms + EPS) * g_ref[...]).astype(BF16)

    acc = _dot(xn_ref[...], w_ref[...])
    o_ref[...] = (acc * cs_ref[...]).astype(o_ref.dtype)


def in_proj(h2, g, w, colscale, tm=1024, tn=1024):
    n, d = h2.shape
    cols = w.shape[1]
    tm = min(tm, n)
    return pl.pallas_call(
        _in_proj_kernel,
        grid=(n // tm, cols // tn),
        in_specs=[
            pl.BlockSpec((tm, d), lambda i, j: (i, 0)),
            pl.BlockSpec((1, d), lambda i, j: (0, 0)),
            pl.BlockSpec((d, tn), lambda i, j: (0, j)),
            pl.BlockSpec((1, tn), lambda i, j: (0, j)),
        ],
        out_specs=pl.BlockSpec((tm, tn), lambda i, j: (i, j)),
        out_shape=jax.ShapeDtypeStruct((n, cols), BF16),
        scratch_shapes=[pltpu.VMEM((tm, d), BF16)],
        compiler_params=_cparams("parallel", "arbitrary"),
        name="in_proj",
    )(h2, g, w, colscale)


def _compress_kernel(r_ref, pe_ref, w1_ref, w2_ref, o_ref):
    half = CMP_STRIDE * HEAD_DIM
    r = r_ref[0, 0].astype(F32)
    nrow = r.shape[0]
    pe = pe_ref[0]
    a = _dot((r + pe[:, :half]).astype(BF16), w1_ref[0, :half, :])
    b = _dot((r + pe[:, half:]).astype(BF16), w1_ref[0, half:, :])
    pre = a + pltpu.roll(b, nrow - 1, 0)
    y = _dot(jax.nn.gelu(pre).astype(BF16), w2_ref[0])
    row = lax.broadcasted_iota(jnp.int32, y.shape, 0)
    o_ref[0, 0] = jnp.where(row < nrow - 1, y, 0.0).astype(o_ref.dtype)


def compress(rows, pe, w1, w2):
    _, bg, nrow, width = rows.shape
    return pl.pallas_call(
        _compress_kernel,
        grid=(2, bg),
        in_specs=[
            pl.BlockSpec((1, 1, nrow, width), lambda s, i: (s, i, 0, 0)),
            pl.BlockSpec((1, 1, 2 * width), lambda s, i: (s, 0, 0)),
            pl.BlockSpec((1, 2 * width, HEAD_DIM), lambda s, i: (s, 0, 0)),
            pl.BlockSpec((1, HEAD_DIM, HEAD_DIM), lambda s, i: (s, 0, 0)),
        ],
        out_specs=pl.BlockSpec((1, 1, nrow, HEAD_DIM), lambda s, i: (s, i, 0, 0)),
        out_shape=jax.ShapeDtypeStruct((2, bg, nrow, HEAD_DIM), BF16),
        compiler_params=_cparams("arbitrary", "arbitrary"),
        name="compress",
    )(rows, pe, w1, w2)


def _group_slope(g, r):
    return jnp.where(g == 0, F32(2.0 ** -(r + 1)), F32(2.0 ** -(r + 1 + NSA_GROUP)))


def _cmp_attn_kernel(q_ref, kc_ref, vc_ref, sht_ref, o_ref, x_ref, *, tq, nb, topk):
    g = pl.program_id(0) % NSA_KV_HEADS
    t0 = pl.program_id(1) * tq
    kc = kc_ref[0, 0]
    vc = vc_ref[0, 0]
    nrow = kc.shape[0]
    tpos = t0 + lax.broadcasted_iota(jnp.int32, (tq, nrow), 0)
    col = lax.broadcasted_iota(jnp.int32, (tq, nrow), 1)
    dist = tpos - (col * CMP_STRIDE + (CMP_LEN - 1))
    valid = (dist >= 0) & (col < nrow - 1)
    distf = dist.astype(F32)
    psum = jnp.zeros((tq, nrow), F32)
    for r in range(NSA_GROUP):
        q = q_ref[:, r * HEAD_DIM:(r + 1) * HEAD_DIM]
        s = _nt_dot(q, kc) - _group_slope(g, r) * distf
        s = jnp.where(valid, s, NEG_INF)
        e = jnp.exp(s - jnp.max(s, axis=-1, keepdims=True))
        p = e / jnp.sum(e, axis=-1, keepdims=True)
        p = jnp.where(valid, p, 0.0)
        o_ref[:, r * HEAD_DIM:(r + 1) * HEAD_DIM] = _dot(p.astype(BF16), vc).astype(o_ref.dtype)
        psum = psum + p
    p_hi = psum.astype(BF16)
    p_lo = (psum - p_hi.astype(F32)).astype(BF16)
    sht = sht_ref[...]
    imp = _nt_dot(sht, p_hi) + _nt_dot(sht, p_lo)
    j = lax.broadcasted_iota(jnp.int32, (nb, tq), 0)
    cur = (t0 + lax.broadcasted_iota(jnp.int32, (nb, tq), 1)) // SLC_LEN
    forced = (j == 0) | (j == cur) | (j == cur - 1)
    score = jnp.where(forced, FORCE_SCORE, imp)
    cand = jnp.where(j > cur, -3e38, score)
    jf = j.astype(F32)
    for _ in range(topk):
        best = jnp.max(cand, axis=0, keepdims=True)
        idx = jnp.min(jnp.where(cand == best, jf, float(nb)), axis=0, keepdims=True)
        cand = jnp.where(jf == idx, -jnp.inf, cand)
    neg = jnp.where(cand == -jnp.inf, 0.0, NEG_INF)
    neg = jnp.concatenate([neg, jnp.zeros((LANES - nb, tq), F32)], axis=0)
    x_ref[...] = neg.T.astype(x_ref.dtype)


def cmp_attn(proj, kvc, sht, batch, seq, tq=256):
    n = proj.shape[0]
    nb = seq // SLC_LEN
    bg = batch * NSA_KV_HEADS
    nrow = kvc.shape[2]
    qt = seq // tq
    rowblk = lambda i, t: ((i // NSA_KV_HEADS) * qt + t, i % NSA_KV_HEADS)
    kern = functools.partial(_cmp_attn_kernel, tq=tq, nb=nb, topk=min(SLC_TOPK, nb))
    return pl.pallas_call(
        kern,
        grid=(bg, qt),
        in_specs=[
            pl.BlockSpec((tq, NSA_GROUP * HEAD_DIM), rowblk),
            pl.BlockSpec((1, 1, nrow, HEAD_DIM), lambda i, t: (0, i, 0, 0)),
            pl.BlockSpec((1, 1, nrow, HEAD_DIM), lambda i, t: (1, i, 0, 0)),
            pl.BlockSpec((nb, nrow), lambda i, t: (0, 0)),
        ],
        out_specs=[
            pl.BlockSpec((tq, NSA_GROUP * HEAD_DIM), rowblk),
            pl.BlockSpec((tq, LANES), rowblk),
        ],
        out_shape=[
            jax.ShapeDtypeStruct((n, NSA_WIDTH), BF16),
            jax.ShapeDtypeStruct((n, NSA_KV_HEADS * LANES), BF16),
        ],
        compiler_params=_cparams("parallel", "parallel"),
        name="cmp_attn",
    )(proj, kvc, kvc, sht)


def _alibi_lanes(slope, tpos_lane, lane):
    hi_t = (tpos_lane // SLC_LEN).astype(F32)
    lo_t = (tpos_lane % SLC_LEN).astype(F32)
    ax = jnp.where(lane == XL_HI, slope * SLC_LEN, 0.0)
    ax = jnp.where(lane == XL_LO, slope, ax)
    ax = jnp.where(lane == XL_ONE_A, -(slope * SLC_LEN) * hi_t, ax)
    ax = jnp.where(lane == XL_ONE_B, -slope * lo_t, ax)
    return ax


def _sel_attn_kernel(q_ref, x_ref, k_ref, v_ref, kx_ref, o_ref, q2_ref, sa_ref, sb_ref, m_ref, acc_ref, *, tq, tk):
    g = pl.program_id(0) % NSA_KV_HEADS
    t0 = pl.program_id(1) * tq
    n_kt = (t0 + tq - 1) // tk + 1
    lane = lax.broadcasted_iota(jnp.int32, (tq, LANES), 1)
    tpos_lane = t0 + lax.broadcasted_iota(jnp.int32, (tq, LANES), 0)
    xsel = x_ref[...].astype(F32)
    rows = NSA_GROUP * tq
    tpos = t0 + (lax.broadcasted_iota(jnp.int32, (rows, tk), 0) & (tq - 1))
    kcol = lax.broadcasted_iota(jnp.int32, (rows, tk), 1)

    parts = []
    for r in range(NSA_GROUP):
        ax = _alibi_lanes(_group_slope(g, r), tpos_lane, lane)
        parts.append(jnp.concatenate(
            [q_ref[:, r * HEAD_DIM:(r + 1) * HEAD_DIM],
             jnp.where(lane < SLC_LEN, xsel, ax).astype(BF16)], axis=1))
    q2_ref[...] = jnp.concatenate(parts, axis=0)
    m_ref[...] = jnp.full(m_ref.shape, -jnp.inf, F32)
    acc_ref[...] = jnp.zeros(acc_ref.shape, F32)
    ones = jnp.ones((tk, LANES), BF16)

    def scores(kt, dst_ref):
        ks = pl.multiple_of(kt * tk, tk)
        k2 = jnp.concatenate([k_ref[pl.ds(ks, tk), :], kx_ref[pl.ds(ks, tk), :]], axis=1)
        dst_ref[...] = _nt_dot(q2_ref[...], k2)

    def update(kt, src_ref, causal):
        ks = pl.multiple_of(kt * tk, tk)
        v2 = jnp.concatenate([v_ref[pl.ds(ks, tk), :], ones], axis=1)
        s = src_ref[...]
        if causal:
            s = jnp.where(ks + kcol <= tpos, s, NEG_INF)
        m_prev = m_ref[...]
        m_new = jnp.maximum(m_prev, jnp.max(s, axis=-1, keepdims=True))
        alpha = jnp.exp(m_prev - m_new)
        p = jnp.exp(s - jnp.concatenate([m_new] * (tk // LANES), axis=1))
        acc_ref[...] = jnp.concatenate([alpha, alpha], axis=1) * acc_ref[...] + _dot(p.astype(BF16), v2)
        m_ref[...] = m_new

    n_full = n_kt - 1
    scores(0, sa_ref)

    def body(j, carry):
        scores(2 * j + 1, sb_ref)
        update(2 * j, sa_ref, False)
        scores(2 * j + 2, sa_ref)
        update(2 * j + 1, sb_ref, False)
        return carry

    lax.fori_loop(0, n_full // 2, body, 0)

    @pl.when(n_full % 2 == 1)
    def _():
        scores(n_full, sb_ref)
        update(n_full - 1, sa_ref, False)
        update(n_full, sb_ref, True)

    @pl.when(n_full % 2 == 0)
    def _():
        update(n_full, sa_ref, True)

    acc = acc_ref[...]
    o = acc[:, :HEAD_DIM] / acc[:, HEAD_DIM:]
    for r in range(NSA_GROUP):
        o_ref[:, r * HEAD_DIM:(r + 1) * HEAD_DIM] = o[r * tq:(r + 1) * tq].astype(o_ref.dtype)


def sel_attn(proj, xsel, kx, batch, seq, tq=256, tk=512):
    n = proj.shape[0]
    tk = min(tk, seq)
    bg = batch * NSA_KV_HEADS
    qt = seq // tq
    rowblk = lambda i, t: ((i // NSA_KV_HEADS) * qt + t, i % NSA_KV_HEADS)
    kern = functools.partial(_sel_attn_kernel, tq=tq, tk=tk)
    return pl.pallas_call(
        kern,
        grid=(bg, qt),
        in_specs=[
            pl.BlockSpec((tq, NSA_GROUP * HEAD_DIM), rowblk),
            pl.BlockSpec((tq, LANES), rowblk),
            pl.BlockSpec((seq, HEAD_DIM), lambda i, t: (i // NSA_KV_HEADS, COL_KSLC + i % NSA_KV_HEADS)),
            pl.BlockSpec((seq, HEAD_DIM), lambda i, t: (i // NSA_KV_HEADS, COL_VSLC + i % NSA_KV_HEADS)),
            pl.BlockSpec((seq, LANES), lambda i, t: (0, 0)),
        ],
        out_specs=pl.BlockSpec((tq, NSA_GROUP * HEAD_DIM), rowblk),
        out_shape=jax.ShapeDtypeStruct((n, NSA_WIDTH), BF16),
        scratch_shapes=[
            pltpu.VMEM((NSA_GROUP * tq, 2 * HEAD_DIM), BF16),
            pltpu.VMEM((NSA_GROUP * tq, tk), F32),
            pltpu.VMEM((NSA_GROUP * tq, tk), F32),
            pltpu.VMEM((NSA_GROUP * tq, LANES), F32),
            pltpu.VMEM((NSA_GROUP * tq, 2 * HEAD_DIM), F32),
        ],
        compiler_params=_cparams("parallel", "parallel"),
        name="sel_attn",
    )(proj, xsel, proj, proj, kx)


def _win_attn_kernel(q_ref, k_ref, v_ref, kx_ref, gl_ref, oc_ref, os_ref, o_ref, *, tq):
    g = pl.program_id(0) % NSA_KV_HEADS
    t0 = pl.program_id(1) * tq
    span = WIN + tq
    starts = [pl.multiple_of(jnp.maximum(t0 - WIN + c * tq, 0), LANES) for c in range(span // tq)]
    k2 = jnp.concatenate([
        jnp.concatenate([k_ref[pl.ds(st, tq), :], kx_ref[pl.ds(st, tq), :]], axis=1)
        for st in starts], axis=0)
    v2 = jnp.concatenate([
        jnp.concatenate([v_ref[pl.ds(st, tq), :] for st in starts], axis=0),
        jnp.ones((span, LANES), BF16)], axis=1)
    tpos = t0 + lax.broadcasted_iota(jnp.int32, (tq, span), 0)
    kpos = (t0 - WIN) + lax.broadcasted_iota(jnp.int32, (tq, span), 1)
    dist = tpos - kpos
    bias = jnp.where((kpos >= 0) & (dist >= 0) & (dist < WIN), 0.0, NEG_INF)
    lane = lax.broadcasted_iota(jnp.int32, (tq, LANES), 1)
    tpos_lane = t0 + lax.broadcasted_iota(jnp.int32, (tq, LANES), 0)
    q2 = jnp.concatenate([
        jnp.concatenate([q_ref[:, r * HEAD_DIM:(r + 1) * HEAD_DIM],
                         _alibi_lanes(_group_slope(g, r), tpos_lane, lane).astype(BF16)], axis=1)
        for r in range(NSA_GROUP)], axis=0)
    s = _nt_dot(q2, k2)
    parts = []
    for r in range(NSA_GROUP):
        sr = s[r * tq:(r + 1) * tq] + bias
        parts.append(jnp.exp(sr - jnp.max(sr, axis=-1, keepdims=True)).astype(BF16))
    ov = _dot(jnp.concatenate(parts, axis=0), v2)
    o_win = ov[:, :HEAD_DIM] / ov[:, HEAD_DIM:]
    gates = jax.nn.sigmoid(gl_ref[...].astype(F32))
    for r in range(NSA_GROUP):
        cols = slice(r * HEAD_DIM, (r + 1) * HEAD_DIM)
        o = (gates[:, 3 * r:3 * r + 1] * oc_ref[:, cols].astype(F32)
             + gates[:, 3 * r + 1:3 * r + 2] * os_ref[:, cols].astype(F32)
             + gates[:, 3 * r + 2:3 * r + 3] * o_win[r * tq:(r + 1) * tq])
        o_ref[:, cols] = o.astype(o_ref.dtype)


def win_attn(proj, kx, o_cmp, o_slc, batch, seq, tq=256):
    n = proj.shape[0]
    bg = batch * NSA_KV_HEADS
    qt = seq // tq
    rowblk = lambda i, t: ((i // NSA_KV_HEADS) * qt + t, i % NSA_KV_HEADS)
    kern = functools.partial(_win_attn_kernel, tq=tq)
    return pl.pallas_call(
        kern,
        grid=(bg, qt),
        in_specs=[
            pl.BlockSpec((tq, NSA_GROUP * HEAD_DIM), rowblk),
            pl.BlockSpec((seq, HEAD_DIM), lambda i, t: (i // NSA_KV_HEADS, COL_KWIN + i % NSA_KV_HEADS)),
            pl.BlockSpec((seq, HEAD_DIM), lambda i, t: (i // NSA_KV_HEADS, COL_VWIN + i % NSA_KV_HEADS)),
            pl.BlockSpec((seq, LANES), lambda i, t: (0, 0)),
            pl.BlockSpec((tq, LANES), lambda i, t: ((i // NSA_KV_HEADS) * qt + t, COL_GATE + i % NSA_KV_HEADS)),
            pl.BlockSpec((tq, NSA_GROUP * HEAD_DIM), rowblk),
            pl.BlockSpec((tq, NSA_GROUP * HEAD_DIM), rowblk),
        ],
        out_specs=pl.BlockSpec((tq, NSA_GROUP * HEAD_DIM), rowblk),
        out_shape=jax.ShapeDtypeStruct((n, NSA_WIDTH), BF16),
        compiler_params=_cparams("parallel", "parallel"),
        name="win_attn",
    )(proj, proj, proj, kx, proj, o_cmp, o_slc)


def _gmlp_kernel(u_ref, v_ref, lg_ref, lb_ref, ws_ref, bsx_ref, o_ref, *, tm):
    u = jax.nn.gelu(u_ref[...].astype(F32))
    v = jax.nn.gelu(v_ref[...].astype(F32))
    mu = jnp.mean(v, axis=-1, keepdims=True)
    vc = v - mu
    var = jnp.mean(vc * vc, axis=-1, keepdims=True)
    vn = (vc * lax.rsqrt(var + EPS) * lg_ref[...] + lb_ref[...]).astype(BF16)
    c = GMLP_CHUNK
    tri = (lax.broadcasted_iota(jnp.int32, (c, c), 0) >= lax.broadcasted_iota(jnp.int32, (c, c), 1))
    bsx = bsx_ref[...]
    for gi in range(GMLP_GROUPS):
        w = jnp.where(tri, ws_ref[gi], 0.0).astype(BF16)
        cols = slice(gi * HEAD_DIM, (gi + 1) * HEAD_DIM)
        for ci in range(tm // c):
            rows = slice(ci * c, (ci + 1) * c)
            s = _dot(w, vn[rows, cols]) + bsx[:, cols]
            o_ref[rows, cols] = (u[rows, cols] * s).astype(o_ref.dtype)


def gmlp(proj, ln_g, ln_b, ws, bsx, tm=512):
    n = proj.shape[0]
    ublk = GMLP_WIDTH // LANES
    kern = functools.partial(_gmlp_kernel, tm=tm)
    return pl.pallas_call(
        kern,
        grid=(n // tm,),
        in_specs=[
            pl.BlockSpec((tm, GMLP_WIDTH), lambda i: (i, COL_U // ublk)),
            pl.BlockSpec((tm, GMLP_WIDTH), lambda i: (i, COL_V // ublk)),
            pl.BlockSpec((1, GMLP_WIDTH), lambda i: (0, 0)),
            pl.BlockSpec((1, GMLP_WIDTH), lambda i: (0, 0)),
            pl.BlockSpec((GMLP_GROUPS, GMLP_CHUNK, GMLP_CHUNK), lambda i: (0, 0, 0)),
            pl.BlockSpec((GMLP_CHUNK, GMLP_WIDTH), lambda i: (0, 0)),
        ],
        out_specs=pl.BlockSpec((tm, GMLP_WIDTH), lambda i: (i, 0)),
        out_shape=jax.ShapeDtypeStruct((n, GMLP_WIDTH), BF16),
        compiler_params=_cparams("parallel"),
        name="gmlp",
    )(proj, proj, ln_g, ln_b, ws, bsx)


def _retention_kernel(q_ref, k_ref, v_ref, g_ref, dec_ref, zeta_ref, xi_ref, dch_ref, gn_ref,
                      o_ref, s_ref, *, seq):
    c = RET_CHUNK
    s_ref[...] = jnp.zeros(s_ref.shape, F32)

    def body(ci, carry):
        rows = pl.ds(pl.multiple_of(ci * c, c), c)
        for hh in range(RET_HEADS):
            cols = slice(hh * HEAD_DIM, (hh + 1) * HEAD_DIM)
            q = q_ref[rows, cols]
            k = k_ref[rows, cols]
            v = v_ref[rows, cols]
            state = s_ref[hh]
            scores = _nt_dot(q, k) * dec_ref[hh]
            y = _dot(scores.astype(BF16), v) + _dot(q, state.astype(BF16)) * xi_ref[hh]
            kz = (k.astype(F32) * zeta_ref[hh]).astype(BF16)
            kv = lax.dot_general(kz, v, (((0,), (0,)), ((), ())), preferred_element_type=F32)
            s_ref[hh] = dch_ref[hh] * state + kv
            mu = jnp.mean(y, axis=-1, keepdims=True)
            yc = y - mu
            var = jnp.mean(yc * yc, axis=-1, keepdims=True)
            yn = yc * lax.rsqrt(var + EPS) * gn_ref[:, cols]
            o_ref[rows, cols] = (jax.nn.silu(g_ref[rows, cols].astype(F32)) * yn).astype(o_ref.dtype)
        return carry

    lax.fori_loop(0, seq // c, body, 0)


def retention(proj, tabs, gn_g, batch, seq):
    n = proj.shape[0]
    dec, zeta, xi, dch = tabs
    rblk = RET_WIDTH // LANES
    kern = functools.partial(_retention_kernel, seq=seq)
    full = lambda shape: pl.BlockSpec(shape, lambda b: (0,) * len(shape))
    return pl.pallas_call(
        kern,
        grid=(batch,),
        in_specs=[
            pl.BlockSpec((seq, RET_WIDTH), lambda b: (b, COL_RQ // rblk)),
            pl.BlockSpec((seq, RET_WIDTH), lambda b: (b, COL_RK // rblk)),
            pl.BlockSpec((seq, RET_WIDTH), lambda b: (b, COL_RV // rblk)),
            pl.BlockSpec((seq, RET_WIDTH), lambda b: (b, COL_RG // rblk)),
            full(dec.shape), full(zeta.shape), full(xi.shape), full(dch.shape),
            full((1, RET_WIDTH)),
        ],
        out_specs=pl.BlockSpec((seq, RET_WIDTH), lambda b: (b, 0)),
        out_shape=jax.ShapeDtypeStruct((n, RET_WIDTH), BF16),
        scratch_shapes=[pltpu.VMEM((RET_HEADS, HEAD_DIM, HEAD_DIM), F32)],
        compiler_params=_cparams("parallel"),
        name="retention",
    )(proj, proj, proj, proj, dec, zeta, xi, dch, gn_g)


ROUTE_E1, ROUTE_E2, ROUTE_G1, ROUTE_G2 = 0, 1, 2, 3


def _top2_route(logits):
    lane = lax.broadcasted_iota(jnp.int32, logits.shape, 1)
    lg = jnp.where(lane < N_EXPERTS, logits, -jnp.inf)
    v1 = jnp.max(lg, axis=-1, keepdims=True)
    i1 = jnp.min(jnp.where(lg == v1, lane, LANES), axis=-1, keepdims=True)
    lg2 = jnp.where(lane == i1, -jnp.inf, lg)
    v2 = jnp.max(lg2, axis=-1, keepdims=True)
    i2 = jnp.min(jnp.where(lg2 == v2, lane, LANES), axis=-1, keepdims=True)
    e2 = jnp.exp(v2 - v1)
    den = 1.0 + e2
    out = jnp.where(lane == ROUTE_E1, i1.astype(F32), 0.0)
    out = jnp.where(lane == ROUTE_E2, i2.astype(F32), out)
    out = jnp.where(lane == ROUTE_G1, 1.0 / den, out)
    return jnp.where(lane == ROUTE_G2, e2 / den, out)


def _out_proj_kernel(*refs, router):
    if router:
        nsa_ref, gm_ref, ret_ref, h_ref, w_ref, g2_ref, wrh_ref, wrl_ref, br_ref, ho_ref, f_ref, rt_ref = refs
    else:
        nsa_ref, gm_ref, ret_ref, h_ref, w_ref, g2_ref, ho_ref, f_ref = refs
    acc = _dot(nsa_ref[...], w_ref[:NSA_WIDTH, :])
    acc = acc + _dot(gm_ref[...], w_ref[NSA_WIDTH:NSA_WIDTH + GMLP_WIDTH, :])
    acc = acc + _dot(ret_ref[...], w_ref[NSA_WIDTH + GMLP_WIDTH:, :])
    hn = h_ref[...] + acc
    ho_ref[...] = hn
    ms = jnp.mean(hn * hn, axis=-1, keepdims=True)
    f = hn * lax.rsqrt(ms + EPS) * g2_ref[...]
    f_hi = f.astype(BF16)
    if router:
        f_ref[...] = f
        f_lo = (f - f_hi.astype(F32)).astype(BF16)
        logits = (_dot(f_hi, wrh_ref[...]) + _dot(f_lo, wrh_ref[...]) + _dot(f_hi, wrl_ref[...])
                  + br_ref[...])
        rt_ref[...] = _top2_route(logits)
    else:
        f_ref[...] = f_hi


def out_proj(o_nsa, o_gm, o_ret, h2, w, g2, router_w=None, tm=512):
    n, d = h2.shape
    tm = min(tm, n)
    router = router_w is not None
    row = lambda width: pl.BlockSpec((tm, width), lambda i: (i, 0))
    full = lambda shape: pl.BlockSpec(shape, lambda i: (0,) * len(shape))
    in_specs = [row(NSA_WIDTH), row(GMLP_WIDTH), row(RET_WIDTH), row(d), full(w.shape), full((1, d))]
    args = [o_nsa, o_gm, o_ret, h2, w, g2]
    if router:
        in_specs += [full((d, LANES)), full((d, LANES)), full((1, LANES))]
        args += list(router_w)
        out_specs = [row(d), row(d), row(LANES)]
        out_shape = [jax.ShapeDtypeStruct((n, d), F32), jax.ShapeDtypeStruct((n, d), F32),
                     jax.ShapeDtypeStruct((n, LANES), F32)]
    else:
        out_specs = [row(d), row(d)]
        out_shape = [jax.ShapeDtypeStruct((n, d), F32), jax.ShapeDtypeStruct((n, d), BF16)]
    return pl.pallas_call(
        functools.partial(_out_proj_kernel, router=router),
        grid=(n // tm,),
        in_specs=in_specs,
        out_specs=out_specs,
        out_shape=out_shape,
        compiler_params=_cparams("parallel"),
        name="out_proj_router" if router else "out_proj",
    )(*args)


def _ffn_kernel(*refs, final):
    if final:
        x_ref, h_ref, w1_ref, w3_ref, w2_ref, fg_ref, o_ref = refs
    else:
        x_ref, h_ref, w1_ref, w3_ref, w2_ref, o_ref = refs
    k = pl.program_id(1)

    @pl.when(k == 0)
    def _():
        o_ref[...] = h_ref[...]

    x = x_ref[...]
    hid = jax.nn.silu(_dot(x, w1_ref[...])) * _dot(x, w3_ref[...])
    o_ref[...] += _dot(hid.astype(BF16), w2_ref[...])

    if final:
        @pl.when(k == pl.num_programs(1) - 1)
        def _():
            y = o_ref[...]
            ms = jnp.mean(y * y, axis=-1, keepdims=True)
            o_ref[...] = y * lax.rsqrt(ms + EPS) * fg_ref[...]


def ffn(f, h2, w1, w3, w2, final_g=None, tm=512, tf=512):
    n, d = h2.shape
    ff = w1.shape[1]
    tm = min(tm, n)
    tf = min(tf, ff)
    in_specs = [
        pl.BlockSpec((tm, d), lambda i, k: (i, 0)),
        pl.BlockSpec((tm, d), lambda i, k: (i, 0)),
        pl.BlockSpec((d, tf), lambda i, k: (0, k)),
        pl.BlockSpec((d, tf), lambda i, k: (0, k)),
        pl.BlockSpec((tf, d), lambda i, k: (k, 0)),
    ]
    args = [f, h2, w1, w3, w2]
    if final_g is not None:
        in_specs.append(pl.BlockSpec((1, d), lambda i, k: (0, 0)))
        args.append(final_g)
    return pl.pallas_call(
        functools.partial(_ffn_kernel, final=final_g is not None),
        grid=(n // tm, ff // tf),
        in_specs=in_specs,
        out_specs=pl.BlockSpec((tm, d), lambda i, k: (i, 0)),
        out_shape=jax.ShapeDtypeStruct((n, d), F32),
        compiler_params=_cparams("parallel", "arbitrary"),
        name="ffn",
    )(*args)


MOE_TM = 512


def _row_copy(src_ref, src_row, dst_ref, dst_row, sem):
    return pltpu.make_async_copy(src_ref.at[pl.ds(src_row, 1)], dst_ref.at[pl.ds(dst_row, 1)], sem)


def _dispatch_kernel(dest_ref, f_ref, xg_in_ref, xg_ref, sem, *, tm):
    del xg_in_ref
    base = pl.program_id(0) * (tm * TOP_K)

    def issue(j, carry):
        for s in range(TOP_K):
            _row_copy(f_ref, j, xg_ref, dest_ref[base + j * TOP_K + s], sem).start()
        return carry

    lax.fori_loop(0, tm, issue, 0, unroll=8)

    def drain(j, carry):
        for s in range(TOP_K):
            _row_copy(f_ref, 0, xg_ref, 0, sem).wait()
        return carry

    lax.fori_loop(0, tm, drain, 0, unroll=8)


def moe_dispatch(dest, f, rows, tm=256):
    n, d = f.shape
    tm = min(tm, n)
    xg0 = jnp.zeros((rows, d), F32)
    return pl.pallas_call(
        functools.partial(_dispatch_kernel, tm=tm),
        grid_spec=pltpu.PrefetchScalarGridSpec(
            num_scalar_prefetch=1,
            grid=(n // tm,),
            in_specs=[
                pl.BlockSpec((tm, d), lambda i, dest: (i, 0)),
                pl.BlockSpec(memory_space=pl.ANY),
            ],
            out_specs=pl.BlockSpec(memory_space=pl.ANY),
            scratch_shapes=[pltpu.SemaphoreType.DMA(())],
        ),
        out_shape=jax.ShapeDtypeStruct((rows, d), F32),
        input_output_aliases={2: 0},
        compiler_params=_cparams("arbitrary"),
        name="moe_dispatch",
    )(dest, f, xg0)


def _moe_ffn_kernel(te_ref, nu_ref, x_ref, w1_ref, w3_ref, w2_ref, o_ref, xb_ref):
    del te_ref
    k = pl.program_id(1)
    used = pl.program_id(0) < nu_ref[0]

    @pl.when(k == 0)
    def _():
        xb_ref[...] = x_ref[...].astype(BF16)
        o_ref[...] = jnp.zeros(o_ref.shape, F32)

    @pl.when(used)
    def _():
        x = xb_ref[...]
        hid = jax.nn.silu(_dot(x, w1_ref[0])) * _dot(x, w3_ref[0])
        o_ref[...] += _dot(hid.astype(BF16), w2_ref[0])


def moe_ffn(tile_expert, n_used, xg, w1, w3, w2, tm=MOE_TM, tf=256):
    rows, d = xg.shape
    ff = w1.shape[2]
    tf = min(tf, ff)
    nk = ff // tf
    last = lambda i, nu: jnp.minimum(i, nu[0] - 1)
    kk = lambda i, k, nu: jnp.where(i < nu[0], k, nk - 1)
    return pl.pallas_call(
        _moe_ffn_kernel,
        grid_spec=pltpu.PrefetchScalarGridSpec(
            num_scalar_prefetch=2,
            grid=(rows // tm, nk),
            in_specs=[
                pl.BlockSpec((tm, d), lambda i, k, te, nu: (last(i, nu), 0)),
                pl.BlockSpec((1, d, tf), lambda i, k, te, nu: (te[i], 0, kk(i, k, nu))),
                pl.BlockSpec((1, d, tf), lambda i, k, te, nu: (te[i], 0, kk(i, k, nu))),
                pl.BlockSpec((1, tf, d), lambda i, k, te, nu: (te[i], kk(i, k, nu), 0)),
            ],
            out_specs=pl.BlockSpec((tm, d), lambda i, k, te, nu: (i, 0)),
            scratch_shapes=[pltpu.VMEM((tm, d), BF16)],
        ),
        out_shape=jax.ShapeDtypeStruct((rows, d), F32),
        compiler_params=_cparams("arbitrary", "arbitrary"),
        name="moe_ffn",
    )(tile_expert, n_used, xg, w1, w3, w2)


def _combine_kernel(*refs, tm, final):
    if final:
        dest_ref, h_ref, rt_ref, yg_ref, fg_ref, o_ref, buf_ref, sem = refs
    else:
        dest_ref, h_ref, rt_ref, yg_ref, o_ref, buf_ref, sem = refs
    base = pl.program_id(0) * (tm * TOP_K)

    def issue(j, carry):
        for s in range(TOP_K):
            _row_copy(yg_ref, dest_ref[base + j * TOP_K + s], buf_ref.at[s], j, sem).start()
        return carry

    lax.fori_loop(0, tm, issue, 0, unroll=8)

    def drain(j, carry):
        for s in range(TOP_K):
            _row_copy(yg_ref, 0, buf_ref.at[s], 0, sem).wait()
        return carry

    lax.fori_loop(0, tm, drain, 0, unroll=8)

    rt = rt_ref[...]
    y = (h_ref[...] + rt[:, ROUTE_G1:ROUTE_G1 + 1] * buf_ref[0]
         + rt[:, ROUTE_G2:ROUTE_G2 + 1] * buf_ref[1])
    if final:
        ms = jnp.mean(y * y, axis=-1, keepdims=True)
        y = y * lax.rsqrt(ms + EPS) * fg_ref[...]
    o_ref[...] = y


def moe_combine(dest, h2, route, yg, final_g=None, tm=256):
    n, d = h2.shape
    tm = min(tm, n)
    in_specs = [
        pl.BlockSpec((tm, d), lambda i, dest: (i, 0)),
        pl.BlockSpec((tm, LANES), lambda i, dest: (i, 0)),
        pl.BlockSpec(memory_space=pl.ANY),
    ]
    args = [dest, h2, route, yg]
    if final_g is not None:
        in_specs.append(pl.BlockSpec((1, d), lambda i, dest: (0, 0)))
        args.append(final_g)
    return pl.pallas_call(
        functools.partial(_combine_kernel, tm=tm, final=final_g is not None),
        grid_spec=pltpu.PrefetchScalarGridSpec(
            num_scalar_prefetch=1,
            grid=(n // tm,),
            in_specs=in_specs,
            out_specs=pl.BlockSpec((tm, d), lambda i, dest: (i, 0)),
            scratch_shapes=[pltpu.VMEM((TOP_K, tm, d), F32), pltpu.SemaphoreType.DMA(())],
        ),
        out_shape=jax.ShapeDtypeStruct((n, d), F32),
        compiler_params=_cparams("arbitrary"),
        name="moe_combine",
    )(*args)


def _routing_tables(route, tm):
    n = route.shape[0]
    flat_e = route[:, ROUTE_E1:ROUTE_E2 + 1].astype(jnp.int32).reshape(-1)
    onehot = (flat_e[:, None] == jnp.arange(N_EXPERTS, dtype=jnp.int32)[None, :]).astype(jnp.int32)
    csum = jnp.cumsum(onehot, axis=0)
    rank = jnp.sum(csum * onehot, axis=1) - 1
    counts = csum[-1]
    padded = ((counts + tm - 1) // tm) * tm
    off_end = jnp.cumsum(padded)
    dest = (off_end - padded)[flat_e] + rank
    n_tiles = (n * TOP_K) // tm + N_EXPERTS
    n_used = (off_end[-1] // tm).astype(jnp.int32)
    tile_start = jnp.arange(n_tiles, dtype=jnp.int32) * tm
    tile_start = jnp.minimum(tile_start, off_end[-1] - tm)
    tile_expert = jnp.sum((tile_start[:, None] >= off_end[None, :]).astype(jnp.int32), axis=1)
    return dest.astype(jnp.int32), tile_expert.astype(jnp.int32), n_used.reshape(1), n_tiles * tm


def moe_layer(h2, f, route, w1, w3, w2, final_g=None):
    dest, tile_expert, n_used, rows = _routing_tables(route, MOE_TM)
    xg = moe_dispatch(dest, f, rows)
    yg = moe_ffn(tile_expert, n_used, xg, w1, w3, w2)
    return moe_combine(dest, h2, route, yg, final_g=final_g)


def _reorder_w_in(w_in):
    d = w_in.shape[0]
    kvw = NSA_KV_WIDTH
    o_gate = NSA_WIDTH + 6 * kvw
    o_uv = o_gate + 3 * NSA_HEADS
    o_ret = o_uv + 2 * GMLP_WIDTH
    gate = w_in[:, o_gate:o_uv].reshape(d, NSA_KV_HEADS, 3 * NSA_GROUP)
    gate = jnp.pad(gate, ((0, 0), (0, 0), (0, LANES - 3 * NSA_GROUP))).reshape(d, NSA_KV_HEADS * LANES)
    parts = [w_in[:, :o_gate], w_in[:, o_uv:], gate]
    w = jnp.concatenate(parts, axis=1)
    w = jnp.pad(w, ((0, 0), (0, PROJ_COLS - w.shape[1])))
    scale = np.ones((1, PROJ_COLS), np.float32)
    scale[:, COL_Q * LANES:(COL_Q + NSA_HEADS) * LANES] = HEAD_DIM ** -0.5
    scale[:, COL_RK * LANES:(COL_RK + RET_HEADS) * LANES] = HEAD_DIM ** -0.5
    del o_ret
    return w.astype(BF16), jnp.asarray(scale)


def _key_extra_lanes(seq):
    pos = np.arange(seq)
    kx = np.zeros((seq, LANES), np.float32)
    kx[pos, pos // SLC_LEN] = 1.0
    kx[:, XL_HI] = pos // SLC_LEN
    kx[:, XL_LO] = pos % SLC_LEN
    kx[:, XL_ONE_A] = 1.0
    kx[:, XL_ONE_B] = 1.0
    return jnp.asarray(kx, BF16)


def _share_t(seq):
    nrow = seq // CMP_STRIDE
    nb = seq // SLC_LEN
    c0 = np.arange(nrow)[None, :] * CMP_STRIDE
    s0 = np.arange(nb)[:, None] * SLC_LEN
    overlap = np.minimum(c0 + CMP_LEN, s0 + SLC_LEN) - np.maximum(c0, s0)
    share = np.clip(overlap, 0, CMP_LEN).astype(np.float32) / CMP_LEN
    share[:, nrow - 1] = 0.0
    return jnp.asarray(share, BF16)


def _retention_tables():
    hh = np.arange(RET_HEADS, dtype=np.float64)
    lg = np.log1p(-np.exp2(-5.0 - hh))
    nn = np.arange(RET_CHUNK, dtype=np.float64)
    rel = nn[:, None] - nn[None, :]
    dec = np.where(rel >= 0, np.exp(lg[:, None, None] * np.maximum(rel, 0.0)), 0.0)
    zeta = np.exp(lg[:, None] * (RET_CHUNK - 1.0 - nn))[:, :, None]
    xi = np.exp(lg[:, None] * (nn + 1.0))[:, :, None]
    dch = np.broadcast_to(np.exp(lg * RET_CHUNK)[:, None, None], (RET_HEADS, 1, HEAD_DIM))
    return tuple(jnp.asarray(a, F32) for a in (dec, zeta, xi, dch))


def _token_mixer(h2, batch, seq, ln1_g, w_in, pe_k, w1_k, w2_k, pe_v, w1_v, w2_v,
                 g_ln_g, g_ln_b, g_ws, g_bs, ret_gn_g, consts):
    kx, sht, ret_tabs = consts
    w_r, colscale = _reorder_w_in(w_in)
    proj = in_proj(h2, ln1_g[None, :], w_r, colscale)
    nrow = seq // CMP_STRIDE
    kv = proj[:, COL_KCMP * LANES:(COL_VCMP + NSA_KV_HEADS) * LANES]
    kv = kv.reshape(batch, nrow, CMP_STRIDE, 2, NSA_KV_HEADS, HEAD_DIM)
    rows = kv.transpose(3, 0, 4, 1, 2, 5).reshape(2, batch * NSA_KV_HEADS, nrow, CMP_STRIDE * HEAD_DIM)
    pe = jnp.stack([pe_k, pe_v]).reshape(2, 1, CMP_LEN * HEAD_DIM)
    kvc = compress(rows, pe, jnp.stack([w1_k, w1_v]).astype(BF16), jnp.stack([w2_k, w2_v]).astype(BF16))
    o_cmp, xsel = cmp_attn(proj, kvc, sht, batch, seq)
    o_slc = sel_attn(proj, xsel, kx, batch, seq)
    o_nsa = win_attn(proj, kx, o_cmp, o_slc, batch, seq)
    bsx = jnp.repeat(g_bs.T, HEAD_DIM, axis=1)
    o_gm = gmlp(proj, g_ln_g[None, :], g_ln_b[None, :], g_ws, bsx)
    o_ret = retention(proj, ret_tabs, ret_gn_g[None, :], batch, seq)
    return o_nsa, o_gm, o_ret


def kernel(x, ln1_g, w_in, cmp_pe_k, cmp_w1_k, cmp_w2_k, cmp_pe_v, cmp_w1_v, cmp_w2_v, gmlp_ln_g, gmlp_ln_b, gmlp_ws, gmlp_bs, ret_gn_g, w_out, ln2_g, ffn_w1, ffn_w3, ffn_w2, moe_wr, moe_br, moe_w1, moe_w3, moe_w2, final_g):
    batch, seq, d = x.shape
    depth = w_in.shape[0]
    consts = (_key_extra_lanes(seq), _share_t(seq), _retention_tables())
    h2 = x.reshape(batch * seq, d)
    for layer in range(depth):
        o_nsa, o_gm, o_ret = _token_mixer(
            h2, batch, seq, ln1_g[layer], w_in[layer], cmp_pe_k[layer], cmp_w1_k[layer], cmp_w2_k[layer],
            cmp_pe_v[layer], cmp_w1_v[layer], cmp_w2_v[layer], gmlp_ln_g[layer], gmlp_ln_b[layer],
            gmlp_ws[layer], gmlp_bs[layer], ret_gn_g[layer], consts)
        fin = final_g[None, :] if layer == depth - 1 else None
        i = layer // 2
        w_o = w_out[layer].astype(BF16)
        if layer % 2 == 0:
            h2, f = out_proj(o_nsa, o_gm, o_ret, h2, w_o, ln2_g[layer][None, :])
            h2 = ffn(f, h2, ffn_w1[i].astype(BF16), ffn_w3[i].astype(BF16), ffn_w2[i].astype(BF16),
                     final_g=fin)
        else:
            wr = jnp.pad(moe_wr[i], ((0, 0), (0, LANES - N_EXPERTS)))
            wr_hi = wr.astype(BF16)
            wr_lo = (wr - wr_hi.astype(F32)).astype(BF16)
            br = jnp.pad(moe_br[i], (0, LANES - N_EXPERTS))[None, :]
            h2, f, route = out_proj(o_nsa, o_gm, o_ret, h2, w_o, ln2_g[layer][None, :],
                                    router_w=(wr_hi, wr_lo, br))
            h2 = moe_layer(h2, f, route, moe_w1[i].astype(BF16), moe_w3[i].astype(BF16),
                           moe_w2[i].astype(BF16), final_g=fin)
    return h2.reshape(batch, seq, d)
```

```python
import functools

import numpy as np
import jax
import jax.numpy as jnp
from jax import lax
from jax.experimental import pallas as pl
from jax.experimental.pallas import tpu as pltpu

F32 = jnp.float32
BF16 = jnp.bfloat16

HEAD_DIM = 128
NSA_HEADS = 8
NSA_KV_HEADS = 2
NSA_GROUP = NSA_HEADS // NSA_KV_HEADS
NSA_WIDTH = NSA_HEADS * HEAD_DIM
NSA_KV_WIDTH = NSA_KV_HEADS * HEAD_DIM
CMP_LEN = 32
CMP_STRIDE = 16
SLC_LEN = 64
SLC_TOPK = 16
WIN = 512
GMLP_GROUPS = 4
GMLP_CHUNK = 128
GMLP_WIDTH = GMLP_GROUPS * HEAD_DIM
RET_HEADS = 4
RET_CHUNK = 128
RET_WIDTH = RET_HEADS * HEAD_DIM
N_EXPERTS = 8
TOP_K = 2
EPS = 1e-6
NEG_INF = -1e30
FORCE_SCORE = 1e4

LANES = 128
VMEM_LIMIT = 56 * 1024 * 1024

COL_Q = 0
COL_KCMP = 8
COL_VCMP = 10
COL_KSLC = 12
COL_VSLC = 14
COL_KWIN = 16
COL_VWIN = 18
COL_U = 20
COL_V = 24
COL_RQ = 28
COL_RK = 32
COL_RV = 36
COL_RG = 40
COL_GATE = 44
PROJ_COLS = 48 * LANES

XL_HI = 64
XL_LO = 65
XL_ONE_A = 66
XL_ONE_B = 67


def _cparams(*sem):
    return pltpu.CompilerParams(dimension_semantics=sem, vmem_limit_bytes=VMEM_LIMIT)


def _nt_dot(a, b):
    return lax.dot_general(a, b, (((1,), (1,)), ((), ())), preferred_element_type=F32)


def _dot(a, b):
    return jnp.dot(a, b, preferred_element_type=F32)


def _in_proj_kernel(x_ref, g_ref, w_ref, cs_ref, o_ref, xn_ref):
    @pl.when(pl.program_id(1) == 0)
    def _():
        x = x_ref[...]
        ms = jnp.mean(x * x, axis=-1, keepdims=True)
        xn_ref[...] = (x * lax.rsqrt(ms + EPS) * g_ref[...]).astype(BF16)

    acc = _dot(xn_ref[...], w_ref[...])
    o_ref[...] = (acc * cs_ref[...]).astype(o_ref.dtype)


def in_proj(h2, g, w, colscale, tm=1024, tn=1024):
    n, d = h2.shape
    cols = w.shape[1]
    tm = min(tm, n)
    return pl.pallas_call(
        _in_proj_kernel,
        grid=(n // tm, cols // tn),
        in_specs=[
            pl.BlockSpec((tm, d), lambda i, j: (i, 0)),
            pl.BlockSpec((1, d), lambda i, j: (0, 0)),
            pl.BlockSpec((d, tn), lambda i, j: (0, j)),
            pl.BlockSpec((1, tn), lambda i, j: (0, j)),
        ],
        out_specs=pl.BlockSpec((tm, tn), lambda i, j: (i, j)),
        out_shape=jax.ShapeDtypeStruct((n, cols), BF16),
        scratch_shapes=[pltpu.VMEM((tm, d), BF16)],
        compiler_params=_cparams("parallel", "arbitrary"),
        name="in_proj",
    )(h2, g, w, colscale)


def _compress_kernel(r_ref, pe_ref, w1_ref, w2_ref, o_ref):
    half = CMP_STRIDE * HEAD_DIM
    r = r_ref[0, 0].astype(F32)
    nrow = r.shape[0]
    pe = pe_ref[0]
    a = _dot((r + pe[:, :half]).astype(BF16), w1_ref[0, :half, :])
    b = _dot((r + pe[:, half:]).astype(BF16), w1_ref[0, half:, :])
    pre = a + pltpu.roll(b, nrow - 1, 0)
    y = _dot(jax.nn.gelu(pre).astype(BF16), w2_ref[0])
    row = lax.broadcasted_iota(jnp.int32, y.shape, 0)
    o_ref[0, 0] = jnp.where(row < nrow - 1, y, 0.0).astype(o_ref.dtype)


def compress(rows, pe, w1, w2):
    _, bg, nrow, width = rows.shape
    return pl.pallas_call(
        _compress_kernel,
        grid=(2, bg),
        in_specs=[
            pl.BlockSpec((1, 1, nrow, width), lambda s, i: (s, i, 0, 0)),
            pl.BlockSpec((1, 1, 2 * width), lambda s, i: (s, 0, 0)),
            pl.BlockSpec((1, 2 * width, HEAD_DIM), lambda s, i: (s, 0, 0)),
            pl.BlockSpec((1, HEAD_DIM, HEAD_DIM), lambda s, i: (s, 0, 0)),
        ],
        out_specs=pl.BlockSpec((1, 1, nrow, HEAD_DIM), lambda s, i: (s, i, 0, 0)),
        out_shape=jax.ShapeDtypeStruct((2, bg, nrow, HEAD_DIM), BF16),
        compiler_params=_cparams("arbitrary", "arbitrary"),
        name="compress",
    )(rows, pe, w1, w2)


def _group_slope(g, r):
    return jnp.where(g == 0, F32(2.0 ** -(r + 1)), F32(2.0 ** -(r + 1 + NSA_GROUP)))


def _cmp_attn_kernel(q_ref, kc_ref, vc_ref, sht_ref, o_ref, x_ref, *, tq, nb, topk):
    g = pl.program_id(0) % NSA_KV_HEADS
    t0 = pl.program_id(1) * tq
    kc = kc_ref[0, 0]
    vc = vc_ref[0, 0]
    nrow = kc.shape[0]
    tpos = t0 + lax.broadcasted_iota(jnp.int32, (tq, nrow), 0)
    col = lax.broadcasted_iota(jnp.int32, (tq, nrow), 1)
    dist = tpos - (col * CMP_STRIDE + (CMP_LEN - 1))
    valid = (dist >= 0) & (col < nrow - 1)
    distf = dist.astype(F32)
    psum = jnp.zeros((tq, nrow), F32)
    for r in range(NSA_GROUP):
        q = q_ref[:, r * HEAD_DIM:(r + 1) * HEAD_DIM]
        s = _nt_dot(q, kc) - _group_slope(g, r) * distf
        s = jnp.where(valid, s, NEG_INF)
        e = jnp.exp(s - jnp.max(s, axis=-1, keepdims=True))
        p = e / jnp.sum(e, axis=-1, keepdims=True)
        p = jnp.where(valid, p, 0.0)
        o_ref[:, r * HEAD_DIM:(r + 1) * HEAD_DIM] = _dot(p.astype(BF16), vc).astype(o_ref.dtype)
        psum = psum + p
    p_hi = psum.astype(BF16)
    p_lo = (psum - p_hi.astype(F32)).astype(BF16)
    sht = sht_ref[...]
    imp = _nt_dot(sht, p_hi) + _nt_dot(sht, p_lo)
    j = lax.broadcasted_iota(jnp.int32, (nb, tq), 0)
    cur = (t0 + lax.broadcasted_iota(jnp.int32, (nb, tq), 1)) // SLC_LEN
    forced = (j == 0) | (j == cur) | (j == cur - 1)
    score = jnp.where(forced, FORCE_SCORE, imp)
    cand = jnp.where(j > cur, -3e38, score)
    jf = j.astype(F32)
    for _ in range(topk):
        best = jnp.max(cand, axis=0, keepdims=True)
        idx = jnp.min(jnp.where(cand == best, jf, float(nb)), axis=0, keepdims=True)
        cand = jnp.where(jf == idx, -jnp.inf, cand)
    neg = jnp.where(cand == -jnp.inf, 0.0, NEG_INF)
    neg = jnp.concatenate([neg, jnp.zeros((LANES - nb, tq), F32)], axis=0)
    x_ref[...] = neg.T.astype(x_ref.dtype)


def cmp_attn(proj, kvc, sht, batch, seq, tq=256):
    n = proj.shape[0]
    nb = seq // SLC_LEN
    bg = batch * NSA_KV_HEADS
    nrow = kvc.shape[2]
    qt = seq // tq
    rowblk = lambda i, t: ((i // NSA_KV_HEADS) * qt + t, i % NSA_KV_HEADS)
    kern = functools.partial(_cmp_attn_kernel, tq=tq, nb=nb, topk=min(SLC_TOPK, nb))
    return pl.pallas_call(
        kern,
        grid=(bg, qt),
        in_specs=[
            pl.BlockSpec((tq, NSA_GROUP * HEAD_DIM), rowblk),
            pl.BlockSpec((1, 1, nrow, HEAD_DIM), lambda i, t: (0, i, 0, 0)),
            pl.BlockSpec((1, 1, nrow, HEAD_DIM), lambda i, t: (1, i, 0, 0)),
            pl.BlockSpec((nb, nrow), lambda i, t: (0, 0)),
        ],
        out_specs=[
            pl.BlockSpec((tq, NSA_GROUP * HEAD_DIM), rowblk),
            pl.BlockSpec((tq, LANES), rowblk),
        ],
        out_shape=[
            jax.ShapeDtypeStruct((n, NSA_WIDTH), BF16),
            jax.ShapeDtypeStruct((n, NSA_KV_HEADS * LANES), BF16),
        ],
        compiler_params=_cparams("parallel", "parallel"),
        name="cmp_attn",
    )(proj, kvc, kvc, sht)


def _alibi_lanes(slope, tpos_lane, lane):
    hi_t = (tpos_lane // SLC_LEN).astype(F32)
    lo_t = (tpos_lane % SLC_LEN).astype(F32)
    ax = jnp.where(lane == XL_HI, slope * SLC_LEN, 0.0)
    ax = jnp.where(lane == XL_LO, slope, ax)
    ax = jnp.where(lane == XL_ONE_A, -(slope * SLC_LEN) * hi_t, ax)
    ax = jnp.where(lane == XL_ONE_B, -slope * lo_t, ax)
    return ax


def _sel_attn_kernel(q_ref, x_ref, k_ref, v_ref, kx_ref, o_ref, q2_ref, sa_ref, sb_ref, m_ref, acc_ref, *, tq, tk):
    g = pl.program_id(0) % NSA_KV_HEADS
    t0 = pl.program_id(1) * tq
    n_kt = (t0 + tq - 1) // tk + 1
    lane = lax.broadcasted_iota(jnp.int32, (tq, LANES), 1)
    tpos_lane = t0 + lax.broadcasted_iota(jnp.int32, (tq, LANES), 0)
    xsel = x_ref[...].astype(F32)
    rows = NSA_GROUP * tq
    tpos = t0 + (lax.broadcasted_iota(jnp.int32, (rows, tk), 0) & (tq - 1))
    kcol = lax.broadcasted_iota(jnp.int32, (rows, tk), 1)

    parts = []
    for r in range(NSA_GROUP):
        ax = _alibi_lanes(_group_slope(g, r), tpos_lane, lane)
        parts.append(jnp.concatenate(
            [q_ref[:, r * HEAD_DIM:(r + 1) * HEAD_DIM],
             jnp.where(lane < SLC_LEN, xsel, ax).astype(BF16)], axis=1))
    q2_ref[...] = jnp.concatenate(parts, axis=0)
    m_ref[...] = jnp.full(m_ref.shape, -jnp.inf, F32)
    acc_ref[...] = jnp.zeros(acc_ref.shape, F32)
    ones = jnp.ones((tk, LANES), BF16)

    def scores(kt, dst_ref):
        ks = pl.multiple_of(kt * tk, tk)
        k2 = jnp.concatenate([k_ref[pl.ds(ks, tk), :], kx_ref[pl.ds(ks, tk), :]], axis=1)
        dst_ref[...] = _nt_dot(q2_ref[...], k2)

    def update(kt, src_ref, causal):
        ks = pl.multiple_of(kt * tk, tk)
        v2 = jnp.concatenate([v_ref[pl.ds(ks, tk), :], ones], axis=1)
        s = src_ref[...]
        if causal:
            s = jnp.where(ks + kcol <= tpos, s, NEG_INF)
        m_prev = m_ref[...]
        m_new = jnp.maximum(m_prev, jnp.max(s, axis=-1, keepdims=True))
        alpha = jnp.exp(m_prev - m_new)
        p = jnp.exp(s - jnp.concatenate([m_new] * (tk // LANES), axis=1))
        acc_ref[...] = jnp.concatenate([alpha, alpha], axis=1) * acc_ref[...] + _dot(p.astype(BF16), v2)
        m_ref[...] = m_new

    n_full = n_kt - 1
    scores(0, sa_ref)

    def body(j, carry):
        scores(2 * j + 1, sb_ref)
        update(2 * j, sa_ref, False)
        scores(2 * j + 2, sa_ref)
        update(2 * j + 1, sb_ref, False)
        return carry

    lax.fori_loop(0, n_full // 2, body, 0)

    @pl.when(n_full % 2 == 1)
    def _():
        scores(n_full, sb_ref)
        update(n_full - 1, sa_ref, False)
        update(n_full, sb_ref, True)

    @pl.when(n_full % 2 == 0)
    def _():
        update(n_full, sa_ref, True)

    acc = acc_ref[...]
    o = acc[:, :HEAD_DIM] / acc[:, HEAD_DIM:]
    for r in range(NSA_GROUP):
        o_ref[:, r * HEAD_DIM:(r + 1) * HEAD_DIM] = o[r * tq:(r + 1) * tq].astype(o_ref.dtype)


def sel_attn(proj, xsel, kx, batch, seq, tq=256, tk=512):
    n = proj.shape[0]
    tk = min(tk, seq)
    bg = batch * NSA_KV_HEADS
    qt = seq // tq
    rowblk = lambda i, t: ((i // NSA_KV_HEADS) * qt + t, i % NSA_KV_HEADS)
    kern = functools.partial(_sel_attn_kernel, tq=tq, tk=tk)
    return pl.pallas_call(
        kern,
        grid=(bg, qt),
        in_specs=[
            pl.BlockSpec((tq, NSA_GROUP * HEAD_DIM), rowblk),
            pl.BlockSpec((tq, LANES), rowblk),
            pl.BlockSpec((seq, HEAD_DIM), lambda i, t: (i // NSA_KV_HEADS, COL_KSLC + i % NSA_KV_HEADS)),
            pl.BlockSpec((seq, HEAD_DIM), lambda i, t: (i // NSA_KV_HEADS, COL_VSLC + i % NSA_KV_HEADS)),
            pl.BlockSpec((seq, LANES), lambda i, t: (0, 0)),
        ],
        out_specs=pl.BlockSpec((tq, NSA_GROUP * HEAD_DIM), rowblk),
        out_shape=jax.ShapeDtypeStruct((n, NSA_WIDTH), BF16),
        scratch_shapes=[
            pltpu.VMEM((NSA_GROUP * tq, 2 * HEAD_DIM), BF16),
            pltpu.VMEM((NSA_GROUP * tq, tk), F32),
            pltpu.VMEM((NSA_GROUP * tq, tk), F32),
            pltpu.VMEM((NSA_GROUP * tq, LANES), F32),
            pltpu.VMEM((NSA_GROUP * tq, 2 * HEAD_DIM), F32),
        ],
        compiler_params=_cparams("parallel", "parallel"),
        name="sel_attn",
    )(proj, xsel, proj, proj, kx)


def _win_attn_kernel(q_ref, k_ref, v_ref, kx_ref, gl_ref, oc_ref, os_ref, o_ref, *, tq):
    g = pl.program_id(0) % NSA_KV_HEADS
    t0 = pl.program_id(1) * tq
    span = WIN + tq
    starts = [pl.multiple_of(jnp.maximum(t0 - WIN + c * tq, 0), LANES) for c in range(span // tq)]
    k2 = jnp.concatenate([
        jnp.concatenate([k_ref[pl.ds(st, tq), :], kx_ref[pl.ds(st, tq), :]], axis=1)
        for st in starts], axis=0)
    v2 = jnp.concatenate([
        jnp.concatenate([v_ref[pl.ds(st, tq), :] for st in starts], axis=0),
        jnp.ones((span, LANES), BF16)], axis=1)
    tpos = t0 + lax.broadcasted_iota(jnp.int32, (tq, span), 0)
    kpos = (t0 - WIN) + lax.broadcasted_iota(jnp.int32, (tq, span), 1)
    dist = tpos - kpos
    bias = jnp.where((kpos >= 0) & (dist >= 0) & (dist < WIN), 0.0, NEG_INF)
    lane = lax.broadcasted_iota(jnp.int32, (tq, LANES), 1)
    tpos_lane = t0 + lax.broadcasted_iota(jnp.int32, (tq, LANES), 0)
    q2 = jnp.concatenate([
        jnp.concatenate([q_ref[:, r * HEAD_DIM:(r + 1) * HEAD_DIM],
                         _alibi_lanes(_group_slope(g, r), tpos_lane, lane).astype(BF16)], axis=1)
        for r in range(NSA_GROUP)], axis=0)
    s = _nt_dot(q2, k2)
    parts = []
    for r in range(NSA_GROUP):
        sr = s[r * tq:(r + 1) * tq] + bias
        parts.append(jnp.exp(sr - jnp.max(sr, axis=-1, keepdims=True)).astype(BF16))
    ov = _dot(jnp.concatenate(parts, axis=0), v2)
    o_win = ov[:, :HEAD_DIM] / ov[:, HEAD_DIM:]
    gates = jax.nn.sigmoid(gl_ref[...].astype(F32))
    for r in range(NSA_GROUP):
        cols = slice(r * HEAD_DIM, (r + 1) * HEAD_DIM)
        o = (gates[:, 3 * r:3 * r + 1] * oc_ref[:, cols].astype(F32)
             + gates[:, 3 * r + 1:3 * r + 2] * os_ref[:, cols].astype(F32)
             + gates[:, 3 * r + 2:3 * r + 3] * o_win[r * tq:(r + 1) * tq])
        o_ref[:, cols] = o.astype(o_ref.dtype)


def win_attn(proj, kx, o_cmp, o_slc, batch, seq, tq=256):
    n = proj.shape[0]
    bg = batch * NSA_KV_HEADS
    qt = seq // tq
    rowblk = lambda i, t: ((i // NSA_KV_HEADS) * qt + t, i % NSA_KV_HEADS)
    kern = functools.partial(_win_attn_kernel, tq=tq)
    return pl.pallas_call(
        kern,
        grid=(bg, qt),
        in_specs=[
            pl.BlockSpec((tq, NSA_GROUP * HEAD_DIM), rowblk),
            pl.BlockSpec((seq, HEAD_DIM), lambda i, t: (i // NSA_KV_HEADS, COL_KWIN + i % NSA_KV_HEADS)),
            pl.BlockSpec((seq, HEAD_DIM), lambda i, t: (i // NSA_KV_HEADS, COL_VWIN + i % NSA_KV_HEADS)),
            pl.BlockSpec((seq, LANES), lambda i, t: (0, 0)),
            pl.BlockSpec((tq, LANES), lambda i, t: ((i // NSA_KV_HEADS) * qt + t, COL_GATE + i % NSA_KV_HEADS)),
            pl.BlockSpec((tq, NSA_GROUP * HEAD_DIM), rowblk),
            pl.BlockSpec((tq, NSA_GROUP * HEAD_DIM), rowblk),
        ],
        out_specs=pl.BlockSpec((tq, NSA_GROUP * HEAD_DIM), rowblk),
        out_shape=jax.ShapeDtypeStruct((n, NSA_WIDTH), BF16),
        compiler_params=_cparams("parallel", "parallel"),
        name="win_attn",
    )(proj, proj, proj, kx, proj, o_cmp, o_slc)


def _gmlp_kernel(u_ref, v_ref, lg_ref, lb_ref, ws_ref, bsx_ref, o_ref, *, tm):
    u = jax.nn.gelu(u_ref[...].astype(F32))
    v = jax.nn.gelu(v_ref[...].astype(F32))
    mu = jnp.mean(v, axis=-1, keepdims=True)
    vc = v - mu
    var = jnp.mean(vc * vc, axis=-1, keepdims=True)
    vn = (vc * lax.rsqrt(var + EPS) * lg_ref[...] + lb_ref[...]).astype(BF16)
    c = GMLP_CHUNK
    tri = (lax.broadcasted_iota(jnp.int32, (c, c), 0) >= lax.broadcasted_iota(jnp.int32, (c, c), 1))
    bsx = bsx_ref[...]
    for gi in range(GMLP_GROUPS):
        w = jnp.where(tri, ws_ref[gi], 0.0).astype(BF16)
        cols = slice(gi * HEAD_DIM, (gi + 1) * HEAD_DIM)
        for ci in range(tm // c):
            rows = slice(ci * c, (ci + 1) * c)
            s = _dot(w, vn[rows, cols]) + bsx[:, cols]
            o_ref[rows, cols] = (u[rows, cols] * s).astype(o_ref.dtype)


def gmlp(proj, ln_g, ln_b, ws, bsx, tm=512):
    n = proj.shape[0]
    ublk = GMLP_WIDTH // LANES
    kern = functools.partial(_gmlp_kernel, tm=tm)
    return pl.pallas_call(
        kern,
        grid=(n // tm,),
        in_specs=[
            pl.BlockSpec((tm, GMLP_WIDTH), lambda i: (i, COL_U // ublk)),
            pl.BlockSpec((tm, GMLP_WIDTH), lambda i: (i, COL_V // ublk)),
            pl.BlockSpec((1, GMLP_WIDTH), lambda i: (0, 0)),
            pl.BlockSpec((1, GMLP_WIDTH), lambda i: (0, 0)),
            pl.BlockSpec((GMLP_GROUPS, GMLP_CHUNK, GMLP_CHUNK), lambda i: (0, 0, 0)),
            pl.BlockSpec((GMLP_CHUNK, GMLP_WIDTH), lambda i: (0, 0)),
        ],
        out_specs=pl.BlockSpec((tm, GMLP_WIDTH), lambda i: (i, 0)),
        out_shape=jax.ShapeDtypeStruct((n, GMLP_WIDTH), BF16),
        compiler_params=_cparams("parallel"),
        name="gmlp",
    )(proj, proj, ln_g, ln_b, ws, bsx)


def _retention_kernel(q_ref, k_ref, v_ref, g_ref, dec_ref, zeta_ref, xi_ref, dch_ref, gn_ref,
                      o_ref, s_ref, st_ref, *, seq):
    c = RET_CHUNK
    nc = seq // c
    heads = [slice(hh * HEAD_DIM, (hh + 1) * HEAD_DIM) for hh in range(RET_HEADS)]

    def kv_body(ci, carry):
        rows = pl.ds(pl.multiple_of(ci * c, c), c)
        for hh, cols in enumerate(heads):
            kz = (k_ref[rows, cols].astype(F32) * zeta_ref[hh]).astype(BF16)
            st_ref[ci, hh] = lax.dot_general(kz, v_ref[rows, cols], (((0,), (0,)), ((), ())),
                                             preferred_element_type=F32)
        return carry

    lax.fori_loop(0, nc, kv_body, 0, unroll=2)

    s_ref[...] = jnp.zeros(s_ref.shape, F32)

    def scan_body(ci, carry):
        for hh in range(RET_HEADS):
            state = s_ref[hh]
            s_ref[hh] = dch_ref[hh] * state + st_ref[ci, hh]
            st_ref[ci, hh] = state
        return carry

    lax.fori_loop(0, nc, scan_body, 0)

    def out_body(ci, carry):
        rows = pl.ds(pl.multiple_of(ci * c, c), c)
        for hh, cols in enumerate(heads):
            q = q_ref[rows, cols]
            v = v_ref[rows, cols]
            scores = _nt_dot(q, k_ref[rows, cols]) * dec_ref[hh]
            y = _dot(scores.astype(BF16), v) + _dot(q, st_ref[ci, hh].astype(BF16)) * xi_ref[hh]
            mu = jnp.mean(y, axis=-1, keepdims=True)
            yc = y - mu
            var = jnp.mean(yc * yc, axis=-1, keepdims=True)
            yn = yc * lax.rsqrt(var + EPS) * gn_ref[:, cols]
            o_ref[rows, cols] = (jax.nn.silu(g_ref[rows, cols].astype(F32)) * yn).astype(o_ref.dtype)
        return carry

    lax.fori_loop(0, nc, out_body, 0, unroll=2)


def retention(proj, tabs, gn_g, batch, seq):
    n = proj.shape[0]
    dec, zeta, xi, dch = tabs
    rblk = RET_WIDTH // LANES
    kern = functools.partial(_retention_kernel, seq=seq)
    full = lambda shape: pl.BlockSpec(shape, lambda b: (0,) * len(shape))
    return pl.pallas_call(
        kern,
        grid=(batch,),
        in_specs=[
            pl.BlockSpec((seq, RET_WIDTH), lambda b: (b, COL_RQ // rblk)),
            pl.BlockSpec((seq, RET_WIDTH), lambda b: (b, COL_RK // rblk)),
            pl.BlockSpec((seq, RET_WIDTH), lambda b: (b, COL_RV // rblk)),
            pl.BlockSpec((seq, RET_WIDTH), lambda b: (b, COL_RG // rblk)),
            full(dec.shape), full(zeta.shape), full(xi.shape), full(dch.shape),
            full((1, RET_WIDTH)),
        ],
        out_specs=pl.BlockSpec((seq, RET_WIDTH), lambda b: (b, 0)),
        out_shape=jax.ShapeDtypeStruct((n, RET_WIDTH), BF16),
        scratch_shapes=[pltpu.VMEM((RET_HEADS, HEAD_DIM, HEAD_DIM), F32),
                        pltpu.VMEM((seq // RET_CHUNK, RET_HEADS, HEAD_DIM, HEAD_DIM), F32)],
        compiler_params=_cparams("parallel"),
        name="retention",
    )(proj, proj, proj, proj, dec, zeta, xi, dch, gn_g)


ROUTE_E1, ROUTE_E2, ROUTE_G1, ROUTE_G2 = 0, 1, 2, 3


def _top2_route(logits):
    lane = lax.broadcasted_iota(jnp.int32, logits.shape, 1)
    lg = jnp.where(lane < N_EXPERTS, logits, -jnp.inf)
    v1 = jnp.max(lg, axis=-1, keepdims=True)
    i1 = jnp.min(jnp.where(lg == v1, lane, LANES), axis=-1, keepdims=True)
    lg2 = jnp.where(lane == i1, -jnp.inf, lg)
    v2 = jnp.max(lg2, axis=-1, keepdims=True)
    i2 = jnp.min(jnp.where(lg2 == v2, lane, LANES), axis=-1, keepdims=True)
    e2 = jnp.exp(v2 - v1)
    den = 1.0 + e2
    out = jnp.where(lane == ROUTE_E1, i1.astype(F32), 0.0)
    out = jnp.where(lane == ROUTE_E2, i2.astype(F32), out)
    out = jnp.where(lane == ROUTE_G1, 1.0 / den, out)
    return jnp.where(lane == ROUTE_G2, e2 / den, out)


def _out_proj_kernel(*refs, router):
    if router:
        nsa_ref, gm_ref, ret_ref, h_ref, w_ref, g2_ref, wrh_ref, wrl_ref, br_ref, ho_ref, f_ref, rt_ref = refs
    else:
        nsa_ref, gm_ref, ret_ref, h_ref, w_ref, g2_ref, ho_ref, f_ref = refs
    acc = _dot(nsa_ref[...], w_ref[:NSA_WIDTH, :])
    acc = acc + _dot(gm_ref[...], w_ref[NSA_WIDTH:NSA_WIDTH + GMLP_WIDTH, :])
    acc = acc + _dot(ret_ref[...], w_ref[NSA_WIDTH + GMLP_WIDTH:, :])
    hn = h_ref[...] + acc
    ho_ref[...] = hn
    ms = jnp.mean(hn * hn, axis=-1, keepdims=True)
    f = hn * lax.rsqrt(ms + EPS) * g2_ref[...]
    f_hi = f.astype(BF16)
    if router:
        f_ref[...] = f
        f_lo = (f - f_hi.astype(F32)).astype(BF16)
        logits = (_dot(f_hi, wrh_ref[...]) + _dot(f_lo, wrh_ref[...]) + _dot(f_hi, wrl_ref[...])
                  + br_ref[...])
        rt_ref[...] = _top2_route(logits)
    else:
        f_ref[...] = f_hi


def out_proj(o_nsa, o_gm, o_ret, h2, w, g2, router_w=None, tm=512):
    n, d = h2.shape
    tm = min(tm, n)
    router = router_w is not None
    row = lambda width: pl.BlockSpec((tm, width), lambda i: (i, 0))
    full = lambda shape: pl.BlockSpec(shape, lambda i: (0,) * len(shape))
    in_specs = [row(NSA_WIDTH), row(GMLP_WIDTH), row(RET_WIDTH), row(d), full(w.shape), full((1, d))]
    args = [o_nsa, o_gm, o_ret, h2, w, g2]
    if router:
        in_specs += [full((d, LANES)), full((d, LANES)), full((1, LANES))]
        args += list(router_w)
        out_specs = [row(d), row(d), row(LANES)]
        out_shape = [jax.ShapeDtypeStruct((n, d), F32), jax.ShapeDtypeStruct((n, d), F32),
                     jax.ShapeDtypeStruct((n, LANES), F32)]
    else:
        out_specs = [row(d), row(d)]
        out_shape = [jax.ShapeDtypeStruct((n, d), F32), jax.ShapeDtypeStruct((n, d), BF16)]
    return pl.pallas_call(
        functools.partial(_out_proj_kernel, router=router),
        grid=(n // tm,),
        in_specs=in_specs,
        out_specs=out_specs,
        out_shape=out_shape,
        compiler_params=_cparams("parallel"),
        name="out_proj_router" if router else "out_proj",
    )(*args)


def _ffn_kernel(*refs, final):
    if final:
        x_ref, h_ref, w1_ref, w3_ref, w2_ref, fg_ref, o_ref = refs
    else:
        x_ref, h_ref, w1_ref, w3_ref, w2_ref, o_ref = refs
    k = pl.program_id(1)

    @pl.when(k == 0)
    def _():
        o_ref[...] = h_ref[...]

    x = x_ref[...]
    hid = jax.nn.silu(_dot(x, w1_ref[...])) * _dot(x, w3_ref[...])
    o_ref[...] += _dot(hid.astype(BF16), w2_ref[...])

    if final:
        @pl.when(k == pl.num_programs(1) - 1)
        def _():
            y = o_ref[...]
            ms = jnp.mean(y * y, axis=-1, keepdims=True)
            o_ref[...] = y * lax.rsqrt(ms + EPS) * fg_ref[...]


def ffn(f, h2, w1, w3, w2, final_g=None, tm=512, tf=512):
    n, d = h2.shape
    ff = w1.shape[1]
    tm = min(tm, n)
    tf = min(tf, ff)
    in_specs = [
        pl.BlockSpec((tm, d), lambda i, k: (i, 0)),
        pl.BlockSpec((tm, d), lambda i, k: (i, 0)),
        pl.BlockSpec((d, tf), lambda i, k: (0, k)),
        pl.BlockSpec((d, tf), lambda i, k: (0, k)),
        pl.BlockSpec((tf, d), lambda i, k: (k, 0)),
    ]
    args = [f, h2, w1, w3, w2]
    if final_g is not None:
        in_specs.append(pl.BlockSpec((1, d), lambda i, k: (0, 0)))
        args.append(final_g)
    return pl.pallas_call(
        functools.partial(_ffn_kernel, final=final_g is not None),
        grid=(n // tm, ff // tf),
        in_specs=in_specs,
        out_specs=pl.BlockSpec((tm, d), lambda i, k: (i, 0)),
        out_shape=jax.ShapeDtypeStruct((n, d), F32),
        compiler_params=_cparams("parallel", "arbitrary"),
        name="ffn",
    )(*args)


MOE_TM = 512


def _row_copy(src_ref, src_row, dst_ref, dst_row, sem):
    return pltpu.make_async_copy(src_ref.at[pl.ds(src_row, 1)], dst_ref.at[pl.ds(dst_row, 1)], sem)


def _moe_ffn_kernel(te_ref, nu_ref, src_ref, f_ref, w1_ref, w3_ref, w2_ref, o_ref, xin_ref, xb_ref, sem,
                    *, tm, chunk):
    del te_ref
    i = pl.program_id(0)
    k = pl.program_id(1)
    nt = pl.num_programs(0)
    nk = pl.num_programs(1)
    n_used = nu_ref[0]
    used = i < n_used
    slot = i % 2

    def start_chunk(tile, c, dst_slot):
        for u in range(chunk):
            row = c * chunk + u
            _row_copy(f_ref, src_ref[tile * tm + row], xin_ref.at[dst_slot], row, sem.at[dst_slot]).start()

    def wait_tile(dst_slot):
        def drain(c, carry):
            for u in range(chunk):
                _row_copy(f_ref, 0, xin_ref.at[dst_slot], 0, sem.at[dst_slot]).wait()
            return carry
        lax.fori_loop(0, nk, drain, 0)

    @pl.when((i == 0) & (k == 0))
    def _():
        def first(c, carry):
            start_chunk(0, c, 0)
            return carry
        lax.fori_loop(0, nk, first, 0)

    @pl.when((k == 0) & (i <= n_used))
    def _():
        wait_tile(slot)
        xb_ref[...] = xin_ref[slot, :tm, :].astype(BF16)

    @pl.when(k == 0)
    def _():
        o_ref[...] = jnp.zeros(o_ref.shape, F32)

    @pl.when(used)
    def _():
        start_chunk(i + 1, k, 1 - slot)
        x = xb_ref[...]
        hid = jax.nn.silu(_dot(x, w1_ref[0])) * _dot(x, w3_ref[0])
        o_ref[...] += _dot(hid.astype(BF16), w2_ref[0])

    @pl.when(used & (i == nt - 1) & (k == nk - 1))
    def _():
        wait_tile(1 - slot)


def moe_ffn(tile_expert, n_used, src, f, w1, w3, w2, rows, tm=MOE_TM, tf=256):
    n, d = f.shape
    ff = w1.shape[2]
    tf = min(tf, ff)
    nk = ff // tf
    chunk = pl.cdiv(tm, nk)
    buf_rows = ((chunk * nk + 7) // 8) * 8
    kk = lambda i, k, nu: jnp.where(i < nu[0], k, nk - 1)
    return pl.pallas_call(
        functools.partial(_moe_ffn_kernel, tm=tm, chunk=chunk),
        grid_spec=pltpu.PrefetchScalarGridSpec(
            num_scalar_prefetch=3,
            grid=(rows // tm, nk),
            in_specs=[
                pl.BlockSpec(memory_space=pl.ANY),
                pl.BlockSpec((1, d, tf), lambda i, k, te, nu, src: (te[i], 0, kk(i, k, nu))),
                pl.BlockSpec((1, d, tf), lambda i, k, te, nu, src: (te[i], 0, kk(i, k, nu))),
                pl.BlockSpec((1, tf, d), lambda i, k, te, nu, src: (te[i], kk(i, k, nu), 0)),
            ],
            out_specs=pl.BlockSpec((tm, d), lambda i, k, te, nu, src: (i, 0)),
            scratch_shapes=[pltpu.VMEM((2, buf_rows, d), F32), pltpu.VMEM((tm, d), BF16),
                            pltpu.SemaphoreType.DMA((2,))],
        ),
        out_shape=jax.ShapeDtypeStruct((rows, d), F32),
        compiler_params=_cparams("arbitrary", "arbitrary"),
        name="moe_ffn",
    )(tile_expert, n_used, src, f, w1, w3, w2)


def _combine_kernel(*refs, tm, final):
    if final:
        dest_ref, h_ref, rt_ref, yg_ref, fg_ref, o_ref, buf_ref, sem = refs
    else:
        dest_ref, h_ref, rt_ref, yg_ref, o_ref, buf_ref, sem = refs
    i = pl.program_id(0)
    nt = pl.num_programs(0)
    slot = i % 2

    def start_row(tile, j, dst_slot):
        for s in range(TOP_K):
            _row_copy(yg_ref, dest_ref[(tile * tm + j) * TOP_K + s], buf_ref.at[dst_slot, s], j,
                      sem.at[dst_slot]).start()

    def wait_tile(dst_slot):
        def drain(j, carry):
            for s in range(TOP_K):
                _row_copy(yg_ref, 0, buf_ref.at[dst_slot, s], 0, sem.at[dst_slot]).wait()
            return carry
        lax.fori_loop(0, tm, drain, 0, unroll=8)

    @pl.when(i == 0)
    def _():
        def first(j, carry):
            start_row(0, j, 0)
            return carry
        lax.fori_loop(0, tm, first, 0, unroll=8)

    wait_tile(slot)
    for j in range(tm):
        start_row(i + 1, j, 1 - slot)

    rt = rt_ref[...]
    y = (h_ref[...] + rt[:, ROUTE_G1:ROUTE_G1 + 1] * buf_ref[slot, 0]
         + rt[:, ROUTE_G2:ROUTE_G2 + 1] * buf_ref[slot, 1])
    if final:
        ms = jnp.mean(y * y, axis=-1, keepdims=True)
        y = y * lax.rsqrt(ms + EPS) * fg_ref[...]
    o_ref[...] = y

    @pl.when(i == nt - 1)
    def _():
        wait_tile(1 - slot)


def moe_combine(dest, h2, route, yg, final_g=None, tm=128):
    n, d = h2.shape
    tm = min(tm, n)
    dest = jnp.pad(dest, (0, tm * TOP_K))
    in_specs = [
        pl.BlockSpec((tm, d), lambda i, dest: (i, 0)),
        pl.BlockSpec((tm, LANES), lambda i, dest: (i, 0)),
        pl.BlockSpec(memory_space=pl.ANY),
    ]
    args = [dest, h2, route, yg]
    if final_g is not None:
        in_specs.append(pl.BlockSpec((1, d), lambda i, dest: (0, 0)))
        args.append(final_g)
    return pl.pallas_call(
        functools.partial(_combine_kernel, tm=tm, final=final_g is not None),
        grid_spec=pltpu.PrefetchScalarGridSpec(
            num_scalar_prefetch=1,
            grid=(n // tm,),
            in_specs=in_specs,
            out_specs=pl.BlockSpec((tm, d), lambda i, dest: (i, 0)),
            scratch_shapes=[pltpu.VMEM((2, TOP_K, tm, d), F32), pltpu.SemaphoreType.DMA((2,))],
        ),
        out_shape=jax.ShapeDtypeStruct((n, d), F32),
        compiler_params=_cparams("arbitrary"),
        name="moe_combine",
    )(*args)


def _routing_tables(route, tm):
    n = route.shape[0]
    flat_e = route[:, ROUTE_E1:ROUTE_E2 + 1].astype(jnp.int32).reshape(-1)
    onehot = (flat_e[:, None] == jnp.arange(N_EXPERTS, dtype=jnp.int32)[None, :]).astype(jnp.int32)
    csum = jnp.cumsum(onehot, axis=0)
    rank = jnp.sum(csum * onehot, axis=1) - 1
    counts = csum[-1]
    padded = ((counts + tm - 1) // tm) * tm
    off_end = jnp.cumsum(padded)
    dest = (off_end - padded)[flat_e] + rank
    n_tiles = (n * TOP_K) // tm + N_EXPERTS
    n_used = (off_end[-1] // tm).astype(jnp.int32)
    tile_start = jnp.arange(n_tiles, dtype=jnp.int32) * tm
    tile_start = jnp.minimum(tile_start, off_end[-1] - tm)
    tile_expert = jnp.sum((tile_start[:, None] >= off_end[None, :]).astype(jnp.int32), axis=1)
    rows = n_tiles * tm
    src = jnp.zeros((rows + tm + 8,), jnp.int32).at[dest].set(jnp.arange(n * TOP_K, dtype=jnp.int32) // TOP_K)
    return dest.astype(jnp.int32), src, tile_expert.astype(jnp.int32), n_used.reshape(1), rows


def moe_layer(h2, f, route, w1, w3, w2, final_g=None):
    dest, src, tile_expert, n_used, rows = _routing_tables(route, MOE_TM)
    yg = moe_ffn(tile_expert, n_used, src, f, w1, w3, w2, rows)
    return moe_combine(dest, h2, route, yg, final_g=final_g)


def _reorder_w_in(w_in):
    d = w_in.shape[0]
    kvw = NSA_KV_WIDTH
    o_gate = NSA_WIDTH + 6 * kvw
    o_uv = o_gate + 3 * NSA_HEADS
    o_ret = o_uv + 2 * GMLP_WIDTH
    gate = w_in[:, o_gate:o_uv].reshape(d, NSA_KV_HEADS, 3 * NSA_GROUP)
    gate = jnp.pad(gate, ((0, 0), (0, 0), (0, LANES - 3 * NSA_GROUP))).reshape(d, NSA_KV_HEADS * LANES)
    parts = [w_in[:, :o_gate], w_in[:, o_uv:], gate]
    w = jnp.concatenate(parts, axis=1)
    w = jnp.pad(w, ((0, 0), (0, PROJ_COLS - w.shape[1])))
    scale = np.ones((1, PROJ_COLS), np.float32)
    scale[:, COL_Q * LANES:(COL_Q + NSA_HEADS) * LANES] = HEAD_DIM ** -0.5
    scale[:, COL_RK * LANES:(COL_RK + RET_HEADS) * LANES] = HEAD_DIM ** -0.5
    del o_ret
    return w.astype(BF16), jnp.asarray(scale)


def _key_extra_lanes(seq):
    pos = np.arange(seq)
    kx = np.zeros((seq, LANES), np.float32)
    kx[pos, pos // SLC_LEN] = 1.0
    kx[:, XL_HI] = pos // SLC_LEN
    kx[:, XL_LO] = pos % SLC_LEN
    kx[:, XL_ONE_A] = 1.0
    kx[:, XL_ONE_B] = 1.0
    return jnp.asarray(kx, BF16)


def _share_t(seq):
    nrow = seq // CMP_STRIDE
    nb = seq // SLC_LEN
    c0 = np.arange(nrow)[None, :] * CMP_STRIDE
    s0 = np.arange(nb)[:, None] * SLC_LEN
    overlap = np.minimum(c0 + CMP_LEN, s0 + SLC_LEN) - np.maximum(c0, s0)
    share = np.clip(overlap, 0, CMP_LEN).astype(np.float32) / CMP_LEN
    share[:, nrow - 1] = 0.0
    return jnp.asarray(share, BF16)


def _retention_tables():
    hh = np.arange(RET_HEADS, dtype=np.float64)
    lg = np.log1p(-np.exp2(-5.0 - hh))
    nn = np.arange(RET_CHUNK, dtype=np.float64)
    rel = nn[:, None] - nn[None, :]
    dec = np.where(rel >= 0, np.exp(lg[:, None, None] * np.maximum(rel, 0.0)), 0.0)
    zeta = np.exp(lg[:, None] * (RET_CHUNK - 1.0 - nn))[:, :, None]
    xi = np.exp(lg[:, None] * (nn + 1.0))[:, :, None]
    dch = np.broadcast_to(np.exp(lg * RET_CHUNK)[:, None, None], (RET_HEADS, 1, HEAD_DIM))
    return tuple(jnp.asarray(a, F32) for a in (dec, zeta, xi, dch))


def _token_mixer(h2, batch, seq, ln1_g, w_in, pe_k, w1_k, w2_k, pe_v, w1_v, w2_v,
                 g_ln_g, g_ln_b, g_ws, g_bs, ret_gn_g, consts):
    kx, sht, ret_tabs = consts
    w_r, colscale = _reorder_w_in(w_in)
    proj = in_proj(h2, ln1_g[None, :], w_r, colscale)
    nrow = seq // CMP_STRIDE
    kv = proj[:, COL_KCMP * LANES:(COL_VCMP + NSA_KV_HEADS) * LANES]
    kv = kv.reshape(batch, nrow, CMP_STRIDE, 2, NSA_KV_HEADS, HEAD_DIM)
    rows = kv.transpose(3, 0, 4, 1, 2, 5).reshape(2, batch * NSA_KV_HEADS, nrow, CMP_STRIDE * HEAD_DIM)
    pe = jnp.stack([pe_k, pe_v]).reshape(2, 1, CMP_LEN * HEAD_DIM)
    kvc = compress(rows, pe, jnp.stack([w1_k, w1_v]).astype(BF16), jnp.stack([w2_k, w2_v]).astype(BF16))
    o_cmp, xsel = cmp_attn(proj, kvc, sht, batch, seq)
    o_slc = sel_attn(proj, xsel, kx, batch, seq)
    o_nsa = win_attn(proj, kx, o_cmp, o_slc, batch, seq)
    bsx = jnp.repeat(g_bs.T, HEAD_DIM, axis=1)
    o_gm = gmlp(proj, g_ln_g[None, :], g_ln_b[None, :], g_ws, bsx)
    o_ret = retention(proj, ret_tabs, ret_gn_g[None, :], batch, seq)
    return o_nsa, o_gm, o_ret


def kernel(x, ln1_g, w_in, cmp_pe_k, cmp_w1_k, cmp_w2_k, cmp_pe_v, cmp_w1_v, cmp_w2_v, gmlp_ln_g, gmlp_ln_b, gmlp_ws, gmlp_bs, ret_gn_g, w_out, ln2_g, ffn_w1, ffn_w3, ffn_w2, moe_wr, moe_br, moe_w1, moe_w3, moe_w2, final_g):
    batch, seq, d = x.shape
    depth = w_in.shape[0]
    consts = (_key_extra_lanes(seq), _share_t(seq), _retention_tables())
    h2 = x.reshape(batch * seq, d)
    for layer in range(depth):
        o_nsa, o_gm, o_ret = _token_mixer(
            h2, batch, seq, ln1_g[layer], w_in[layer], cmp_pe_k[layer], cmp_w1_k[layer], cmp_w2_k[layer],
            cmp_pe_v[layer], cmp_w1_v[layer], cmp_w2_v[layer], gmlp_ln_g[layer], gmlp_ln_b[layer],
            gmlp_ws[layer], gmlp_bs[layer], ret_gn_g[layer], consts)
        fin = final_g[None, :] if layer == depth - 1 else None
        i = layer // 2
        w_o = w_out[layer].astype(BF16)
        if layer % 2 == 0:
            h2, f = out_proj(o_nsa, o_gm, o_ret, h2, w_o, ln2_g[layer][None, :])
            h2 = ffn(f, h2, ffn_w1[i].astype(BF16), ffn_w3[i].astype(BF16), ffn_w2[i].astype(BF16),
                     final_g=fin)
        else:
            wr = jnp.pad(moe_wr[i], ((0, 0), (0, LANES - N_EXPERTS)))
            wr_hi = wr.astype(BF16)
            wr_lo = (wr - wr_hi.astype(F32)).astype(BF16)
            br = jnp.pad(moe_br[i], (0, LANES - N_EXPERTS))[None, :]
            h2, f, route = out_proj(o_nsa, o_gm, o_ret, h2, w_o, ln2_g[layer][None, :],
                                    router_w=(wr_hi, wr_lo, br))
            h2 = moe_layer(h2, f, route, moe_w1[i].astype(BF16), moe_w3[i].astype(BF16),
                           moe_w2[i].astype(BF16), final_g=fin)
    return h2.reshape(batch, seq, d)
```

```python
import functools

import numpy as np
import jax
import jax.numpy as jnp
from jax import lax
from jax.experimental import pallas as pl
from jax.experimental.pallas import tpu as pltpu

F32 = jnp.float32
BF16 = jnp.bfloat16

HEAD_DIM = 128
NSA_HEADS = 8
NSA_KV_HEADS = 2
NSA_GROUP = NSA_HEADS // NSA_KV_HEADS
NSA_WIDTH = NSA_HEADS * HEAD_DIM
NSA_KV_WIDTH = NSA_KV_HEADS * HEAD_DIM
CMP_LEN = 32
CMP_STRIDE = 16
SLC_LEN = 64
SLC_TOPK = 16
WIN = 512
GMLP_GROUPS = 4
GMLP_CHUNK = 128
GMLP_WIDTH = GMLP_GROUPS * HEAD_DIM
RET_HEADS = 4
RET_CHUNK = 128
RET_WIDTH = RET_HEADS * HEAD_DIM
N_EXPERTS = 8
TOP_K = 2
EPS = 1e-6
NEG_INF = -1e30
FORCE_SCORE = 1e4

LANES = 128
VMEM_LIMIT = 56 * 1024 * 1024

COL_Q = 0
COL_KCMP = 8
COL_VCMP = 10
COL_KSLC = 12
COL_VSLC = 14
COL_KWIN = 16
COL_VWIN = 18
COL_U = 20
COL_V = 24
COL_RQ = 28
COL_RK = 32
COL_RV = 36
COL_RG = 40
COL_GATE = 44
PROJ_COLS = 48 * LANES

XL_HI = 64
XL_LO = 65
XL_ONE_A = 66
XL_ONE_B = 67


def _cparams(*sem):
    return pltpu.CompilerParams(dimension_semantics=sem, vmem_limit_bytes=VMEM_LIMIT)


def _nt_dot(a, b):
    return lax.dot_general(a, b, (((1,), (1,)), ((), ())), preferred_element_type=F32)


def _dot(a, b):
    return jnp.dot(a, b, preferred_element_type=F32)


def _in_proj_kernel(x_ref, g_ref, w_ref, cs_ref, o_ref, xn_ref):
    @pl.when(pl.program_id(1) == 0)
    def _():
        x = x_ref[...]
        ms = jnp.mean(x * x, axis=-1, keepdims=True)
        xn_ref[...] = (x * lax.rsqrt(ms + EPS) * g_ref[...]).astype(BF16)

    acc = _dot(xn_ref[...], w_ref[...])
    o_ref[...] = (acc * cs_ref[...]).astype(o_ref.dtype)


def in_proj(h2, g, w, colscale, tm=1024, tn=1024):
    n, d = h2.shape
    cols = w.shape[1]
    tm = min(tm, n)
    return pl.pallas_call(
        _in_proj_kernel,
        grid=(n // tm, cols // tn),
        in_specs=[
            pl.BlockSpec((tm, d), lambda i, j: (i, 0)),
            pl.BlockSpec((1, d), lambda i, j: (0, 0)),
            pl.BlockSpec((d, tn), lambda i, j: (0, j)),
            pl.BlockSpec((1, tn), lambda i, j: (0, j)),
        ],
        out_specs=pl.BlockSpec((tm, tn), lambda i, j: (i, j)),
        out_shape=jax.ShapeDtypeStruct((n, cols), BF16),
        scratch_shapes=[pltpu.VMEM((tm, d), BF16)],
        compiler_params=_cparams("parallel", "arbitrary"),
        name="in_proj",
    )(h2, g, w, colscale)


def _compress_kernel(r_ref, pe_ref, w1_ref, w2_ref, o_ref):
    half = CMP_STRIDE * HEAD_DIM
    r = r_ref[0, 0].astype(F32)
    nrow = r.shape[0]
    pe = pe_ref[0]
    a = _dot((r + pe[:, :half]).astype(BF16), w1_ref[0, :half, :])
    b = _dot((r + pe[:, half:]).astype(BF16), w1_ref[0, half:, :])
    pre = a + pltpu.roll(b, nrow - 1, 0)
    y = _dot(jax.nn.gelu(pre).astype(BF16), w2_ref[0])
    row = lax.broadcasted_iota(jnp.int32, y.shape, 0)
    o_ref[0, 0] = jnp.where(row < nrow - 1, y, 0.0).astype(o_ref.dtype)


def compress(rows, pe, w1, w2):
    _, bg, nrow, width = rows.shape
    return pl.pallas_call(
        _compress_kernel,
        grid=(2, bg),
        in_specs=[
            pl.BlockSpec((1, 1, nrow, width), lambda s, i: (s, i, 0, 0)),
            pl.BlockSpec((1, 1, 2 * width), lambda s, i: (s, 0, 0)),
            pl.BlockSpec((1, 2 * width, HEAD_DIM), lambda s, i: (s, 0, 0)),
            pl.BlockSpec((1, HEAD_DIM, HEAD_DIM), lambda s, i: (s, 0, 0)),
        ],
        out_specs=pl.BlockSpec((1, 1, nrow, HEAD_DIM), lambda s, i: (s, i, 0, 0)),
        out_shape=jax.ShapeDtypeStruct((2, bg, nrow, HEAD_DIM), BF16),
        compiler_params=_cparams("arbitrary", "arbitrary"),
        name="compress",
    )(rows, pe, w1, w2)


def _group_slope(g, r):
    return jnp.where(g == 0, F32(2.0 ** -(r + 1)), F32(2.0 ** -(r + 1 + NSA_GROUP)))


def _cmp_attn_kernel(q_ref, kc_ref, vc_ref, sht_ref, dist_ref, cb_ref, o_ref, x_ref, *, tq, nb, topk):
    g = pl.program_id(0) % NSA_KV_HEADS
    t0 = pl.program_id(1) * tq
    kc = kc_ref[0, 0]
    vc = vc_ref[0, 0]
    nrow = kc.shape[0]
    distf = dist_ref[...]
    cbias = cb_ref[...]
    row_ok = jnp.where(t0 + lax.broadcasted_iota(jnp.int32, (tq, 1), 0) >= CMP_LEN - 1, 1.0, 0.0)
    psum = jnp.zeros((tq, nrow), F32)
    for r in range(NSA_GROUP):
        q = q_ref[:, r * HEAD_DIM:(r + 1) * HEAD_DIM]
        s = _nt_dot(q, kc) - _group_slope(g, r) * distf + cbias
        e = jnp.exp(s - jnp.max(s, axis=-1, keepdims=True))
        p = e * (row_ok / jnp.sum(e, axis=-1, keepdims=True))
        o_ref[:, r * HEAD_DIM:(r + 1) * HEAD_DIM] = _dot(p.astype(BF16), vc).astype(o_ref.dtype)
        psum = psum + p
    p_hi = psum.astype(BF16)
    p_lo = (psum - p_hi.astype(F32)).astype(BF16)
    sht = sht_ref[...]
    imp = _nt_dot(sht, p_hi) + _nt_dot(sht, p_lo)
    j = lax.broadcasted_iota(jnp.int32, (nb, tq), 0)
    cur = (t0 + lax.broadcasted_iota(jnp.int32, (nb, tq), 1)) // SLC_LEN
    forced = (j == 0) | (j == cur) | (j == cur - 1)
    score = jnp.where(forced, FORCE_SCORE, imp)
    cand = jnp.where(j > cur, -3e38, score)
    jf = j.astype(F32)[:, :LANES]
    groups = [cand[:, c:c + LANES] for c in range(0, tq, LANES)]
    for _ in range(topk):
        for n, cg in enumerate(groups):
            best = jnp.max(cg, axis=0, keepdims=True)
            idx = jnp.min(jnp.where(cg == best, jf, float(nb)), axis=0, keepdims=True)
            groups[n] = jnp.where(jf == idx, -jnp.inf, cg)
    cand = jnp.concatenate(groups, axis=1)
    neg = jnp.where(cand == -jnp.inf, 0.0, NEG_INF)
    neg = jnp.concatenate([neg, jnp.zeros((LANES - nb, tq), F32)], axis=0)
    x_ref[...] = neg.T.astype(x_ref.dtype)


def _cmp_tables(seq):
    nrow = seq // CMP_STRIDE
    t = np.arange(seq)[:, None]
    c = np.arange(nrow)[None, :]
    dist = t - (c * CMP_STRIDE + (CMP_LEN - 1))
    valid = (dist >= 0) & (c < nrow - 1)
    return jnp.asarray(dist, F32), jnp.asarray(np.where(valid, 0.0, NEG_INF), F32)


def cmp_attn(proj, kvc, sht, tables, batch, seq, tq=256):
    n = proj.shape[0]
    nb = seq // SLC_LEN
    bg = batch * NSA_KV_HEADS
    nrow = kvc.shape[2]
    qt = seq // tq
    rowblk = lambda i, t: ((i // NSA_KV_HEADS) * qt + t, i % NSA_KV_HEADS)
    kern = functools.partial(_cmp_attn_kernel, tq=tq, nb=nb, topk=min(SLC_TOPK, nb))
    return pl.pallas_call(
        kern,
        grid=(bg, qt),
        in_specs=[
            pl.BlockSpec((tq, NSA_GROUP * HEAD_DIM), rowblk),
            pl.BlockSpec((1, 1, nrow, HEAD_DIM), lambda i, t: (0, i, 0, 0)),
            pl.BlockSpec((1, 1, nrow, HEAD_DIM), lambda i, t: (1, i, 0, 0)),
            pl.BlockSpec((nb, nrow), lambda i, t: (0, 0)),
            pl.BlockSpec((tq, nrow), lambda i, t: (t, 0)),
            pl.BlockSpec((tq, nrow), lambda i, t: (t, 0)),
        ],
        out_specs=[
            pl.BlockSpec((tq, NSA_GROUP * HEAD_DIM), rowblk),
            pl.BlockSpec((tq, LANES), rowblk),
        ],
        out_shape=[
            jax.ShapeDtypeStruct((n, NSA_WIDTH), BF16),
            jax.ShapeDtypeStruct((n, NSA_KV_HEADS * LANES), BF16),
        ],
        compiler_params=_cparams("parallel", "parallel"),
        name="cmp_attn",
    )(proj, kvc, kvc, sht, *tables)


def _alibi_q_table(seq):
    t = np.arange(seq)
    tab = np.zeros((NSA_HEADS, seq, LANES), np.float32)
    for h in range(NSA_HEADS):
        slope = 2.0 ** (-8.0 * (h + 1) / NSA_HEADS)
        tab[h, :, XL_HI] = slope * SLC_LEN
        tab[h, :, XL_LO] = slope
        tab[h, :, XL_ONE_A] = -(slope * SLC_LEN) * (t // SLC_LEN)
        tab[h, :, XL_ONE_B] = -slope * (t % SLC_LEN)
    return jnp.asarray(tab, BF16)


def _sel_attn_kernel(q_ref, x_ref, ax_ref, k_ref, v_ref, kx_ref, o_ref, q2_ref, sa_ref, sb_ref, m_ref, acc_ref,
                     *, tq, tk):
    t0 = pl.program_id(1) * tq
    n_kt = (t0 + tq - 1) // tk + 1
    rows = NSA_GROUP * tq
    half = rows // 2
    tpos = t0 + (lax.broadcasted_iota(jnp.int32, (half, tk), 0) & (tq - 1))
    kcol = lax.broadcasted_iota(jnp.int32, (half, tk), 1)

    xsel = x_ref[...]
    q2_ref[...] = jnp.concatenate([
        jnp.concatenate([q_ref[:, r * HEAD_DIM:(r + 1) * HEAD_DIM], xsel + ax_ref[r]], axis=1)
        for r in range(NSA_GROUP)], axis=0)
    m_ref[...] = jnp.full(m_ref.shape, -jnp.inf, F32)
    acc_ref[...] = jnp.zeros(acc_ref.shape, F32)
    ones = jnp.ones((tk, LANES), BF16)

    def scores(kt, dst_ref):
        ks = pl.multiple_of(kt * tk, tk)
        k2 = jnp.concatenate([k_ref[pl.ds(ks, tk), :], kx_ref[pl.ds(ks, tk), :]], axis=1)
        dst_ref[...] = _nt_dot(q2_ref[...], k2)

    def update(kt, src_ref, causal):
        ks = pl.multiple_of(kt * tk, tk)
        v2 = jnp.concatenate([v_ref[pl.ds(ks, tk), :], ones], axis=1)
        step = half if causal else rows
        for r0 in range(0, rows, step):
            rs = slice(r0, r0 + step)
            s = src_ref[rs, :]
            if causal:
                s = jnp.where(ks + kcol <= tpos, s, NEG_INF)
            m_prev = m_ref[rs, :]
            m_new = jnp.maximum(m_prev, jnp.max(s, axis=-1, keepdims=True))
            alpha = jnp.exp(m_prev - m_new)
            p = jnp.exp(s - jnp.concatenate([m_new] * (tk // LANES), axis=1))
            acc_ref[rs, :] = (jnp.concatenate([alpha, alpha], axis=1) * acc_ref[rs, :]
                              + _dot(p.astype(BF16), v2))
            m_ref[rs, :] = m_new

    n_full = n_kt - 1
    scores(0, sa_ref)

    def body(j, carry):
        scores(2 * j + 1, sb_ref)
        update(2 * j, sa_ref, False)
        scores(2 * j + 2, sa_ref)
        update(2 * j + 1, sb_ref, False)
        return carry

    lax.fori_loop(0, n_full // 2, body, 0)

    @pl.when(n_full % 2 == 1)
    def _():
        scores(n_full, sb_ref)
        update(n_full - 1, sa_ref, False)
        update(n_full, sb_ref, True)

    @pl.when(n_full % 2 == 0)
    def _():
        update(n_full, sa_ref, True)

    acc = acc_ref[...]
    o = acc[:, :HEAD_DIM] / acc[:, HEAD_DIM:]
    for r in range(NSA_GROUP):
        o_ref[:, r * HEAD_DIM:(r + 1) * HEAD_DIM] = o[r * tq:(r + 1) * tq].astype(o_ref.dtype)


def sel_attn(proj, xsel, kx, axq, batch, seq, tq=256, tk=512):
    n = proj.shape[0]
    tk = min(tk, seq)
    bg = batch * NSA_KV_HEADS
    qt = seq // tq
    rowblk = lambda i, t: ((i // NSA_KV_HEADS) * qt + t, i % NSA_KV_HEADS)
    kern = functools.partial(_sel_attn_kernel, tq=tq, tk=tk)
    return pl.pallas_call(
        kern,
        grid=(bg, qt),
        in_specs=[
            pl.BlockSpec((tq, NSA_GROUP * HEAD_DIM), rowblk),
            pl.BlockSpec((tq, LANES), rowblk),
            pl.BlockSpec((NSA_GROUP, tq, LANES), lambda i, t: (i % NSA_KV_HEADS, t, 0)),
            pl.BlockSpec((seq, HEAD_DIM), lambda i, t: (i // NSA_KV_HEADS, COL_KSLC + i % NSA_KV_HEADS)),
            pl.BlockSpec((seq, HEAD_DIM), lambda i, t: (i // NSA_KV_HEADS, COL_VSLC + i % NSA_KV_HEADS)),
            pl.BlockSpec((seq, LANES), lambda i, t: (0, 0)),
        ],
        out_specs=pl.BlockSpec((tq, NSA_GROUP * HEAD_DIM), rowblk),
        out_shape=jax.ShapeDtypeStruct((n, NSA_WIDTH), BF16),
        scratch_shapes=[
            pltpu.VMEM((NSA_GROUP * tq, 2 * HEAD_DIM), BF16),
            pltpu.VMEM((NSA_GROUP * tq, tk), F32),
            pltpu.VMEM((NSA_GROUP * tq, tk), F32),
            pltpu.VMEM((NSA_GROUP * tq, LANES), F32),
            pltpu.VMEM((NSA_GROUP * tq, 2 * HEAD_DIM), F32),
        ],
        compiler_params=_cparams("parallel", "parallel"),
        name="sel_attn",
    )(proj, xsel, axq, proj, proj, kx)


def _win_attn_kernel(q_ref, ax_ref, wb_ref, k_ref, v_ref, kx_ref, gl_ref, oc_ref, os_ref, o_ref, *, tq):
    t0 = pl.program_id(1) * tq
    span = WIN + tq
    starts = [pl.multiple_of(jnp.maximum(t0 - WIN + c * tq, 0), LANES) for c in range(span // tq)]
    k2 = jnp.concatenate([
        jnp.concatenate([k_ref[pl.ds(st, tq), :], kx_ref[pl.ds(st, tq), :]], axis=1)
        for st in starts], axis=0)
    v2 = jnp.concatenate([
        jnp.concatenate([v_ref[pl.ds(st, tq), :] for st in starts], axis=0),
        jnp.ones((span, LANES), BF16)], axis=1)
    bias = wb_ref[0]
    gates = jax.nn.sigmoid(gl_ref[...].astype(F32))
    pair = NSA_GROUP // 2
    for h0 in range(0, NSA_GROUP, pair):
        heads = range(h0, h0 + pair)
        q2 = jnp.concatenate([
            jnp.concatenate([q_ref[:, r * HEAD_DIM:(r + 1) * HEAD_DIM], ax_ref[r]], axis=1)
            for r in heads], axis=0)
        s = _nt_dot(q2, k2)
        parts = []
        for n, r in enumerate(heads):
            sr = s[n * tq:(n + 1) * tq] + bias
            parts.append(jnp.exp(sr - jnp.max(sr, axis=-1, keepdims=True)).astype(BF16))
        ov = _dot(jnp.concatenate(parts, axis=0), v2)
        o_win = ov[:, :HEAD_DIM] / ov[:, HEAD_DIM:]
        for n, r in enumerate(heads):
            cols = slice(r * HEAD_DIM, (r + 1) * HEAD_DIM)
            o = (gates[:, 3 * r:3 * r + 1] * oc_ref[:, cols].astype(F32)
                 + gates[:, 3 * r + 1:3 * r + 2] * os_ref[:, cols].astype(F32)
                 + gates[:, 3 * r + 2:3 * r + 3] * o_win[n * tq:(n + 1) * tq])
            o_ref[:, cols] = o.astype(o_ref.dtype)


WIN_TQ = 256


def _window_bias(tq):
    nvar = WIN // tq + 1
    r = np.arange(tq)[None, :, None]
    c = np.arange(WIN + tq)[None, None, :]
    t0 = (np.arange(nvar) * tq)[:, None, None]
    kpos = t0 - WIN + c
    dist = (t0 + r) - kpos
    ok = (kpos >= 0) & (dist >= 0) & (dist < WIN)
    return jnp.asarray(np.where(ok, 0.0, NEG_INF), F32)


def win_attn(proj, kx, axq, wbias, o_cmp, o_slc, batch, seq, tq=WIN_TQ):
    n = proj.shape[0]
    bg = batch * NSA_KV_HEADS
    qt = seq // tq
    nvar = wbias.shape[0]
    rowblk = lambda i, t: ((i // NSA_KV_HEADS) * qt + t, i % NSA_KV_HEADS)
    kern = functools.partial(_win_attn_kernel, tq=tq)
    return pl.pallas_call(
        kern,
        grid=(bg, qt),
        in_specs=[
            pl.BlockSpec((tq, NSA_GROUP * HEAD_DIM), rowblk),
            pl.BlockSpec((NSA_GROUP, tq, LANES), lambda i, t: (i % NSA_KV_HEADS, t, 0)),
            pl.BlockSpec((1, tq, WIN + tq), lambda i, t: (jnp.minimum(t, nvar - 1), 0, 0)),
            pl.BlockSpec((seq, HEAD_DIM), lambda i, t: (i // NSA_KV_HEADS, COL_KWIN + i % NSA_KV_HEADS)),
            pl.BlockSpec((seq, HEAD_DIM), lambda i, t: (i // NSA_KV_HEADS, COL_VWIN + i % NSA_KV_HEADS)),
            pl.BlockSpec((seq, LANES), lambda i, t: (0, 0)),
            pl.BlockSpec((tq, LANES), lambda i, t: ((i // NSA_KV_HEADS) * qt + t, COL_GATE + i % NSA_KV_HEADS)),
            pl.BlockSpec((tq, NSA_GROUP * HEAD_DIM), rowblk),
            pl.BlockSpec((tq, NSA_GROUP * HEAD_DIM), rowblk),
        ],
        out_specs=pl.BlockSpec((tq, NSA_GROUP * HEAD_DIM), rowblk),
        out_shape=jax.ShapeDtypeStruct((n, NSA_WIDTH), BF16),
        compiler_params=_cparams("parallel", "parallel"),
        name="win_attn",
    )(proj, axq, wbias, proj, proj, kx, proj, o_cmp, o_slc)


def _gmlp_kernel(u_ref, v_ref, lg_ref, lb_ref, ws_ref, bsx_ref, o_ref, *, tm):
    u = jax.nn.gelu(u_ref[...].astype(F32))
    v = jax.nn.gelu(v_ref[...].astype(F32))
    mu = jnp.mean(v, axis=-1, keepdims=True)
    vc = v - mu
    var = jnp.mean(vc * vc, axis=-1, keepdims=True)
    vn = (vc * lax.rsqrt(var + EPS) * lg_ref[...] + lb_ref[...]).astype(BF16)
    c = GMLP_CHUNK
    tri = (lax.broadcasted_iota(jnp.int32, (c, c), 0) >= lax.broadcasted_iota(jnp.int32, (c, c), 1))
    bsx = bsx_ref[...]
    for gi in range(GMLP_GROUPS):
        w = jnp.where(tri, ws_ref[gi], 0.0).astype(BF16)
        cols = slice(gi * HEAD_DIM, (gi + 1) * HEAD_DIM)
        for ci in range(tm // c):
            rows = slice(ci * c, (ci + 1) * c)
            s = _dot(w, vn[rows, cols]) + bsx[:, cols]
            o_ref[rows, cols] = (u[rows, cols] * s).astype(o_ref.dtype)


def gmlp(proj, ln_g, ln_b, ws, bsx, tm=512):
    n = proj.shape[0]
    ublk = GMLP_WIDTH // LANES
    kern = functools.partial(_gmlp_kernel, tm=tm)
    return pl.pallas_call(
        kern,
        grid=(n // tm,),
        in_specs=[
            pl.BlockSpec((tm, GMLP_WIDTH), lambda i: (i, COL_U // ublk)),
            pl.BlockSpec((tm, GMLP_WIDTH), lambda i: (i, COL_V // ublk)),
            pl.BlockSpec((1, GMLP_WIDTH), lambda i: (0, 0)),
            pl.BlockSpec((1, GMLP_WIDTH), lambda i: (0, 0)),
            pl.BlockSpec((GMLP_GROUPS, GMLP_CHUNK, GMLP_CHUNK), lambda i: (0, 0, 0)),
            pl.BlockSpec((GMLP_CHUNK, GMLP_WIDTH), lambda i: (0, 0)),
        ],
        out_specs=pl.BlockSpec((tm, GMLP_WIDTH), lambda i: (i, 0)),
        out_shape=jax.ShapeDtypeStruct((n, GMLP_WIDTH), BF16),
        compiler_params=_cparams("parallel"),
        name="gmlp",
    )(proj, proj, ln_g, ln_b, ws, bsx)


def _retention_kernel(q_ref, k_ref, v_ref, g_ref, dec_ref, zeta_ref, xi_ref, dch_ref, gn_ref,
                      o_ref, s_ref, st_ref, *, seq):
    c = RET_CHUNK
    nc = seq // c
    heads = [slice(hh * HEAD_DIM, (hh + 1) * HEAD_DIM) for hh in range(RET_HEADS)]

    def kv_body(ci, carry):
        rows = pl.ds(pl.multiple_of(ci * c, c), c)
        for hh, cols in enumerate(heads):
            kz = (k_ref[rows, cols].astype(F32) * zeta_ref[hh]).astype(BF16)
            st_ref[ci, hh] = lax.dot_general(kz, v_ref[rows, cols], (((0,), (0,)), ((), ())),
                                             preferred_element_type=F32)
        return carry

    lax.fori_loop(0, nc, kv_body, 0, unroll=2)

    s_ref[...] = jnp.zeros(s_ref.shape, F32)

    def scan_body(ci, carry):
        for hh in range(RET_HEADS):
            state = s_ref[hh]
            s_ref[hh] = dch_ref[hh] * state + st_ref[ci, hh]
            st_ref[ci, hh] = state
        return carry

    lax.fori_loop(0, nc, scan_body, 0)

    def out_body(ci, carry):
        rows = pl.ds(pl.multiple_of(ci * c, c), c)
        for hh, cols in enumerate(heads):
            q = q_ref[rows, cols]
            v = v_ref[rows, cols]
            scores = _nt_dot(q, k_ref[rows, cols]) * dec_ref[hh]
            y = _dot(scores.astype(BF16), v) + _dot(q, st_ref[ci, hh].astype(BF16)) * xi_ref[hh]
            mu = jnp.mean(y, axis=-1, keepdims=True)
            yc = y - mu
            var = jnp.mean(yc * yc, axis=-1, keepdims=True)
            yn = yc * lax.rsqrt(var + EPS) * gn_ref[:, cols]
            o_ref[rows, cols] = (jax.nn.silu(g_ref[rows, cols].astype(F32)) * yn).astype(o_ref.dtype)
        return carry

    lax.fori_loop(0, nc, out_body, 0, unroll=2)


def retention(proj, tabs, gn_g, batch, seq):
    n = proj.shape[0]
    dec, zeta, xi, dch = tabs
    rblk = RET_WIDTH // LANES
    kern = functools.partial(_retention_kernel, seq=seq)
    full = lambda shape: pl.BlockSpec(shape, lambda b: (0,) * len(shape))
    return pl.pallas_call(
        kern,
        grid=(batch,),
        in_specs=[
            pl.BlockSpec((seq, RET_WIDTH), lambda b: (b, COL_RQ // rblk)),
            pl.BlockSpec((seq, RET_WIDTH), lambda b: (b, COL_RK // rblk)),
            pl.BlockSpec((seq, RET_WIDTH), lambda b: (b, COL_RV // rblk)),
            pl.BlockSpec((seq, RET_WIDTH), lambda b: (b, COL_RG // rblk)),
            full(dec.shape), full(zeta.shape), full(xi.shape), full(dch.shape),
            full((1, RET_WIDTH)),
        ],
        out_specs=pl.BlockSpec((seq, RET_WIDTH), lambda b: (b, 0)),
        out_shape=jax.ShapeDtypeStruct((n, RET_WIDTH), BF16),
        scratch_shapes=[pltpu.VMEM((RET_HEADS, HEAD_DIM, HEAD_DIM), F32),
                        pltpu.VMEM((seq // RET_CHUNK, RET_HEADS, HEAD_DIM, HEAD_DIM), F32)],
        compiler_params=_cparams("parallel"),
        name="retention",
    )(proj, proj, proj, proj, dec, zeta, xi, dch, gn_g)


ROUTE_E1, ROUTE_E2, ROUTE_G1, ROUTE_G2 = 0, 1, 2, 3


def _top2_route(logits):
    lane = lax.broadcasted_iota(jnp.int32, logits.shape, 1)
    lg = jnp.where(lane < N_EXPERTS, logits, -jnp.inf)
    v1 = jnp.max(lg, axis=-1, keepdims=True)
    i1 = jnp.min(jnp.where(lg == v1, lane, LANES), axis=-1, keepdims=True)
    lg2 = jnp.where(lane == i1, -jnp.inf, lg)
    v2 = jnp.max(lg2, axis=-1, keepdims=True)
    i2 = jnp.min(jnp.where(lg2 == v2, lane, LANES), axis=-1, keepdims=True)
    e2 = jnp.exp(v2 - v1)
    den = 1.0 + e2
    out = jnp.where(lane == ROUTE_E1, i1.astype(F32), 0.0)
    out = jnp.where(lane == ROUTE_E2, i2.astype(F32), out)
    out = jnp.where(lane == ROUTE_G1, 1.0 / den, out)
    return jnp.where(lane == ROUTE_G2, e2 / den, out)


def _out_proj_kernel(*refs, router):
    if router:
        nsa_ref, gm_ref, ret_ref, h_ref, w_ref, g2_ref, wrh_ref, wrl_ref, br_ref, ho_ref, f_ref, rt_ref = refs
    else:
        nsa_ref, gm_ref, ret_ref, h_ref, w_ref, g2_ref, ho_ref, f_ref = refs
    acc = _dot(nsa_ref[...], w_ref[:NSA_WIDTH, :])
    acc = acc + _dot(gm_ref[...], w_ref[NSA_WIDTH:NSA_WIDTH + GMLP_WIDTH, :])
    acc = acc + _dot(ret_ref[...], w_ref[NSA_WIDTH + GMLP_WIDTH:, :])
    hn = h_ref[...] + acc
    ho_ref[...] = hn
    ms = jnp.mean(hn * hn, axis=-1, keepdims=True)
    f = hn * lax.rsqrt(ms + EPS) * g2_ref[...]
    f_hi = f.astype(BF16)
    if router:
        f_ref[...] = f
        f_lo = (f - f_hi.astype(F32)).astype(BF16)
        logits = (_dot(f_hi, wrh_ref[...]) + _dot(f_lo, wrh_ref[...]) + _dot(f_hi, wrl_ref[...])
                  + br_ref[...])
        rt_ref[...] = _top2_route(logits)
    else:
        f_ref[...] = f_hi


def out_proj(o_nsa, o_gm, o_ret, h2, w, g2, router_w=None, tm=512):
    n, d = h2.shape
    tm = min(tm, n)
    router = router_w is not None
    row = lambda width: pl.BlockSpec((tm, width), lambda i: (i, 0))
    full = lambda shape: pl.BlockSpec(shape, lambda i: (0,) * len(shape))
    in_specs = [row(NSA_WIDTH), row(GMLP_WIDTH), row(RET_WIDTH), row(d), full(w.shape), full((1, d))]
    args = [o_nsa, o_gm, o_ret, h2, w, g2]
    if router:
        in_specs += [full((d, LANES)), full((d, LANES)), full((1, LANES))]
        args += list(router_w)
        out_specs = [row(d), row(d), row(LANES)]
        out_shape = [jax.ShapeDtypeStruct((n, d), F32), jax.ShapeDtypeStruct((n, d), F32),
                     jax.ShapeDtypeStruct((n, LANES), F32)]
    else:
        out_specs = [row(d), row(d)]
        out_shape = [jax.ShapeDtypeStruct((n, d), F32), jax.ShapeDtypeStruct((n, d), BF16)]
    return pl.pallas_call(
        functools.partial(_out_proj_kernel, router=router),
        grid=(n // tm,),
        in_specs=in_specs,
        out_specs=out_specs,
        out_shape=out_shape,
        compiler_params=_cparams("parallel"),
        name="out_proj_router" if router else "out_proj",
    )(*args)


def _ffn_kernel(*refs, final):
    if final:
        x_ref, h_ref, w1_ref, w3_ref, w2_ref, fg_ref, o_ref = refs
    else:
        x_ref, h_ref, w1_ref, w3_ref, w2_ref, o_ref = refs
    k = pl.program_id(1)

    @pl.when(k == 0)
    def _():
        o_ref[...] = h_ref[...]

    x = x_ref[...]
    hid = jax.nn.silu(_dot(x, w1_ref[...])) * _dot(x, w3_ref[...])
    o_ref[...] += _dot(hid.astype(BF16), w2_ref[...])

    if final:
        @pl.when(k == pl.num_programs(1) - 1)
        def _():
            y = o_ref[...]
            ms = jnp.mean(y * y, axis=-1, keepdims=True)
            o_ref[...] = y * lax.rsqrt(ms + EPS) * fg_ref[...]


def ffn(f, h2, w1, w3, w2, final_g=None, tm=512, tf=512):
    n, d = h2.shape
    ff = w1.shape[1]
    tm = min(tm, n)
    tf = min(tf, ff)
    in_specs = [
        pl.BlockSpec((tm, d), lambda i, k: (i, 0)),
        pl.BlockSpec((tm, d), lambda i, k: (i, 0)),
        pl.BlockSpec((d, tf), lambda i, k: (0, k)),
        pl.BlockSpec((d, tf), lambda i, k: (0, k)),
        pl.BlockSpec((tf, d), lambda i, k: (k, 0)),
    ]
    args = [f, h2, w1, w3, w2]
    if final_g is not None:
        in_specs.append(pl.BlockSpec((1, d), lambda i, k: (0, 0)))
        args.append(final_g)
    return pl.pallas_call(
        functools.partial(_ffn_kernel, final=final_g is not None),
        grid=(n // tm, ff // tf),
        in_specs=in_specs,
        out_specs=pl.BlockSpec((tm, d), lambda i, k: (i, 0)),
        out_shape=jax.ShapeDtypeStruct((n, d), F32),
        compiler_params=_cparams("parallel", "arbitrary"),
        name="ffn",
    )(*args)


MOE_TM = 512


def _row_copy(src_ref, src_row, dst_ref, dst_row, sem):
    return pltpu.make_async_copy(src_ref.at[pl.ds(src_row, 1)], dst_ref.at[pl.ds(dst_row, 1)], sem)


def _moe_ffn_kernel(te_ref, nu_ref, src_ref, f_ref, w1_ref, w3_ref, w2_ref, o_ref, xin_ref, xb_ref, sem,
                    *, tm, chunk):
    del te_ref
    i = pl.program_id(0)
    k = pl.program_id(1)
    nt = pl.num_programs(0)
    nk = pl.num_programs(1)
    n_used = nu_ref[0]
    used = i < n_used
    slot = i % 2

    def start_chunk(tile, c, dst_slot):
        for u in range(chunk):
            row = c * chunk + u
            _row_copy(f_ref, src_ref[tile * tm + row], xin_ref.at[dst_slot], row, sem.at[dst_slot]).start()

    def wait_tile(dst_slot):
        def drain(c, carry):
            for u in range(chunk):
                _row_copy(f_ref, 0, xin_ref.at[dst_slot], 0, sem.at[dst_slot]).wait()
            return carry
        lax.fori_loop(0, nk, drain, 0)

    @pl.when((i == 0) & (k == 0))
    def _():
        def first(c, carry):
            start_chunk(0, c, 0)
            return carry
        lax.fori_loop(0, nk, first, 0)

    @pl.when((k == 0) & (i <= n_used))
    def _():
        wait_tile(slot)
        xb_ref[...] = xin_ref[slot, :tm, :].astype(BF16)

    @pl.when(k == 0)
    def _():
        o_ref[...] = jnp.zeros(o_ref.shape, F32)

    @pl.when(used)
    def _():
        start_chunk(i + 1, k, 1 - slot)
        x = xb_ref[...]
        hid = jax.nn.silu(_dot(x, w1_ref[0])) * _dot(x, w3_ref[0])
        o_ref[...] += _dot(hid.astype(BF16), w2_ref[0])

    @pl.when(used & (i == nt - 1) & (k == nk - 1))
    def _():
        wait_tile(1 - slot)


def moe_ffn(tile_expert, n_used, src, f, w1, w3, w2, rows, tm=MOE_TM, tf=256):
    n, d = f.shape
    ff = w1.shape[2]
    tf = min(tf, ff)
    nk = ff // tf
    chunk = pl.cdiv(tm, nk)
    buf_rows = ((chunk * nk + 7) // 8) * 8
    kk = lambda i, k, nu: jnp.where(i < nu[0], k, nk - 1)
    return pl.pallas_call(
        functools.partial(_moe_ffn_kernel, tm=tm, chunk=chunk),
        grid_spec=pltpu.PrefetchScalarGridSpec(
            num_scalar_prefetch=3,
            grid=(rows // tm, nk),
            in_specs=[
                pl.BlockSpec(memory_space=pl.ANY),
                pl.BlockSpec((1, d, tf), lambda i, k, te, nu, src: (te[i], 0, kk(i, k, nu))),
                pl.BlockSpec((1, d, tf), lambda i, k, te, nu, src: (te[i], 0, kk(i, k, nu))),
                pl.BlockSpec((1, tf, d), lambda i, k, te, nu, src: (te[i], kk(i, k, nu), 0)),
            ],
            out_specs=pl.BlockSpec((tm, d), lambda i, k, te, nu, src: (i, 0)),
            scratch_shapes=[pltpu.VMEM((2, buf_rows, d), F32), pltpu.VMEM((tm, d), BF16),
                            pltpu.SemaphoreType.DMA((2,))],
        ),
        out_shape=jax.ShapeDtypeStruct((rows, d), F32),
        compiler_params=_cparams("arbitrary", "arbitrary"),
        name="moe_ffn",
    )(tile_expert, n_used, src, f, w1, w3, w2)


def _combine_kernel(*refs, tm, final):
    if final:
        dest_ref, h_ref, rt_ref, yg_ref, fg_ref, o_ref, buf_ref, sem = refs
    else:
        dest_ref, h_ref, rt_ref, yg_ref, o_ref, buf_ref, sem = refs
    i = pl.program_id(0)
    nt = pl.num_programs(0)
    slot = i % 2

    def start_row(tile, j, dst_slot):
        for s in range(TOP_K):
            _row_copy(yg_ref, dest_ref[(tile * tm + j) * TOP_K + s], buf_ref.at[dst_slot, s], j,
                      sem.at[dst_slot]).start()

    def wait_tile(dst_slot):
        def drain(j, carry):
            for s in range(TOP_K):
                _row_copy(yg_ref, 0, buf_ref.at[dst_slot, s], 0, sem.at[dst_slot]).wait()
            return carry
        lax.fori_loop(0, tm, drain, 0, unroll=8)

    @pl.when(i == 0)
    def _():
        def first(j, carry):
            start_row(0, j, 0)
            return carry
        lax.fori_loop(0, tm, first, 0, unroll=8)

    wait_tile(slot)
    for j in range(tm):
        start_row(i + 1, j, 1 - slot)

    rt = rt_ref[...]
    y = (h_ref[...] + rt[:, ROUTE_G1:ROUTE_G1 + 1] * buf_ref[slot, 0]
         + rt[:, ROUTE_G2:ROUTE_G2 + 1] * buf_ref[slot, 1])
    if final:
        ms = jnp.mean(y * y, axis=-1, keepdims=True)
        y = y * lax.rsqrt(ms + EPS) * fg_ref[...]
    o_ref[...] = y

    @pl.when(i == nt - 1)
    def _():
        wait_tile(1 - slot)


def moe_combine(dest, h2, route, yg, final_g=None, tm=128):
    n, d = h2.shape
    tm = min(tm, n)
    dest = jnp.pad(dest, (0, tm * TOP_K))
    in_specs = [
        pl.BlockSpec((tm, d), lambda i, dest: (i, 0)),
        pl.BlockSpec((tm, LANES), lambda i, dest: (i, 0)),
        pl.BlockSpec(memory_space=pl.ANY),
    ]
    args = [dest, h2, route, yg]
    if final_g is not None:
        in_specs.append(pl.BlockSpec((1, d), lambda i, dest: (0, 0)))
        args.append(final_g)
    return pl.pallas_call(
        functools.partial(_combine_kernel, tm=tm, final=final_g is not None),
        grid_spec=pltpu.PrefetchScalarGridSpec(
            num_scalar_prefetch=1,
            grid=(n // tm,),
            in_specs=in_specs,
            out_specs=pl.BlockSpec((tm, d), lambda i, dest: (i, 0)),
            scratch_shapes=[pltpu.VMEM((2, TOP_K, tm, d), F32), pltpu.SemaphoreType.DMA((2,))],
        ),
        out_shape=jax.ShapeDtypeStruct((n, d), F32),
        compiler_params=_cparams("arbitrary"),
        name="moe_combine",
    )(*args)


def _routing_tables(route, tm):
    n = route.shape[0]
    flat_e = route[:, ROUTE_E1:ROUTE_E2 + 1].astype(jnp.int32).reshape(-1)
    onehot = (flat_e[:, None] == jnp.arange(N_EXPERTS, dtype=jnp.int32)[None, :]).astype(jnp.int32)
    csum = jnp.cumsum(onehot, axis=0)
    rank = jnp.sum(csum * onehot, axis=1) - 1
    counts = csum[-1]
    padded = ((counts + tm - 1) // tm) * tm
    off_end = jnp.cumsum(padded)
    dest = (off_end - padded)[flat_e] + rank
    n_tiles = (n * TOP_K) // tm + N_EXPERTS
    n_used = (off_end[-1] // tm).astype(jnp.int32)
    tile_start = jnp.arange(n_tiles, dtype=jnp.int32) * tm
    tile_start = jnp.minimum(tile_start, off_end[-1] - tm)
    tile_expert = jnp.sum((tile_start[:, None] >= off_end[None, :]).astype(jnp.int32), axis=1)
    rows = n_tiles * tm
    src = jnp.zeros((rows + tm + 8,), jnp.int32).at[dest].set(jnp.arange(n * TOP_K, dtype=jnp.int32) // TOP_K)
    return dest.astype(jnp.int32), src, tile_expert.astype(jnp.int32), n_used.reshape(1), rows


def moe_layer(h2, f, route, w1, w3, w2, final_g=None):
    dest, src, tile_expert, n_used, rows = _routing_tables(route, MOE_TM)
    yg = moe_ffn(tile_expert, n_used, src, f, w1, w3, w2, rows)
    return moe_combine(dest, h2, route, yg, final_g=final_g)


def _reorder_w_in(w_in):
    d = w_in.shape[0]
    kvw = NSA_KV_WIDTH
    o_gate = NSA_WIDTH + 6 * kvw
    o_uv = o_gate + 3 * NSA_HEADS
    o_ret = o_uv + 2 * GMLP_WIDTH
    gate = w_in[:, o_gate:o_uv].reshape(d, NSA_KV_HEADS, 3 * NSA_GROUP)
    gate = jnp.pad(gate, ((0, 0), (0, 0), (0, LANES - 3 * NSA_GROUP))).reshape(d, NSA_KV_HEADS * LANES)
    parts = [w_in[:, :o_gate], w_in[:, o_uv:], gate]
    w = jnp.concatenate(parts, axis=1)
    w = jnp.pad(w, ((0, 0), (0, PROJ_COLS - w.shape[1])))
    scale = np.ones((1, PROJ_COLS), np.float32)
    scale[:, COL_Q * LANES:(COL_Q + NSA_HEADS) * LANES] = HEAD_DIM ** -0.5
    scale[:, COL_RK * LANES:(COL_RK + RET_HEADS) * LANES] = HEAD_DIM ** -0.5
    del o_ret
    return w.astype(BF16), jnp.asarray(scale)


def _key_extra_lanes(seq):
    pos = np.arange(seq)
    kx = np.zeros((seq, LANES), np.float32)
    kx[pos, pos // SLC_LEN] = 1.0
    kx[:, XL_HI] = pos // SLC_LEN
    kx[:, XL_LO] = pos % SLC_LEN
    kx[:, XL_ONE_A] = 1.0
    kx[:, XL_ONE_B] = 1.0
    return jnp.asarray(kx, BF16)


def _share_t(seq):
    nrow = seq // CMP_STRIDE
    nb = seq // SLC_LEN
    c0 = np.arange(nrow)[None, :] * CMP_STRIDE
    s0 = np.arange(nb)[:, None] * SLC_LEN
    overlap = np.minimum(c0 + CMP_LEN, s0 + SLC_LEN) - np.maximum(c0, s0)
    share = np.clip(overlap, 0, CMP_LEN).astype(np.float32) / CMP_LEN
    share[:, nrow - 1] = 0.0
    return jnp.asarray(share, BF16)


def _retention_tables():
    hh = np.arange(RET_HEADS, dtype=np.float64)
    lg = np.log1p(-np.exp2(-5.0 - hh))
    nn = np.arange(RET_CHUNK, dtype=np.float64)
    rel = nn[:, None] - nn[None, :]
    dec = np.where(rel >= 0, np.exp(lg[:, None, None] * np.maximum(rel, 0.0)), 0.0)
    zeta = np.exp(lg[:, None] * (RET_CHUNK - 1.0 - nn))[:, :, None]
    xi = np.exp(lg[:, None] * (nn + 1.0))[:, :, None]
    dch = np.broadcast_to(np.exp(lg * RET_CHUNK)[:, None, None], (RET_HEADS, 1, HEAD_DIM))
    return tuple(jnp.asarray(a, F32) for a in (dec, zeta, xi, dch))


def _token_mixer(h2, batch, seq, ln1_g, w_in, pe_k, w1_k, w2_k, pe_v, w1_v, w2_v,
                 g_ln_g, g_ln_b, g_ws, g_bs, ret_gn_g, consts):
    kx, sht, ret_tabs, cmp_tabs, axq, wbias = consts
    w_r, colscale = _reorder_w_in(w_in)
    proj = in_proj(h2, ln1_g[None, :], w_r, colscale)
    nrow = seq // CMP_STRIDE
    kv = proj[:, COL_KCMP * LANES:(COL_VCMP + NSA_KV_HEADS) * LANES]
    kv = kv.reshape(batch, nrow, CMP_STRIDE, 2, NSA_KV_HEADS, HEAD_DIM)
    rows = kv.transpose(3, 0, 4, 1, 2, 5).reshape(2, batch * NSA_KV_HEADS, nrow, CMP_STRIDE * HEAD_DIM)
    pe = jnp.stack([pe_k, pe_v]).reshape(2, 1, CMP_LEN * HEAD_DIM)
    kvc = compress(rows, pe, jnp.stack([w1_k, w1_v]).astype(BF16), jnp.stack([w2_k, w2_v]).astype(BF16))
    o_cmp, xsel = cmp_attn(proj, kvc, sht, cmp_tabs, batch, seq)
    o_slc = sel_attn(proj, xsel, kx, axq, batch, seq)
    o_nsa = win_attn(proj, kx, axq, wbias, o_cmp, o_slc, batch, seq)
    bsx = jnp.repeat(g_bs.T, HEAD_DIM, axis=1)
    o_gm = gmlp(proj, g_ln_g[None, :], g_ln_b[None, :], g_ws, bsx)
    o_ret = retention(proj, ret_tabs, ret_gn_g[None, :], batch, seq)
    return o_nsa, o_gm, o_ret


def kernel(x, ln1_g, w_in, cmp_pe_k, cmp_w1_k, cmp_w2_k, cmp_pe_v, cmp_w1_v, cmp_w2_v, gmlp_ln_g, gmlp_ln_b, gmlp_ws, gmlp_bs, ret_gn_g, w_out, ln2_g, ffn_w1, ffn_w3, ffn_w2, moe_wr, moe_br, moe_w1, moe_w3, moe_w2, final_g):
    batch, seq, d = x.shape
    depth = w_in.shape[0]
    consts = (_key_extra_lanes(seq), _share_t(seq), _retention_tables(), _cmp_tables(seq),
              _alibi_q_table(seq), _window_bias(WIN_TQ))
    h2 = x.reshape(batch * seq, d)
    for layer in range(depth):
        o_nsa, o_gm, o_ret = _token_mixer(
            h2, batch, seq, ln1_g[layer], w_in[layer], cmp_pe_k[layer], cmp_w1_k[layer], cmp_w2_k[layer],
            cmp_pe_v[layer], cmp_w1_v[layer], cmp_w2_v[layer], gmlp_ln_g[layer], gmlp_ln_b[layer],
            gmlp_ws[layer], gmlp_bs[layer], ret_gn_g[layer], consts)
        fin = final_g[None, :] if layer == depth - 1 else None
        i = layer // 2
        w_o = w_out[layer].astype(BF16)
        if layer % 2 == 0:
            h2, f = out_proj(o_nsa, o_gm, o_ret, h2, w_o, ln2_g[layer][None, :])
            h2 = ffn(f, h2, ffn_w1[i].astype(BF16), ffn_w3[i].astype(BF16), ffn_w2[i].astype(BF16),
                     final_g=fin)
        else:
            wr = jnp.pad(moe_wr[i], ((0, 0), (0, LANES - N_EXPERTS)))
            wr_hi = wr.astype(BF16)
            wr_lo = (wr - wr_hi.astype(F32)).astype(BF16)
            br = jnp.pad(moe_br[i], (0, LANES - N_EXPERTS))[None, :]
            h2, f, route = out_proj(o_nsa, o_gm, o_ret, h2, w_o, ln2_g[layer][None, :],
                                    router_w=(wr_hi, wr_lo, br))
            h2 = moe_layer(h2, f, route, moe_w1[i].astype(BF16), moe_w3[i].astype(BF16),
                           moe_w2[i].astype(BF16), final_g=fin)
    return h2.reshape(batch, seq, d)
```

```python
import functools

import numpy as np
import jax
import jax.numpy as jnp
from jax import lax
from jax.experimental import pallas as pl
from jax.experimental.pallas import tpu as pltpu

F32 = jnp.float32
BF16 = jnp.bfloat16

HEAD_DIM = 128
NSA_HEADS = 8
NSA_KV_HEADS = 2
NSA_GROUP = NSA_HEADS // NSA_KV_HEADS
NSA_WIDTH = NSA_HEADS * HEAD_DIM
NSA_KV_WIDTH = NSA_KV_HEADS * HEAD_DIM
CMP_LEN = 32
CMP_STRIDE = 16
SLC_LEN = 64
SLC_TOPK = 16
WIN = 512
GMLP_GROUPS = 4
GMLP_CHUNK = 128
GMLP_WIDTH = GMLP_GROUPS * HEAD_DIM
RET_HEADS = 4
RET_CHUNK = 128
RET_WIDTH = RET_HEADS * HEAD_DIM
N_EXPERTS = 8
TOP_K = 2
EPS = 1e-6
NEG_INF = -1e30
FORCE_SCORE = 1e4

LANES = 128
VMEM_LIMIT = 56 * 1024 * 1024

COL_Q = 0
COL_KCMP = 8
COL_VCMP = 10
COL_KSLC = 12
COL_VSLC = 14
COL_KWIN = 16
COL_VWIN = 18
COL_U = 20
COL_V = 24
COL_RQ = 28
COL_RK = 32
COL_RV = 36
COL_RG = 40
COL_GATE = 44
PROJ_COLS = 48 * LANES

XL_HI = 64
XL_LO = 65
XL_ONE_A = 66
XL_ONE_B = 67


def _cparams(*sem):
    return pltpu.CompilerParams(dimension_semantics=sem, vmem_limit_bytes=VMEM_LIMIT)


def _nt_dot(a, b):
    return lax.dot_general(a, b, (((1,), (1,)), ((), ())), preferred_element_type=F32)


def _dot(a, b):
    return jnp.dot(a, b, preferred_element_type=F32)


def _in_proj_kernel(x_ref, g_ref, w_ref, cs_ref, o_ref, xn_ref):
    @pl.when(pl.program_id(1) == 0)
    def _():
        x = x_ref[...]
        ms = jnp.mean(x * x, axis=-1, keepdims=True)
        xn_ref[...] = (x * lax.rsqrt(ms + EPS) * g_ref[...]).astype(BF16)

    acc = _dot(xn_ref[...], w_ref[...])
    o_ref[...] = (acc * cs_ref[...]).astype(o_ref.dtype)


def in_proj(h2, g, w, colscale, tm=1024, tn=1024):
    n, d = h2.shape
    cols = w.shape[1]
    tm = min(tm, n)
    return pl.pallas_call(
        _in_proj_kernel,
        grid=(n // tm, cols // tn),
        in_specs=[
            pl.BlockSpec((tm, d), lambda i, j: (i, 0)),
            pl.BlockSpec((1, d), lambda i, j: (0, 0)),
            pl.BlockSpec((d, tn), lambda i, j: (0, j)),
            pl.BlockSpec((1, tn), lambda i, j: (0, j)),
        ],
        out_specs=pl.BlockSpec((tm, tn), lambda i, j: (i, j)),
        out_shape=jax.ShapeDtypeStruct((n, cols), BF16),
        scratch_shapes=[pltpu.VMEM((tm, d), BF16)],
        compiler_params=_cparams("parallel", "arbitrary"),
        name="in_proj",
    )(h2, g, w, colscale)


def _compress_kernel(r_ref, pe_ref, w1_ref, w2_ref, o_ref):
    half = CMP_STRIDE * HEAD_DIM
    r = r_ref[0, 0].astype(F32)
    nrow = r.shape[0]
    pe = pe_ref[0]
    a = _dot((r + pe[:, :half]).astype(BF16), w1_ref[0, :half, :])
    b = _dot((r + pe[:, half:]).astype(BF16), w1_ref[0, half:, :])
    pre = a + pltpu.roll(b, nrow - 1, 0)
    y = _dot(jax.nn.gelu(pre).astype(BF16), w2_ref[0])
    row = lax.broadcasted_iota(jnp.int32, y.shape, 0)
    o_ref[0, 0] = jnp.where(row < nrow - 1, y, 0.0).astype(o_ref.dtype)


def compress(rows, pe, w1, w2):
    _, bg, nrow, width = rows.shape
    return pl.pallas_call(
        _compress_kernel,
        grid=(2, bg),
        in_specs=[
            pl.BlockSpec((1, 1, nrow, width), lambda s, i: (s, i, 0, 0)),
            pl.BlockSpec((1, 1, 2 * width), lambda s, i: (s, 0, 0)),
            pl.BlockSpec((1, 2 * width, HEAD_DIM), lambda s, i: (s, 0, 0)),
            pl.BlockSpec((1, HEAD_DIM, HEAD_DIM), lambda s, i: (s, 0, 0)),
        ],
        out_specs=pl.BlockSpec((1, 1, nrow, HEAD_DIM), lambda s, i: (s, i, 0, 0)),
        out_shape=jax.ShapeDtypeStruct((2, bg, nrow, HEAD_DIM), BF16),
        compiler_params=_cparams("arbitrary", "arbitrary"),
        name="compress",
    )(rows, pe, w1, w2)


def _group_slope(g, r):
    return jnp.where(g == 0, F32(2.0 ** -(r + 1)), F32(2.0 ** -(r + 1 + NSA_GROUP)))


SEL_TQ = 256
SEL_TK = 512
SEL_TILE_ROWS = 8


def _cmp_attn_kernel(q_ref, kc_ref, vc_ref, sht_ref, dist_ref, cb_ref, grp_ref, o_ref, x_ref, c_ref,
                     *, tq, nb, topk):
    g = pl.program_id(0) % NSA_KV_HEADS
    t0 = pl.program_id(1) * tq
    kc = kc_ref[0, 0]
    vc = vc_ref[0, 0]
    nrow = kc.shape[0]
    distf = dist_ref[...]
    cbias = cb_ref[...]
    row_ok = jnp.where(t0 + lax.broadcasted_iota(jnp.int32, (tq, 1), 0) >= CMP_LEN - 1, 1.0, 0.0)
    psum = jnp.zeros((tq, nrow), F32)
    for r in range(NSA_GROUP):
        q = q_ref[:, r * HEAD_DIM:(r + 1) * HEAD_DIM]
        s = _nt_dot(q, kc) - _group_slope(g, r) * distf + cbias
        e = jnp.exp(s - jnp.max(s, axis=-1, keepdims=True))
        p = e * (row_ok / jnp.sum(e, axis=-1, keepdims=True))
        o_ref[:, r * HEAD_DIM:(r + 1) * HEAD_DIM] = _dot(p.astype(BF16), vc).astype(o_ref.dtype)
        psum = psum + p
    p_hi = psum.astype(BF16)
    p_lo = (psum - p_hi.astype(F32)).astype(BF16)
    sht = sht_ref[...]
    imp = _nt_dot(sht, p_hi) + _nt_dot(sht, p_lo)
    j = lax.broadcasted_iota(jnp.int32, (nb, tq), 0)
    cur = (t0 + lax.broadcasted_iota(jnp.int32, (nb, tq), 1)) // SLC_LEN
    forced = (j == 0) | (j == cur) | (j == cur - 1)
    score = jnp.where(forced, FORCE_SCORE, imp)
    cand = jnp.where(j > cur, -3e38, score)
    jf = j.astype(F32)[:, :LANES]
    groups = [cand[:, c:c + LANES] for c in range(0, tq, LANES)]
    for _ in range(topk):
        for n, cg in enumerate(groups):
            best = jnp.max(cg, axis=0, keepdims=True)
            idx = jnp.min(jnp.where(cg == best, jf, float(nb)), axis=0, keepdims=True)
            groups[n] = jnp.where(jf == idx, -jnp.inf, cg)
    cand = jnp.concatenate(groups, axis=1)
    picked = cand == -jnp.inf
    grp = grp_ref[...]
    cnt = _dot(grp, jnp.where(picked, 1.0, 0.0).astype(BF16))
    c_ref[0] = jnp.broadcast_to(jnp.sum(cnt, axis=1, keepdims=True), c_ref.shape[1:])
    neg = jnp.where(picked, 0.0, NEG_INF)
    neg = jnp.concatenate([neg, jnp.zeros((LANES - nb, tq), F32)], axis=0)
    x_ref[...] = neg.T.astype(x_ref.dtype)


def _cmp_tables(seq):
    nrow = seq // CMP_STRIDE
    t = np.arange(seq)[:, None]
    c = np.arange(nrow)[None, :]
    dist = t - (c * CMP_STRIDE + (CMP_LEN - 1))
    valid = (dist >= 0) & (c < nrow - 1)
    return jnp.asarray(dist, F32), jnp.asarray(np.where(valid, 0.0, NEG_INF), F32)


def cmp_attn(proj, kvc, sht, tables, batch, seq, tq=256):
    n = proj.shape[0]
    nb = seq // SLC_LEN
    bg = batch * NSA_KV_HEADS
    nrow = kvc.shape[2]
    qt = seq // tq
    rowblk = lambda i, t: ((i // NSA_KV_HEADS) * qt + t, i % NSA_KV_HEADS)
    kern = functools.partial(_cmp_attn_kernel, tq=tq, nb=nb, topk=min(SLC_TOPK, nb))
    return pl.pallas_call(
        kern,
        grid=(bg, qt),
        in_specs=[
            pl.BlockSpec((tq, NSA_GROUP * HEAD_DIM), rowblk),
            pl.BlockSpec((1, 1, nrow, HEAD_DIM), lambda i, t: (0, i, 0, 0)),
            pl.BlockSpec((1, 1, nrow, HEAD_DIM), lambda i, t: (1, i, 0, 0)),
            pl.BlockSpec((nb, nrow), lambda i, t: (0, 0)),
            pl.BlockSpec((tq, nrow), lambda i, t: (t, 0)),
            pl.BlockSpec((tq, nrow), lambda i, t: (t, 0)),
            pl.BlockSpec((SEL_TILE_ROWS, nb), lambda i, t: (0, 0)),
        ],
        out_specs=[
            pl.BlockSpec((tq, NSA_GROUP * HEAD_DIM), rowblk),
            pl.BlockSpec((tq, LANES), rowblk),
            pl.BlockSpec((1, SEL_TILE_ROWS, LANES), lambda i, t: (i * qt + t, 0, 0)),
        ],
        out_shape=[
            jax.ShapeDtypeStruct((n, NSA_WIDTH), BF16),
            jax.ShapeDtypeStruct((n, NSA_KV_HEADS * LANES), BF16),
            jax.ShapeDtypeStruct((bg * qt, SEL_TILE_ROWS, LANES), F32),
        ],
        compiler_params=_cparams("parallel", "parallel"),
        name="cmp_attn",
    )(proj, kvc, kvc, sht, *tables, _tile_groups(seq))


def _tile_groups(seq):
    nb = seq // SLC_LEN
    assert seq // SEL_TK <= SEL_TILE_ROWS
    grp = np.zeros((SEL_TILE_ROWS, nb), np.float32)
    grp[np.arange(nb) // (SEL_TK // SLC_LEN), np.arange(nb)] = 1.0
    return jnp.asarray(grp, BF16)


def _active_key_tiles(counts, batch, seq, tq, tk):
    qt = seq // tq
    n_full = ((jnp.arange(qt, dtype=jnp.int32) * tq + tq - 1) // tk)[None, :, None]
    kt = jnp.arange(SEL_TILE_ROWS, dtype=jnp.int32)[None, None, :]
    cnt = counts[:, :, 0].reshape(batch * NSA_KV_HEADS, qt, SEL_TILE_ROWS)
    active = (cnt > 0) & (kt < n_full)
    order = jnp.argsort(jnp.where(active, 0, 1), axis=-1, stable=True).astype(jnp.int32)
    return order.reshape(-1), jnp.sum(active, axis=-1).astype(jnp.int32).reshape(-1)


def _alibi_q_table(seq):
    t = np.arange(seq)
    tab = np.zeros((NSA_HEADS, seq, LANES), np.float32)
    for h in range(NSA_HEADS):
        slope = 2.0 ** (-8.0 * (h + 1) / NSA_HEADS)
        tab[h, :, XL_HI] = slope * SLC_LEN
        tab[h, :, XL_LO] = slope
        tab[h, :, XL_ONE_A] = -(slope * SLC_LEN) * (t // SLC_LEN)
        tab[h, :, XL_ONE_B] = -slope * (t % SLC_LEN)
    return jnp.asarray(tab, BF16)


def _sel_attn_kernel(act_ref, nact_ref, q_ref, x_ref, ax_ref, k_ref, v_ref, kx_ref, o_ref,
                     q2_ref, sa_ref, sb_ref, m_ref, acc_ref, *, tq, tk):
    t0 = pl.program_id(1) * tq
    n_kt = (t0 + tq - 1) // tk + 1
    rows = NSA_GROUP * tq
    half = rows // 2
    tpos = t0 + (lax.broadcasted_iota(jnp.int32, (half, tk), 0) & (tq - 1))
    kcol = lax.broadcasted_iota(jnp.int32, (half, tk), 1)

    xsel = x_ref[...]
    q2_ref[...] = jnp.concatenate([
        jnp.concatenate([q_ref[:, r * HEAD_DIM:(r + 1) * HEAD_DIM], xsel + ax_ref[r]], axis=1)
        for r in range(NSA_GROUP)], axis=0)
    m_ref[...] = jnp.full(m_ref.shape, -jnp.inf, F32)
    acc_ref[...] = jnp.zeros(acc_ref.shape, F32)
    ones = jnp.ones((tk, LANES), BF16)

    def scores(kt, dst_ref):
        ks = pl.multiple_of(kt * tk, tk)
        k2 = jnp.concatenate([k_ref[pl.ds(ks, tk), :], kx_ref[pl.ds(ks, tk), :]], axis=1)
        dst_ref[...] = _nt_dot(q2_ref[...], k2)

    def update(kt, src_ref, causal):
        ks = pl.multiple_of(kt * tk, tk)
        v2 = jnp.concatenate([v_ref[pl.ds(ks, tk), :], ones], axis=1)
        step = half if causal else rows
        for r0 in range(0, rows, step):
            rs = slice(r0, r0 + step)
            s = src_ref[rs, :]
            if causal:
                s = jnp.where(ks + kcol <= tpos, s, NEG_INF)
            m_prev = m_ref[rs, :]
            m_new = jnp.maximum(m_prev, jnp.max(s, axis=-1, keepdims=True))
            alpha = jnp.exp(m_prev - m_new)
            p = jnp.exp(s - jnp.concatenate([m_new] * (tk // LANES), axis=1))
            acc_ref[rs, :] = (jnp.concatenate([alpha, alpha], axis=1) * acc_ref[rs, :]
                              + _dot(p.astype(BF16), v2))
            m_ref[rs, :] = m_new

    step_id = pl.program_id(0) * pl.num_programs(1) + pl.program_id(1)
    n_act = nact_ref[step_id]
    diag = n_kt - 1

    def visited(j):
        return jnp.where(j < n_act, act_ref[step_id * SEL_TILE_ROWS + jnp.minimum(j, SEL_TILE_ROWS - 1)], diag)

    scores(visited(0), sa_ref)

    def body(j, carry):
        scores(visited(2 * j + 1), sb_ref)
        update(visited(2 * j), sa_ref, False)
        scores(visited(2 * j + 2), sa_ref)
        update(visited(2 * j + 1), sb_ref, False)
        return carry

    lax.fori_loop(0, n_act // 2, body, 0)

    @pl.when(n_act % 2 == 1)
    def _():
        scores(diag, sb_ref)
        update(visited(n_act - 1), sa_ref, False)
        update(diag, sb_ref, True)

    @pl.when(n_act % 2 == 0)
    def _():
        update(diag, sa_ref, True)

    acc = acc_ref[...]
    o = acc[:, :HEAD_DIM] / acc[:, HEAD_DIM:]
    for r in range(NSA_GROUP):
        o_ref[:, r * HEAD_DIM:(r + 1) * HEAD_DIM] = o[r * tq:(r + 1) * tq].astype(o_ref.dtype)


def sel_attn(proj, xsel, counts, kx, axq, batch, seq, tq=SEL_TQ, tk=SEL_TK):
    n = proj.shape[0]
    bg = batch * NSA_KV_HEADS
    qt = seq // tq
    act, n_act = _active_key_tiles(counts, batch, seq, tq, tk)
    rowblk = lambda i, t, a, na: ((i // NSA_KV_HEADS) * qt + t, i % NSA_KV_HEADS)
    kern = functools.partial(_sel_attn_kernel, tq=tq, tk=tk)
    return pl.pallas_call(
        kern,
        grid_spec=pltpu.PrefetchScalarGridSpec(
            num_scalar_prefetch=2,
            grid=(bg, qt),
            in_specs=[
                pl.BlockSpec((tq, NSA_GROUP * HEAD_DIM), rowblk),
                pl.BlockSpec((tq, LANES), rowblk),
                pl.BlockSpec((NSA_GROUP, tq, LANES), lambda i, t, a, na: (i % NSA_KV_HEADS, t, 0)),
                pl.BlockSpec((seq, HEAD_DIM),
                             lambda i, t, a, na: (i // NSA_KV_HEADS, COL_KSLC + i % NSA_KV_HEADS)),
                pl.BlockSpec((seq, HEAD_DIM),
                             lambda i, t, a, na: (i // NSA_KV_HEADS, COL_VSLC + i % NSA_KV_HEADS)),
                pl.BlockSpec((seq, LANES), lambda i, t, a, na: (0, 0)),
            ],
            out_specs=pl.BlockSpec((tq, NSA_GROUP * HEAD_DIM), rowblk),
            scratch_shapes=[
                pltpu.VMEM((NSA_GROUP * tq, 2 * HEAD_DIM), BF16),
                pltpu.VMEM((NSA_GROUP * tq, tk), F32),
                pltpu.VMEM((NSA_GROUP * tq, tk), F32),
                pltpu.VMEM((NSA_GROUP * tq, LANES), F32),
                pltpu.VMEM((NSA_GROUP * tq, 2 * HEAD_DIM), F32),
            ],
        ),
        out_shape=jax.ShapeDtypeStruct((n, NSA_WIDTH), BF16),
        compiler_params=_cparams("parallel", "parallel"),
        name="sel_attn",
    )(act, n_act, proj, xsel, axq, proj, proj, kx)


def _win_attn_kernel(q_ref, ax_ref, wb_ref, k_ref, v_ref, kx_ref, gl_ref, oc_ref, os_ref, o_ref, *, tq):
    t0 = pl.program_id(1) * tq
    span = WIN + tq
    starts = [pl.multiple_of(jnp.maximum(t0 - WIN + c * tq, 0), LANES) for c in range(span // tq)]
    k2 = jnp.concatenate([
        jnp.concatenate([k_ref[pl.ds(st, tq), :], kx_ref[pl.ds(st, tq), :]], axis=1)
        for st in starts], axis=0)
    v2 = jnp.concatenate([
        jnp.concatenate([v_ref[pl.ds(st, tq), :] for st in starts], axis=0),
        jnp.ones((span, LANES), BF16)], axis=1)
    bias = wb_ref[0]
    gates = jax.nn.sigmoid(gl_ref[...].astype(F32))
    pair = NSA_GROUP // 2
    for h0 in range(0, NSA_GROUP, pair):
        heads = range(h0, h0 + pair)
        q2 = jnp.concatenate([
            jnp.concatenate([q_ref[:, r * HEAD_DIM:(r + 1) * HEAD_DIM], ax_ref[r]], axis=1)
            for r in heads], axis=0)
        s = _nt_dot(q2, k2)
        parts = []
        for n, r in enumerate(heads):
            sr = s[n * tq:(n + 1) * tq] + bias
            parts.append(jnp.exp(sr - jnp.max(sr, axis=-1, keepdims=True)).astype(BF16))
        ov = _dot(jnp.concatenate(parts, axis=0), v2)
        o_win = ov[:, :HEAD_DIM] / ov[:, HEAD_DIM:]
        for n, r in enumerate(heads):
            cols = slice(r * HEAD_DIM, (r + 1) * HEAD_DIM)
            o = (gates[:, 3 * r:3 * r + 1] * oc_ref[:, cols].astype(F32)
                 + gates[:, 3 * r + 1:3 * r + 2] * os_ref[:, cols].astype(F32)
                 + gates[:, 3 * r + 2:3 * r + 3] * o_win[n * tq:(n + 1) * tq])
            o_ref[:, cols] = o.astype(o_ref.dtype)


WIN_TQ = 256


def _window_bias(tq):
    nvar = WIN // tq + 1
    r = np.arange(tq)[None, :, None]
    c = np.arange(WIN + tq)[None, None, :]
    t0 = (np.arange(nvar) * tq)[:, None, None]
    kpos = t0 - WIN + c
    dist = (t0 + r) - kpos
    ok = (kpos >= 0) & (dist >= 0) & (dist < WIN)
    return jnp.asarray(np.where(ok, 0.0, NEG_INF), F32)


def win_attn(proj, kx, axq, wbias, o_cmp, o_slc, batch, seq, tq=WIN_TQ):
    n = proj.shape[0]
    bg = batch * NSA_KV_HEADS
    qt = seq // tq
    nvar = wbias.shape[0]
    rowblk = lambda i, t: ((i // NSA_KV_HEADS) * qt + t, i % NSA_KV_HEADS)
    kern = functools.partial(_win_attn_kernel, tq=tq)
    return pl.pallas_call(
        kern,
        grid=(bg, qt),
        in_specs=[
            pl.BlockSpec((tq, NSA_GROUP * HEAD_DIM), rowblk),
            pl.BlockSpec((NSA_GROUP, tq, LANES), lambda i, t: (i % NSA_KV_HEADS, t, 0)),
            pl.BlockSpec((1, tq, WIN + tq), lambda i, t: (jnp.minimum(t, nvar - 1), 0, 0)),
            pl.BlockSpec((seq, HEAD_DIM), lambda i, t: (i // NSA_KV_HEADS, COL_KWIN + i % NSA_KV_HEADS)),
            pl.BlockSpec((seq, HEAD_DIM), lambda i, t: (i // NSA_KV_HEADS, COL_VWIN + i % NSA_KV_HEADS)),
            pl.BlockSpec((seq, LANES), lambda i, t: (0, 0)),
            pl.BlockSpec((tq, LANES), lambda i, t: ((i // NSA_KV_HEADS) * qt + t, COL_GATE + i % NSA_KV_HEADS)),
            pl.BlockSpec((tq, NSA_GROUP * HEAD_DIM), rowblk),
            pl.BlockSpec((tq, NSA_GROUP * HEAD_DIM), rowblk),
        ],
        out_specs=pl.BlockSpec((tq, NSA_GROUP * HEAD_DIM), rowblk),
        out_shape=jax.ShapeDtypeStruct((n, NSA_WIDTH), BF16),
        compiler_params=_cparams("parallel", "parallel"),
        name="win_attn",
    )(proj, axq, wbias, proj, proj, kx, proj, o_cmp, o_slc)


def _gmlp_kernel(u_ref, v_ref, lg_ref, lb_ref, ws_ref, bsx_ref, o_ref, *, tm):
    u = jax.nn.gelu(u_ref[...].astype(F32))
    v = jax.nn.gelu(v_ref[...].astype(F32))
    mu = jnp.mean(v, axis=-1, keepdims=True)
    vc = v - mu
    var = jnp.mean(vc * vc, axis=-1, keepdims=True)
    vn = (vc * lax.rsqrt(var + EPS) * lg_ref[...] + lb_ref[...]).astype(BF16)
    c = GMLP_CHUNK
    tri = (lax.broadcasted_iota(jnp.int32, (c, c), 0) >= lax.broadcasted_iota(jnp.int32, (c, c), 1))
    bsx = bsx_ref[...]
    for gi in range(GMLP_GROUPS):
        w = jnp.where(tri, ws_ref[gi], 0.0).astype(BF16)
        cols = slice(gi * HEAD_DIM, (gi + 1) * HEAD_DIM)
        for ci in range(tm // c):
            rows = slice(ci * c, (ci + 1) * c)
            s = _dot(w, vn[rows, cols]) + bsx[:, cols]
            o_ref[rows, cols] = (u[rows, cols] * s).astype(o_ref.dtype)


def gmlp(proj, ln_g, ln_b, ws, bsx, tm=512):
    n = proj.shape[0]
    ublk = GMLP_WIDTH // LANES
    kern = functools.partial(_gmlp_kernel, tm=tm)
    return pl.pallas_call(
        kern,
        grid=(n // tm,),
        in_specs=[
            pl.BlockSpec((tm, GMLP_WIDTH), lambda i: (i, COL_U // ublk)),
            pl.BlockSpec((tm, GMLP_WIDTH), lambda i: (i, COL_V // ublk)),
            pl.BlockSpec((1, GMLP_WIDTH), lambda i: (0, 0)),
            pl.BlockSpec((1, GMLP_WIDTH), lambda i: (0, 0)),
            pl.BlockSpec((GMLP_GROUPS, GMLP_CHUNK, GMLP_CHUNK), lambda i: (0, 0, 0)),
            pl.BlockSpec((GMLP_CHUNK, GMLP_WIDTH), lambda i: (0, 0)),
        ],
        out_specs=pl.BlockSpec((tm, GMLP_WIDTH), lambda i: (i, 0)),
        out_shape=jax.ShapeDtypeStruct((n, GMLP_WIDTH), BF16),
        compiler_params=_cparams("parallel"),
        name="gmlp",
    )(proj, proj, ln_g, ln_b, ws, bsx)


def _retention_kernel(q_ref, k_ref, v_ref, g_ref, dec_ref, zeta_ref, xi_ref, dch_ref, gn_ref,
                      o_ref, s_ref, st_ref, *, seq):
    c = RET_CHUNK
    nc = seq // c
    heads = [slice(hh * HEAD_DIM, (hh + 1) * HEAD_DIM) for hh in range(RET_HEADS)]

    def kv_body(ci, carry):
        rows = pl.ds(pl.multiple_of(ci * c, c), c)
        for hh, cols in enumerate(heads):
            kz = (k_ref[rows, cols].astype(F32) * zeta_ref[hh]).astype(BF16)
            st_ref[ci, hh] = lax.dot_general(kz, v_ref[rows, cols], (((0,), (0,)), ((), ())),
                                             preferred_element_type=F32)
        return carry

    lax.fori_loop(0, nc, kv_body, 0, unroll=2)

    s_ref[...] = jnp.zeros(s_ref.shape, F32)

    def scan_body(ci, carry):
        for hh in range(RET_HEADS):
            state = s_ref[hh]
            s_ref[hh] = dch_ref[hh] * state + st_ref[ci, hh]
            st_ref[ci, hh] = state
        return carry

    lax.fori_loop(0, nc, scan_body, 0)

    def out_body(ci, carry):
        rows = pl.ds(pl.multiple_of(ci * c, c), c)
        for hh, cols in enumerate(heads):
            q = q_ref[rows, cols]
            v = v_ref[rows, cols]
            scores = _nt_dot(q, k_ref[rows, cols]) * dec_ref[hh]
            y = _dot(scores.astype(BF16), v) + _dot(q, st_ref[ci, hh].astype(BF16)) * xi_ref[hh]
            mu = jnp.mean(y, axis=-1, keepdims=True)
            yc = y - mu
            var = jnp.mean(yc * yc, axis=-1, keepdims=True)
            yn = yc * lax.rsqrt(var + EPS) * gn_ref[:, cols]
            o_ref[rows, cols] = (jax.nn.silu(g_ref[rows, cols].astype(F32)) * yn).astype(o_ref.dtype)
        return carry

    lax.fori_loop(0, nc, out_body, 0, unroll=2)


def retention(proj, tabs, gn_g, batch, seq):
    n = proj.shape[0]
    dec, zeta, xi, dch = tabs
    rblk = RET_WIDTH // LANES
    kern = functools.partial(_retention_kernel, seq=seq)
    full = lambda shape: pl.BlockSpec(shape, lambda b: (0,) * len(shape))
    return pl.pallas_call(
        kern,
        grid=(batch,),
        in_specs=[
            pl.BlockSpec((seq, RET_WIDTH), lambda b: (b, COL_RQ // rblk)),
            pl.BlockSpec((seq, RET_WIDTH), lambda b: (b, COL_RK // rblk)),
            pl.BlockSpec((seq, RET_WIDTH), lambda b: (b, COL_RV // rblk)),
            pl.BlockSpec((seq, RET_WIDTH), lambda b: (b, COL_RG // rblk)),
            full(dec.shape), full(zeta.shape), full(xi.shape), full(dch.shape),
            full((1, RET_WIDTH)),
        ],
        out_specs=pl.BlockSpec((seq, RET_WIDTH), lambda b: (b, 0)),
        out_shape=jax.ShapeDtypeStruct((n, RET_WIDTH), BF16),
        scratch_shapes=[pltpu.VMEM((RET_HEADS, HEAD_DIM, HEAD_DIM), F32),
                        pltpu.VMEM((seq // RET_CHUNK, RET_HEADS, HEAD_DIM, HEAD_DIM), F32)],
        compiler_params=_cparams("parallel"),
        name="retention",
    )(proj, proj, proj, proj, dec, zeta, xi, dch, gn_g)


ROUTE_E1, ROUTE_E2, ROUTE_G1, ROUTE_G2 = 0, 1, 2, 3


def _top2_route(logits):
    lane = lax.broadcasted_iota(jnp.int32, logits.shape, 1)
    lg = jnp.where(lane < N_EXPERTS, logits, -jnp.inf)
    v1 = jnp.max(lg, axis=-1, keepdims=True)
    i1 = jnp.min(jnp.where(lg == v1, lane, LANES), axis=-1, keepdims=True)
    lg2 = jnp.where(lane == i1, -jnp.inf, lg)
    v2 = jnp.max(lg2, axis=-1, keepdims=True)
    i2 = jnp.min(jnp.where(lg2 == v2, lane, LANES), axis=-1, keepdims=True)
    e2 = jnp.exp(v2 - v1)
    den = 1.0 + e2
    out = jnp.where(lane == ROUTE_E1, i1.astype(F32), 0.0)
    out = jnp.where(lane == ROUTE_E2, i2.astype(F32), out)
    out = jnp.where(lane == ROUTE_G1, 1.0 / den, out)
    return jnp.where(lane == ROUTE_G2, e2 / den, out)


def _out_proj_kernel(*refs, router):
    if router:
        nsa_ref, gm_ref, ret_ref, h_ref, w_ref, g2_ref, wrh_ref, wrl_ref, br_ref, ho_ref, f_ref, rt_ref = refs
    else:
        nsa_ref, gm_ref, ret_ref, h_ref, w_ref, g2_ref, ho_ref, f_ref = refs
    acc = _dot(nsa_ref[...], w_ref[:NSA_WIDTH, :])
    acc = acc + _dot(gm_ref[...], w_ref[NSA_WIDTH:NSA_WIDTH + GMLP_WIDTH, :])
    acc = acc + _dot(ret_ref[...], w_ref[NSA_WIDTH + GMLP_WIDTH:, :])
    hn = h_ref[...] + acc
    ho_ref[...] = hn
    ms = jnp.mean(hn * hn, axis=-1, keepdims=True)
    f = hn * lax.rsqrt(ms + EPS) * g2_ref[...]
    f_hi = f.astype(BF16)
    if router:
        f_ref[...] = f
        f_lo = (f - f_hi.astype(F32)).astype(BF16)
        logits = (_dot(f_hi, wrh_ref[...]) + _dot(f_lo, wrh_ref[...]) + _dot(f_hi, wrl_ref[...])
                  + br_ref[...])
        rt_ref[...] = _top2_route(logits)
    else:
        f_ref[...] = f_hi


def out_proj(o_nsa, o_gm, o_ret, h2, w, g2, router_w=None, tm=512):
    n, d = h2.shape
    tm = min(tm, n)
    router = router_w is not None
    row = lambda width: pl.BlockSpec((tm, width), lambda i: (i, 0))
    full = lambda shape: pl.BlockSpec(shape, lambda i: (0,) * len(shape))
    in_specs = [row(NSA_WIDTH), row(GMLP_WIDTH), row(RET_WIDTH), row(d), full(w.shape), full((1, d))]
    args = [o_nsa, o_gm, o_ret, h2, w, g2]
    if router:
        in_specs += [full((d, LANES)), full((d, LANES)), full((1, LANES))]
        args += list(router_w)
        out_specs = [row(d), row(d), row(LANES)]
        out_shape = [jax.ShapeDtypeStruct((n, d), F32), jax.ShapeDtypeStruct((n, d), F32),
                     jax.ShapeDtypeStruct((n, LANES), F32)]
    else:
        out_specs = [row(d), row(d)]
        out_shape = [jax.ShapeDtypeStruct((n, d), F32), jax.ShapeDtypeStruct((n, d), BF16)]
    return pl.pallas_call(
        functools.partial(_out_proj_kernel, router=router),
        grid=(n // tm,),
        in_specs=in_specs,
        out_specs=out_specs,
        out_shape=out_shape,
        compiler_params=_cparams("parallel"),
        name="out_proj_router" if router else "out_proj",
    )(*args)


def _ffn_kernel(*refs, final):
    if final:
        x_ref, h_ref, w1_ref, w3_ref, w2_ref, fg_ref, o_ref = refs
    else:
        x_ref, h_ref, w1_ref, w3_ref, w2_ref, o_ref = refs
    k = pl.program_id(1)

    @pl.when(k == 0)
    def _():
        o_ref[...] = h_ref[...]

    x = x_ref[...]
    hid = jax.nn.silu(_dot(x, w1_ref[...])) * _dot(x, w3_ref[...])
    o_ref[...] += _dot(hid.astype(BF16), w2_ref[...])

    if final:
        @pl.when(k == pl.num_programs(1) - 1)
        def _():
            y = o_ref[...]
            ms = jnp.mean(y * y, axis=-1, keepdims=True)
            o_ref[...] = y * lax.rsqrt(ms + EPS) * fg_ref[...]


def ffn(f, h2, w1, w3, w2, final_g=None, tm=512, tf=512):
    n, d = h2.shape
    ff = w1.shape[1]
    tm = min(tm, n)
    tf = min(tf, ff)
    in_specs = [
        pl.BlockSpec((tm, d), lambda i, k: (i, 0)),
        pl.BlockSpec((tm, d), lambda i, k: (i, 0)),
        pl.BlockSpec((d, tf), lambda i, k: (0, k)),
        pl.BlockSpec((d, tf), lambda i, k: (0, k)),
        pl.BlockSpec((tf, d), lambda i, k: (k, 0)),
    ]
    args = [f, h2, w1, w3, w2]
    if final_g is not None:
        in_specs.append(pl.BlockSpec((1, d), lambda i, k: (0, 0)))
        args.append(final_g)
    return pl.pallas_call(
        functools.partial(_ffn_kernel, final=final_g is not None),
        grid=(n // tm, ff // tf),
        in_specs=in_specs,
        out_specs=pl.BlockSpec((tm, d), lambda i, k: (i, 0)),
        out_shape=jax.ShapeDtypeStruct((n, d), F32),
        compiler_params=_cparams("parallel", "arbitrary"),
        name="ffn",
    )(*args)


MOE_TM = 512


def _row_copy(src_ref, src_row, dst_ref, dst_row, sem):
    return pltpu.make_async_copy(src_ref.at[pl.ds(src_row, 1)], dst_ref.at[pl.ds(dst_row, 1)], sem)


def _moe_ffn_kernel(te_ref, nu_ref, src_ref, f_ref, w1_ref, w3_ref, w2_ref, o_ref, xin_ref, xb_ref, sem,
                    *, tm, chunk):
    del te_ref
    i = pl.program_id(0)
    k = pl.program_id(1)
    nt = pl.num_programs(0)
    nk = pl.num_programs(1)
    n_used = nu_ref[0]
    used = i < n_used
    slot = i % 2

    def start_chunk(tile, c, dst_slot):
        for u in range(chunk):
            row = c * chunk + u
            _row_copy(f_ref, src_ref[tile * tm + row], xin_ref.at[dst_slot], row, sem.at[dst_slot]).start()

    def wait_tile(dst_slot):
        def drain(c, carry):
            for u in range(chunk):
                _row_copy(f_ref, 0, xin_ref.at[dst_slot], 0, sem.at[dst_slot]).wait()
            return carry
        lax.fori_loop(0, nk, drain, 0)

    @pl.when((i == 0) & (k == 0))
    def _():
        def first(c, carry):
            start_chunk(0, c, 0)
            return carry
        lax.fori_loop(0, nk, first, 0)

    @pl.when((k == 0) & (i <= n_used))
    def _():
        wait_tile(slot)
        xb_ref[...] = xin_ref[slot, :tm, :].astype(BF16)

    @pl.when(k == 0)
    def _():
        o_ref[...] = jnp.zeros(o_ref.shape, F32)

    @pl.when(used)
    def _():
        start_chunk(i + 1, k, 1 - slot)
        x = xb_ref[...]
        hid = jax.nn.silu(_dot(x, w1_ref[0])) * _dot(x, w3_ref[0])
        o_ref[...] += _dot(hid.astype(BF16), w2_ref[0])

    @pl.when(used & (i == nt - 1) & (k == nk - 1))
    def _():
        wait_tile(1 - slot)


def moe_ffn(tile_expert, n_used, src, f, w1, w3, w2, rows, tm=MOE_TM, tf=256):
    n, d = f.shape
    ff = w1.shape[2]
    tf = min(tf, ff)
    nk = ff // tf
    chunk = pl.cdiv(tm, nk)
    buf_rows = ((chunk * nk + 7) // 8) * 8
    kk = lambda i, k, nu: jnp.where(i < nu[0], k, nk - 1)
    return pl.pallas_call(
        functools.partial(_moe_ffn_kernel, tm=tm, chunk=chunk),
        grid_spec=pltpu.PrefetchScalarGridSpec(
            num_scalar_prefetch=3,
            grid=(rows // tm, nk),
            in_specs=[
                pl.BlockSpec(memory_space=pl.ANY),
                pl.BlockSpec((1, d, tf), lambda i, k, te, nu, src: (te[i], 0, kk(i, k, nu))),
                pl.BlockSpec((1, d, tf), lambda i, k, te, nu, src: (te[i], 0, kk(i, k, nu))),
                pl.BlockSpec((1, tf, d), lambda i, k, te, nu, src: (te[i], kk(i, k, nu), 0)),
            ],
            out_specs=pl.BlockSpec((tm, d), lambda i, k, te, nu, src: (i, 0)),
            scratch_shapes=[pltpu.VMEM((2, buf_rows, d), F32), pltpu.VMEM((tm, d), BF16),
                            pltpu.SemaphoreType.DMA((2,))],
        ),
        out_shape=jax.ShapeDtypeStruct((rows, d), F32),
        compiler_params=_cparams("arbitrary", "arbitrary"),
        name="moe_ffn",
    )(tile_expert, n_used, src, f, w1, w3, w2)


def _combine_kernel(*refs, tm, final):
    if final:
        dest_ref, h_ref, rt_ref, yg_ref, fg_ref, o_ref, buf_ref, sem = refs
    else:
        dest_ref, h_ref, rt_ref, yg_ref, o_ref, buf_ref, sem = refs
    i = pl.program_id(0)
    nt = pl.num_programs(0)
    slot = i % 2

    def start_row(tile, j, dst_slot):
        for s in range(TOP_K):
            _row_copy(yg_ref, dest_ref[(tile * tm + j) * TOP_K + s], buf_ref.at[dst_slot, s], j,
                      sem.at[dst_slot]).start()

    def wait_tile(dst_slot):
        def drain(j, carry):
            for s in range(TOP_K):
                _row_copy(yg_ref, 0, buf_ref.at[dst_slot, s], 0, sem.at[dst_slot]).wait()
            return carry
        lax.fori_loop(0, tm, drain, 0, unroll=8)

    @pl.when(i == 0)
    def _():
        def first(j, carry):
            start_row(0, j, 0)
            return carry
        lax.fori_loop(0, tm, first, 0, unroll=8)

    wait_tile(slot)
    for j in range(tm):
        start_row(i + 1, j, 1 - slot)

    rt = rt_ref[...]
    y = (h_ref[...] + rt[:, ROUTE_G1:ROUTE_G1 + 1] * buf_ref[slot, 0]
         + rt[:, ROUTE_G2:ROUTE_G2 + 1] * buf_ref[slot, 1])
    if final:
        ms = jnp.mean(y * y, axis=-1, keepdims=True)
        y = y * lax.rsqrt(ms + EPS) * fg_ref[...]
    o_ref[...] = y

    @pl.when(i == nt - 1)
    def _():
        wait_tile(1 - slot)


def moe_combine(dest, h2, route, yg, final_g=None, tm=128):
    n, d = h2.shape
    tm = min(tm, n)
    dest = jnp.pad(dest, (0, tm * TOP_K))
    in_specs = [
        pl.BlockSpec((tm, d), lambda i, dest: (i, 0)),
        pl.BlockSpec((tm, LANES), lambda i, dest: (i, 0)),
        pl.BlockSpec(memory_space=pl.ANY),
    ]
    args = [dest, h2, route, yg]
    if final_g is not None:
        in_specs.append(pl.BlockSpec((1, d), lambda i, dest: (0, 0)))
        args.append(final_g)
    return pl.pallas_call(
        functools.partial(_combine_kernel, tm=tm, final=final_g is not None),
        grid_spec=pltpu.PrefetchScalarGridSpec(
            num_scalar_prefetch=1,
            grid=(n // tm,),
            in_specs=in_specs,
            out_specs=pl.BlockSpec((tm, d), lambda i, dest: (i, 0)),
            scratch_shapes=[pltpu.VMEM((2, TOP_K, tm, d), F32), pltpu.SemaphoreType.DMA((2,))],
        ),
        out_shape=jax.ShapeDtypeStruct((n, d), F32),
        compiler_params=_cparams("arbitrary"),
        name="moe_combine",
    )(*args)


def _routing_tables(route, tm):
    n = route.shape[0]
    flat_e = route[:, ROUTE_E1:ROUTE_E2 + 1].astype(jnp.int32).reshape(-1)
    onehot = (flat_e[:, None] == jnp.arange(N_EXPERTS, dtype=jnp.int32)[None, :]).astype(jnp.int32)
    csum = jnp.cumsum(onehot, axis=0)
    rank = jnp.sum(csum * onehot, axis=1) - 1
    counts = csum[-1]
    padded = ((counts + tm - 1) // tm) * tm
    off_end = jnp.cumsum(padded)
    dest = (off_end - padded)[flat_e] + rank
    n_tiles = (n * TOP_K) // tm + N_EXPERTS
    n_used = (off_end[-1] // tm).astype(jnp.int32)
    tile_start = jnp.arange(n_tiles, dtype=jnp.int32) * tm
    tile_start = jnp.minimum(tile_start, off_end[-1] - tm)
    tile_expert = jnp.sum((tile_start[:, None] >= off_end[None, :]).astype(jnp.int32), axis=1)
    rows = n_tiles * tm
    src = jnp.zeros((rows + tm + 8,), jnp.int32).at[dest].set(jnp.arange(n * TOP_K, dtype=jnp.int32) // TOP_K)
    return dest.astype(jnp.int32), src, tile_expert.astype(jnp.int32), n_used.reshape(1), rows


def moe_layer(h2, f, route, w1, w3, w2, final_g=None):
    dest, src, tile_expert, n_used, rows = _routing_tables(route, MOE_TM)
    yg = moe_ffn(tile_expert, n_used, src, f, w1, w3, w2, rows)
    return moe_combine(dest, h2, route, yg, final_g=final_g)


def _reorder_w_in(w_in):
    d = w_in.shape[0]
    kvw = NSA_KV_WIDTH
    o_gate = NSA_WIDTH + 6 * kvw
    o_uv = o_gate + 3 * NSA_HEADS
    o_ret = o_uv + 2 * GMLP_WIDTH
    gate = w_in[:, o_gate:o_uv].reshape(d, NSA_KV_HEADS, 3 * NSA_GROUP)
    gate = jnp.pad(gate, ((0, 0), (0, 0), (0, LANES - 3 * NSA_GROUP))).reshape(d, NSA_KV_HEADS * LANES)
    parts = [w_in[:, :o_gate], w_in[:, o_uv:], gate]
    w = jnp.concatenate(parts, axis=1)
    w = jnp.pad(w, ((0, 0), (0, PROJ_COLS - w.shape[1])))
    scale = np.ones((1, PROJ_COLS), np.float32)
    scale[:, COL_Q * LANES:(COL_Q + NSA_HEADS) * LANES] = HEAD_DIM ** -0.5
    scale[:, COL_RK * LANES:(COL_RK + RET_HEADS) * LANES] = HEAD_DIM ** -0.5
    del o_ret
    return w.astype(BF16), jnp.asarray(scale)


def _key_extra_lanes(seq):
    pos = np.arange(seq)
    kx = np.zeros((seq, LANES), np.float32)
    kx[pos, pos // SLC_LEN] = 1.0
    kx[:, XL_HI] = pos // SLC_LEN
    kx[:, XL_LO] = pos % SLC_LEN
    kx[:, XL_ONE_A] = 1.0
    kx[:, XL_ONE_B] = 1.0
    return jnp.asarray(kx, BF16)


def _share_t(seq):
    nrow = seq // CMP_STRIDE
    nb = seq // SLC_LEN
    c0 = np.arange(nrow)[None, :] * CMP_STRIDE
    s0 = np.arange(nb)[:, None] * SLC_LEN
    overlap = np.minimum(c0 + CMP_LEN, s0 + SLC_LEN) - np.maximum(c0, s0)
    share = np.clip(overlap, 0, CMP_LEN).astype(np.float32) / CMP_LEN
    share[:, nrow - 1] = 0.0
    return jnp.asarray(share, BF16)


def _retention_tables():
    hh = np.arange(RET_HEADS, dtype=np.float64)
    lg = np.log1p(-np.exp2(-5.0 - hh))
    nn = np.arange(RET_CHUNK, dtype=np.float64)
    rel = nn[:, None] - nn[None, :]
    dec = np.where(rel >= 0, np.exp(lg[:, None, None] * np.maximum(rel, 0.0)), 0.0)
    zeta = np.exp(lg[:, None] * (RET_CHUNK - 1.0 - nn))[:, :, None]
    xi = np.exp(lg[:, None] * (nn + 1.0))[:, :, None]
    dch = np.broadcast_to(np.exp(lg * RET_CHUNK)[:, None, None], (RET_HEADS, 1, HEAD_DIM))
    return tuple(jnp.asarray(a, F32) for a in (dec, zeta, xi, dch))


def _token_mixer(h2, batch, seq, ln1_g, w_in, pe_k, w1_k, w2_k, pe_v, w1_v, w2_v,
                 g_ln_g, g_ln_b, g_ws, g_bs, ret_gn_g, consts):
    kx, sht, ret_tabs, cmp_tabs, axq, wbias = consts
    w_r, colscale = _reorder_w_in(w_in)
    proj = in_proj(h2, ln1_g[None, :], w_r, colscale)
    nrow = seq // CMP_STRIDE
    kv = proj[:, COL_KCMP * LANES:(COL_VCMP + NSA_KV_HEADS) * LANES]
    kv = kv.reshape(batch, nrow, CMP_STRIDE, 2, NSA_KV_HEADS, HEAD_DIM)
    rows = kv.transpose(3, 0, 4, 1, 2, 5).reshape(2, batch * NSA_KV_HEADS, nrow, CMP_STRIDE * HEAD_DIM)
    pe = jnp.stack([pe_k, pe_v]).reshape(2, 1, CMP_LEN * HEAD_DIM)
    kvc = compress(rows, pe, jnp.stack([w1_k, w1_v]).astype(BF16), jnp.stack([w2_k, w2_v]).astype(BF16))
    o_cmp, xsel, counts = cmp_attn(proj, kvc, sht, cmp_tabs, batch, seq)
    o_slc = sel_attn(proj, xsel, counts, kx, axq, batch, seq)
    o_nsa = win_attn(proj, kx, axq, wbias, o_cmp, o_slc, batch, seq)
    bsx = jnp.repeat(g_bs.T, HEAD_DIM, axis=1)
    o_gm = gmlp(proj, g_ln_g[None, :], g_ln_b[None, :], g_ws, bsx)
    o_ret = retention(proj, ret_tabs, ret_gn_g[None, :], batch, seq)
    return o_nsa, o_gm, o_ret


def kernel(x, ln1_g, w_in, cmp_pe_k, cmp_w1_k, cmp_w2_k, cmp_pe_v, cmp_w1_v, cmp_w2_v, gmlp_ln_g, gmlp_ln_b, gmlp_ws, gmlp_bs, ret_gn_g, w_out, ln2_g, ffn_w1, ffn_w3, ffn_w2, moe_wr, moe_br, moe_w1, moe_w3, moe_w2, final_g):
    batch, seq, d = x.shape
    depth = w_in.shape[0]
    consts = (_key_extra_lanes(seq), _share_t(seq), _retention_tables(), _cmp_tables(seq),
              _alibi_q_table(seq), _window_bias(WIN_TQ))
    h2 = x.reshape(batch * seq, d)
    for layer in range(depth):
        o_nsa, o_gm, o_ret = _token_mixer(
            h2, batch, seq, ln1_g[layer], w_in[layer], cmp_pe_k[layer], cmp_w1_k[layer], cmp_w2_k[layer],
            cmp_pe_v[layer], cmp_w1_v[layer], cmp_w2_v[layer], gmlp_ln_g[layer], gmlp_ln_b[layer],
            gmlp_ws[layer], gmlp_bs[layer], ret_gn_g[layer], consts)
        fin = final_g[None, :] if layer == depth - 1 else None
        i = layer // 2
        w_o = w_out[layer].astype(BF16)
        if layer % 2 == 0:
            h2, f = out_proj(o_nsa, o_gm, o_ret, h2, w_o, ln2_g[layer][None, :])
            h2 = ffn(f, h2, ffn_w1[i].astype(BF16), ffn_w3[i].astype(BF16), ffn_w2[i].astype(BF16),
                     final_g=fin)
        else:
            wr = jnp.pad(moe_wr[i], ((0, 0), (0, LANES - N_EXPERTS)))
            wr_hi = wr.astype(BF16)
            wr_lo = (wr - wr_hi.astype(F32)).astype(BF16)
            br = jnp.pad(moe_br[i], (0, LANES - N_EXPERTS))[None, :]
            h2, f, route = out_proj(o_nsa, o_gm, o_ret, h2, w_o, ln2_g[layer][None, :],
                                    router_w=(wr_hi, wr_lo, br))
            h2 = moe_layer(h2, f, route, moe_w1[i].astype(BF16), moe_w3[i].astype(BF16),
                           moe_w2[i].astype(BF16), final_g=fin)
    return h2.reshape(batch, seq, d)
```

```python
import functools

import numpy as np
import jax
import jax.numpy as jnp
from jax import lax
from jax.experimental import pallas as pl
from jax.experimental.pallas import tpu as pltpu

F32 = jnp.float32
BF16 = jnp.bfloat16

HEAD_DIM = 128
NSA_HEADS = 8
NSA_KV_HEADS = 2
NSA_GROUP = NSA_HEADS // NSA_KV_HEADS
NSA_WIDTH = NSA_HEADS * HEAD_DIM
NSA_KV_WIDTH = NSA_KV_HEADS * HEAD_DIM
CMP_LEN = 32
CMP_STRIDE = 16
SLC_LEN = 64
SLC_TOPK = 16
WIN = 512
GMLP_GROUPS = 4
GMLP_CHUNK = 128
GMLP_WIDTH = GMLP_GROUPS * HEAD_DIM
RET_HEADS = 4
RET_CHUNK = 128
RET_WIDTH = RET_HEADS * HEAD_DIM
N_EXPERTS = 8
TOP_K = 2
EPS = 1e-6
NEG_INF = -1e30
FORCE_SCORE = 1e4

LANES = 128
VMEM_LIMIT = 56 * 1024 * 1024

COL_Q = 0
COL_KCMP = 8
COL_VCMP = 10
COL_KSLC = 12
COL_VSLC = 14
COL_KWIN = 16
COL_VWIN = 18
COL_U = 20
COL_V = 24
COL_RQ = 28
COL_RK = 32
COL_RV = 36
COL_RG = 40
COL_GATE = 44
PROJ_COLS = 48 * LANES

XL_HI = 64
XL_LO = 65
XL_ONE_A = 66
XL_ONE_B = 67


def _cparams(*sem):
    return pltpu.CompilerParams(dimension_semantics=sem, vmem_limit_bytes=VMEM_LIMIT)


def _nt_dot(a, b):
    return lax.dot_general(a, b, (((1,), (1,)), ((), ())), preferred_element_type=F32)


def _dot(a, b):
    return jnp.dot(a, b, preferred_element_type=F32)


def _in_proj_kernel(x_ref, g_ref, w_ref, cs_ref, o_ref, xn_ref):
    @pl.when(pl.program_id(1) == 0)
    def _():
        x = x_ref[...]
        ms = jnp.mean(x * x, axis=-1, keepdims=True)
        xn_ref[...] = (x * lax.rsqrt(ms + EPS) * g_ref[...]).astype(BF16)

    acc = _dot(xn_ref[...], w_ref[...])
    o_ref[...] = (acc * cs_ref[...]).astype(o_ref.dtype)


def in_proj(h2, g, w, colscale, tm=1024, tn=1024):
    n, d = h2.shape
    cols = w.shape[1]
    tm = min(tm, n)
    return pl.pallas_call(
        _in_proj_kernel,
        grid=(n // tm, cols // tn),
        in_specs=[
            pl.BlockSpec((tm, d), lambda i, j: (i, 0)),
            pl.BlockSpec((1, d), lambda i, j: (0, 0)),
            pl.BlockSpec((d, tn), lambda i, j: (0, j)),
            pl.BlockSpec((1, tn), lambda i, j: (0, j)),
        ],
        out_specs=pl.BlockSpec((tm, tn), lambda i, j: (i, j)),
        out_shape=jax.ShapeDtypeStruct((n, cols), BF16),
        scratch_shapes=[pltpu.VMEM((tm, d), BF16)],
        compiler_params=_cparams("parallel", "arbitrary"),
        name="in_proj",
    )(h2, g, w, colscale)


def _compress_kernel(r_ref, pe_ref, w1_ref, w2_ref, o_ref):
    half = CMP_STRIDE * HEAD_DIM
    r = r_ref[0, 0].astype(F32)
    nrow = r.shape[0]
    pe = pe_ref[0]
    a = _dot((r + pe[:, :half]).astype(BF16), w1_ref[0, :half, :])
    b = _dot((r + pe[:, half:]).astype(BF16), w1_ref[0, half:, :])
    pre = a + pltpu.roll(b, nrow - 1, 0)
    y = _dot(jax.nn.gelu(pre).astype(BF16), w2_ref[0])
    row = lax.broadcasted_iota(jnp.int32, y.shape, 0)
    o_ref[0, 0] = jnp.where(row < nrow - 1, y, 0.0).astype(o_ref.dtype)


def compress(rows, pe, w1, w2):
    _, bg, nrow, width = rows.shape
    return pl.pallas_call(
        _compress_kernel,
        grid=(2, bg),
        in_specs=[
            pl.BlockSpec((1, 1, nrow, width), lambda s, i: (s, i, 0, 0)),
            pl.BlockSpec((1, 1, 2 * width), lambda s, i: (s, 0, 0)),
            pl.BlockSpec((1, 2 * width, HEAD_DIM), lambda s, i: (s, 0, 0)),
            pl.BlockSpec((1, HEAD_DIM, HEAD_DIM), lambda s, i: (s, 0, 0)),
        ],
        out_specs=pl.BlockSpec((1, 1, nrow, HEAD_DIM), lambda s, i: (s, i, 0, 0)),
        out_shape=jax.ShapeDtypeStruct((2, bg, nrow, HEAD_DIM), BF16),
        compiler_params=_cparams("arbitrary", "arbitrary"),
        name="compress",
    )(rows, pe, w1, w2)


def _group_slope(g, r):
    return jnp.where(g == 0, F32(2.0 ** -(r + 1)), F32(2.0 ** -(r + 1 + NSA_GROUP)))


SEL_TQ = 256
SEL_TK = 512
SEL_TILE_ROWS = 8


def _cmp_attn_kernel(q_ref, kc_ref, vc_ref, sht_ref, dist_ref, cb_ref, grp_ref, o_ref, x_ref, c_ref,
                     *, tq, nb, topk):
    g = pl.program_id(0) % NSA_KV_HEADS
    t0 = pl.program_id(1) * tq
    kc = kc_ref[0, 0]
    vc = vc_ref[0, 0]
    nrow = kc.shape[0]
    distf = dist_ref[...]
    cbias = cb_ref[...]
    row_ok = jnp.where(t0 + lax.broadcasted_iota(jnp.int32, (tq, 1), 0) >= CMP_LEN - 1, 1.0, 0.0)
    psum = jnp.zeros((tq, nrow), F32)
    for r in range(NSA_GROUP):
        q = q_ref[:, r * HEAD_DIM:(r + 1) * HEAD_DIM]
        s = _nt_dot(q, kc) - _group_slope(g, r) * distf + cbias
        e = jnp.exp(s - jnp.max(s, axis=-1, keepdims=True))
        p = e * (row_ok / jnp.sum(e, axis=-1, keepdims=True))
        o_ref[:, r * HEAD_DIM:(r + 1) * HEAD_DIM] = _dot(p.astype(BF16), vc).astype(o_ref.dtype)
        psum = psum + p
    p_hi = psum.astype(BF16)
    p_lo = (psum - p_hi.astype(F32)).astype(BF16)
    sht = sht_ref[...]
    imp = _nt_dot(sht, p_hi) + _nt_dot(sht, p_lo)
    j = lax.broadcasted_iota(jnp.int32, (nb, tq), 0)
    cur = (t0 + lax.broadcasted_iota(jnp.int32, (nb, tq), 1)) // SLC_LEN
    forced = (j == 0) | (j == cur) | (j == cur - 1)
    score = jnp.where(forced, FORCE_SCORE, imp)
    cand = jnp.where(j > cur, -3e38, score)
    jf = j.astype(F32)[:, :LANES]
    groups = [cand[:, c:c + LANES] for c in range(0, tq, LANES)]
    for _ in range(topk):
        for n, cg in enumerate(groups):
            best = jnp.max(cg, axis=0, keepdims=True)
            idx = jnp.min(jnp.where(cg == best, jf, float(nb)), axis=0, keepdims=True)
            groups[n] = jnp.where(jf == idx, -jnp.inf, cg)
    cand = jnp.concatenate(groups, axis=1)
    picked = cand == -jnp.inf
    grp = grp_ref[...]
    cnt = _dot(grp, jnp.where(picked, 1.0, 0.0).astype(BF16))
    c_ref[0] = jnp.broadcast_to(jnp.sum(cnt, axis=1, keepdims=True), c_ref.shape[1:])
    neg = jnp.where(picked, 0.0, NEG_INF)
    neg = jnp.concatenate([neg, jnp.zeros((LANES - nb, tq), F32)], axis=0)
    x_ref[...] = neg.T.astype(x_ref.dtype)


def _cmp_tables(seq):
    nrow = seq // CMP_STRIDE
    t = np.arange(seq)[:, None]
    c = np.arange(nrow)[None, :]
    dist = t - (c * CMP_STRIDE + (CMP_LEN - 1))
    valid = (dist >= 0) & (c < nrow - 1)
    return jnp.asarray(dist, F32), jnp.asarray(np.where(valid, 0.0, NEG_INF), F32)


def cmp_attn(proj, kvc, sht, tables, batch, seq, tq=256):
    n = proj.shape[0]
    nb = seq // SLC_LEN
    bg = batch * NSA_KV_HEADS
    nrow = kvc.shape[2]
    qt = seq // tq
    rowblk = lambda i, t: ((i // NSA_KV_HEADS) * qt + t, i % NSA_KV_HEADS)
    kern = functools.partial(_cmp_attn_kernel, tq=tq, nb=nb, topk=min(SLC_TOPK, nb))
    return pl.pallas_call(
        kern,
        grid=(bg, qt),
        in_specs=[
            pl.BlockSpec((tq, NSA_GROUP * HEAD_DIM), rowblk),
            pl.BlockSpec((1, 1, nrow, HEAD_DIM), lambda i, t: (0, i, 0, 0)),
            pl.BlockSpec((1, 1, nrow, HEAD_DIM), lambda i, t: (1, i, 0, 0)),
            pl.BlockSpec((nb, nrow), lambda i, t: (0, 0)),
            pl.BlockSpec((tq, nrow), lambda i, t: (t, 0)),
            pl.BlockSpec((tq, nrow), lambda i, t: (t, 0)),
            pl.BlockSpec((SEL_TILE_ROWS, nb), lambda i, t: (0, 0)),
        ],
        out_specs=[
            pl.BlockSpec((tq, NSA_GROUP * HEAD_DIM), rowblk),
            pl.BlockSpec((tq, LANES), rowblk),
            pl.BlockSpec((1, SEL_TILE_ROWS, LANES), lambda i, t: (i * qt + t, 0, 0)),
        ],
        out_shape=[
            jax.ShapeDtypeStruct((n, NSA_WIDTH), BF16),
            jax.ShapeDtypeStruct((n, NSA_KV_HEADS * LANES), BF16),
            jax.ShapeDtypeStruct((bg * qt, SEL_TILE_ROWS, LANES), F32),
        ],
        compiler_params=_cparams("parallel", "parallel"),
        name="cmp_attn",
    )(proj, kvc, kvc, sht, *tables, _tile_groups(seq))


def _tile_groups(seq):
    nb = seq // SLC_LEN
    assert seq // SEL_TK <= SEL_TILE_ROWS
    grp = np.zeros((SEL_TILE_ROWS, nb), np.float32)
    grp[np.arange(nb) // (SEL_TK // SLC_LEN), np.arange(nb)] = 1.0
    return jnp.asarray(grp, BF16)


def _active_key_tiles(counts, batch, seq, tq, tk):
    qt = seq // tq
    n_full = ((jnp.arange(qt, dtype=jnp.int32) * tq + tq - 1) // tk)[None, :, None]
    kt = jnp.arange(SEL_TILE_ROWS, dtype=jnp.int32)[None, None, :]
    cnt = counts[:, :, 0].reshape(batch * NSA_KV_HEADS, qt, SEL_TILE_ROWS)
    active = (cnt > 0) & (kt < n_full)
    order = jnp.argsort(jnp.where(active, 0, 1), axis=-1, stable=True).astype(jnp.int32)
    return order.reshape(-1), jnp.sum(active, axis=-1).astype(jnp.int32).reshape(-1)


def _alibi_q_table(seq):
    t = np.arange(seq)
    tab = np.zeros((NSA_HEADS, seq, LANES), np.float32)
    for h in range(NSA_HEADS):
        slope = 2.0 ** (-8.0 * (h + 1) / NSA_HEADS)
        tab[h, :, XL_HI] = slope * SLC_LEN
        tab[h, :, XL_LO] = slope
        tab[h, :, XL_ONE_A] = -(slope * SLC_LEN) * (t // SLC_LEN)
        tab[h, :, XL_ONE_B] = -slope * (t % SLC_LEN)
    return jnp.asarray(tab, BF16)


def _sel_attn_kernel(act_ref, nact_ref, q_ref, x_ref, ax_ref, k_ref, v_ref, kx_ref, o_ref,
                     q2_ref, sa_ref, sb_ref, m_ref, acc_ref, *, tq, tk):
    t0 = pl.program_id(1) * tq
    n_kt = (t0 + tq - 1) // tk + 1
    rows = NSA_GROUP * tq
    half = rows // 2
    tpos = t0 + (lax.broadcasted_iota(jnp.int32, (half, tk), 0) & (tq - 1))
    kcol = lax.broadcasted_iota(jnp.int32, (half, tk), 1)

    xsel = x_ref[...]
    q2_ref[...] = jnp.concatenate([
        jnp.concatenate([q_ref[:, r * HEAD_DIM:(r + 1) * HEAD_DIM], xsel + ax_ref[r]], axis=1)
        for r in range(NSA_GROUP)], axis=0)
    m_ref[...] = jnp.full(m_ref.shape, -jnp.inf, F32)
    acc_ref[...] = jnp.zeros(acc_ref.shape, F32)
    ones = jnp.ones((tk, LANES), BF16)

    def scores(kt, dst_ref):
        ks = pl.multiple_of(kt * tk, tk)
        k2 = jnp.concatenate([k_ref[pl.ds(ks, tk), :], kx_ref[pl.ds(ks, tk), :]], axis=1)
        dst_ref[...] = _nt_dot(q2_ref[...], k2)

    def update(kt, src_ref, causal):
        ks = pl.multiple_of(kt * tk, tk)
        v2 = jnp.concatenate([v_ref[pl.ds(ks, tk), :], ones], axis=1)
        step = half if causal else rows
        for r0 in range(0, rows, step):
            rs = slice(r0, r0 + step)
            s = src_ref[rs, :]
            if causal:
                s = jnp.where(ks + kcol <= tpos, s, NEG_INF)
            m_prev = m_ref[rs, :]
            m_new = jnp.maximum(m_prev, jnp.max(s, axis=-1, keepdims=True))
            alpha = jnp.exp(m_prev - m_new)
            p = jnp.exp(s - jnp.concatenate([m_new] * (tk // LANES), axis=1))
            acc_ref[rs, :] = (jnp.concatenate([alpha, alpha], axis=1) * acc_ref[rs, :]
                              + _dot(p.astype(BF16), v2))
            m_ref[rs, :] = m_new

    step_id = pl.program_id(0) * pl.num_programs(1) + pl.program_id(1)
    n_act = nact_ref[step_id]
    diag = n_kt - 1

    def visited(j):
        return jnp.where(j < n_act, act_ref[step_id * SEL_TILE_ROWS + jnp.minimum(j, SEL_TILE_ROWS - 1)], diag)

    scores(visited(0), sa_ref)

    def body(j, carry):
        scores(visited(2 * j + 1), sb_ref)
        update(visited(2 * j), sa_ref, False)
        scores(visited(2 * j + 2), sa_ref)
        update(visited(2 * j + 1), sb_ref, False)
        return carry

    lax.fori_loop(0, n_act // 2, body, 0)

    @pl.when(n_act % 2 == 1)
    def _():
        scores(diag, sb_ref)
        update(visited(n_act - 1), sa_ref, False)
        update(diag, sb_ref, True)

    @pl.when(n_act % 2 == 0)
    def _():
        update(diag, sa_ref, True)

    acc = acc_ref[...]
    o = acc[:, :HEAD_DIM] / acc[:, HEAD_DIM:]
    for r in range(NSA_GROUP):
        o_ref[:, r * HEAD_DIM:(r + 1) * HEAD_DIM] = o[r * tq:(r + 1) * tq].astype(o_ref.dtype)


def sel_attn(proj, xsel, counts, kx, axq, batch, seq, tq=SEL_TQ, tk=SEL_TK):
    n = proj.shape[0]
    bg = batch * NSA_KV_HEADS
    qt = seq // tq
    act, n_act = _active_key_tiles(counts, batch, seq, tq, tk)
    rowblk = lambda i, t, a, na: ((i // NSA_KV_HEADS) * qt + t, i % NSA_KV_HEADS)
    kern = functools.partial(_sel_attn_kernel, tq=tq, tk=tk)
    return pl.pallas_call(
        kern,
        grid_spec=pltpu.PrefetchScalarGridSpec(
            num_scalar_prefetch=2,
            grid=(bg, qt),
            in_specs=[
                pl.BlockSpec((tq, NSA_GROUP * HEAD_DIM), rowblk),
                pl.BlockSpec((tq, LANES), rowblk),
                pl.BlockSpec((NSA_GROUP, tq, LANES), lambda i, t, a, na: (i % NSA_KV_HEADS, t, 0)),
                pl.BlockSpec((seq, HEAD_DIM),
                             lambda i, t, a, na: (i // NSA_KV_HEADS, COL_KSLC + i % NSA_KV_HEADS)),
                pl.BlockSpec((seq, HEAD_DIM),
                             lambda i, t, a, na: (i // NSA_KV_HEADS, COL_VSLC + i % NSA_KV_HEADS)),
                pl.BlockSpec((seq, LANES), lambda i, t, a, na: (0, 0)),
            ],
            out_specs=pl.BlockSpec((tq, NSA_GROUP * HEAD_DIM), rowblk),
            scratch_shapes=[
                pltpu.VMEM((NSA_GROUP * tq, 2 * HEAD_DIM), BF16),
                pltpu.VMEM((NSA_GROUP * tq, tk), F32),
                pltpu.VMEM((NSA_GROUP * tq, tk), F32),
                pltpu.VMEM((NSA_GROUP * tq, LANES), F32),
                pltpu.VMEM((NSA_GROUP * tq, 2 * HEAD_DIM), F32),
            ],
        ),
        out_shape=jax.ShapeDtypeStruct((n, NSA_WIDTH), BF16),
        compiler_params=_cparams("parallel", "parallel"),
        name="sel_attn",
    )(act, n_act, proj, xsel, axq, proj, proj, kx)


def _win_attn_kernel(q_ref, ax_ref, wb_ref, k_ref, v_ref, kx_ref, gl_ref, oc_ref, os_ref, o_ref, *, tq):
    t0 = pl.program_id(1) * tq
    span = WIN + tq
    starts = [pl.multiple_of(jnp.maximum(t0 - WIN + c * tq, 0), LANES) for c in range(span // tq)]
    k2 = jnp.concatenate([
        jnp.concatenate([k_ref[pl.ds(st, tq), :], kx_ref[pl.ds(st, tq), :]], axis=1)
        for st in starts], axis=0)
    v2 = jnp.concatenate([
        jnp.concatenate([v_ref[pl.ds(st, tq), :] for st in starts], axis=0),
        jnp.ones((span, LANES), BF16)], axis=1)
    bias = wb_ref[0]
    gates = jax.nn.sigmoid(gl_ref[...].astype(F32))
    pair = NSA_GROUP // 2
    for h0 in range(0, NSA_GROUP, pair):
        heads = range(h0, h0 + pair)
        q2 = jnp.concatenate([
            jnp.concatenate([q_ref[:, r * HEAD_DIM:(r + 1) * HEAD_DIM], ax_ref[r]], axis=1)
            for r in heads], axis=0)
        s = _nt_dot(q2, k2)
        parts = []
        for n, r in enumerate(heads):
            sr = s[n * tq:(n + 1) * tq] + bias
            parts.append(jnp.exp(sr - jnp.max(sr, axis=-1, keepdims=True)).astype(BF16))
        ov = _dot(jnp.concatenate(parts, axis=0), v2)
        o_win = ov[:, :HEAD_DIM] / ov[:, HEAD_DIM:]
        for n, r in enumerate(heads):
            cols = slice(r * HEAD_DIM, (r + 1) * HEAD_DIM)
            o = (gates[:, 3 * r:3 * r + 1] * oc_ref[:, cols].astype(F32)
                 + gates[:, 3 * r + 1:3 * r + 2] * os_ref[:, cols].astype(F32)
                 + gates[:, 3 * r + 2:3 * r + 3] * o_win[n * tq:(n + 1) * tq])
            o_ref[:, cols] = o.astype(o_ref.dtype)


WIN_TQ = 256


def _window_bias(tq):
    nvar = WIN // tq + 1
    r = np.arange(tq)[None, :, None]
    c = np.arange(WIN + tq)[None, None, :]
    t0 = (np.arange(nvar) * tq)[:, None, None]
    kpos = t0 - WIN + c
    dist = (t0 + r) - kpos
    ok = (kpos >= 0) & (dist >= 0) & (dist < WIN)
    return jnp.asarray(np.where(ok, 0.0, NEG_INF), F32)


def win_attn(proj, kx, axq, wbias, o_cmp, o_slc, batch, seq, tq=WIN_TQ):
    n = proj.shape[0]
    bg = batch * NSA_KV_HEADS
    qt = seq // tq
    nvar = wbias.shape[0]
    rowblk = lambda i, t: ((i // NSA_KV_HEADS) * qt + t, i % NSA_KV_HEADS)
    kern = functools.partial(_win_attn_kernel, tq=tq)
    return pl.pallas_call(
        kern,
        grid=(bg, qt),
        in_specs=[
            pl.BlockSpec((tq, NSA_GROUP * HEAD_DIM), rowblk),
            pl.BlockSpec((NSA_GROUP, tq, LANES), lambda i, t: (i % NSA_KV_HEADS, t, 0)),
            pl.BlockSpec((1, tq, WIN + tq), lambda i, t: (jnp.minimum(t, nvar - 1), 0, 0)),
            pl.BlockSpec((seq, HEAD_DIM), lambda i, t: (i // NSA_KV_HEADS, COL_KWIN + i % NSA_KV_HEADS)),
            pl.BlockSpec((seq, HEAD_DIM), lambda i, t: (i // NSA_KV_HEADS, COL_VWIN + i % NSA_KV_HEADS)),
            pl.BlockSpec((seq, LANES), lambda i, t: (0, 0)),
            pl.BlockSpec((tq, LANES), lambda i, t: ((i // NSA_KV_HEADS) * qt + t, COL_GATE + i % NSA_KV_HEADS)),
            pl.BlockSpec((tq, NSA_GROUP * HEAD_DIM), rowblk),
            pl.BlockSpec((tq, NSA_GROUP * HEAD_DIM), rowblk),
        ],
        out_specs=pl.BlockSpec((tq, NSA_GROUP * HEAD_DIM), rowblk),
        out_shape=jax.ShapeDtypeStruct((n, NSA_WIDTH), BF16),
        compiler_params=_cparams("parallel", "parallel"),
        name="win_attn",
    )(proj, axq, wbias, proj, proj, kx, proj, o_cmp, o_slc)


def _gmlp_kernel(u_ref, v_ref, lg_ref, lb_ref, ws_ref, bsx_ref, o_ref, *, tm):
    u = jax.nn.gelu(u_ref[...].astype(F32))
    v = jax.nn.gelu(v_ref[...].astype(F32))
    mu = jnp.mean(v, axis=-1, keepdims=True)
    vc = v - mu
    var = jnp.mean(vc * vc, axis=-1, keepdims=True)
    vn = (vc * lax.rsqrt(var + EPS) * lg_ref[...] + lb_ref[...]).astype(BF16)
    c = GMLP_CHUNK
    tri = (lax.broadcasted_iota(jnp.int32, (c, c), 0) >= lax.broadcasted_iota(jnp.int32, (c, c), 1))
    bsx = bsx_ref[...]
    for gi in range(GMLP_GROUPS):
        w = jnp.where(tri, ws_ref[gi], 0.0).astype(BF16)
        cols = slice(gi * HEAD_DIM, (gi + 1) * HEAD_DIM)
        for ci in range(tm // c):
            rows = slice(ci * c, (ci + 1) * c)
            s = _dot(w, vn[rows, cols]) + bsx[:, cols]
            o_ref[rows, cols] = (u[rows, cols] * s).astype(o_ref.dtype)


def gmlp(proj, ln_g, ln_b, ws, bsx, tm=512):
    n = proj.shape[0]
    ublk = GMLP_WIDTH // LANES
    kern = functools.partial(_gmlp_kernel, tm=tm)
    return pl.pallas_call(
        kern,
        grid=(n // tm,),
        in_specs=[
            pl.BlockSpec((tm, GMLP_WIDTH), lambda i: (i, COL_U // ublk)),
            pl.BlockSpec((tm, GMLP_WIDTH), lambda i: (i, COL_V // ublk)),
            pl.BlockSpec((1, GMLP_WIDTH), lambda i: (0, 0)),
            pl.BlockSpec((1, GMLP_WIDTH), lambda i: (0, 0)),
            pl.BlockSpec((GMLP_GROUPS, GMLP_CHUNK, GMLP_CHUNK), lambda i: (0, 0, 0)),
            pl.BlockSpec((GMLP_CHUNK, GMLP_WIDTH), lambda i: (0, 0)),
        ],
        out_specs=pl.BlockSpec((tm, GMLP_WIDTH), lambda i: (i, 0)),
        out_shape=jax.ShapeDtypeStruct((n, GMLP_WIDTH), BF16),
        compiler_params=_cparams("parallel"),
        name="gmlp",
    )(proj, proj, ln_g, ln_b, ws, bsx)


def _retention_kernel(q_ref, k_ref, v_ref, g_ref, dec_ref, zeta_ref, xi_ref, dch_ref, gn_ref,
                      o_ref, s_ref, st_ref, *, seq):
    c = RET_CHUNK
    nc = seq // c
    heads = [slice(hh * HEAD_DIM, (hh + 1) * HEAD_DIM) for hh in range(RET_HEADS)]

    def kv_body(ci, carry):
        rows = pl.ds(pl.multiple_of(ci * c, c), c)
        for hh, cols in enumerate(heads):
            kz = (k_ref[rows, cols].astype(F32) * zeta_ref[hh]).astype(BF16)
            st_ref[ci, hh] = lax.dot_general(kz, v_ref[rows, cols], (((0,), (0,)), ((), ())),
                                             preferred_element_type=F32)
        return carry

    lax.fori_loop(0, nc, kv_body, 0, unroll=2)

    s_ref[...] = jnp.zeros(s_ref.shape, F32)

    def scan_body(ci, carry):
        for hh in range(RET_HEADS):
            state = s_ref[hh]
            s_ref[hh] = dch_ref[hh] * state + st_ref[ci, hh]
            st_ref[ci, hh] = state
        return carry

    lax.fori_loop(0, nc, scan_body, 0)

    def out_body(ci, carry):
        rows = pl.ds(pl.multiple_of(ci * c, c), c)
        for hh, cols in enumerate(heads):
            q = q_ref[rows, cols]
            v = v_ref[rows, cols]
            scores = _nt_dot(q, k_ref[rows, cols]) * dec_ref[hh]
            y = _dot(scores.astype(BF16), v) + _dot(q, st_ref[ci, hh].astype(BF16)) * xi_ref[hh]
            mu = jnp.mean(y, axis=-1, keepdims=True)
            yc = y - mu
            var = jnp.mean(yc * yc, axis=-1, keepdims=True)
            yn = yc * lax.rsqrt(var + EPS) * gn_ref[:, cols]
            o_ref[rows, cols] = (jax.nn.silu(g_ref[rows, cols].astype(F32)) * yn).astype(o_ref.dtype)
        return carry

    lax.fori_loop(0, nc, out_body, 0, unroll=2)


def retention(proj, tabs, gn_g, batch, seq):
    n = proj.shape[0]
    dec, zeta, xi, dch = tabs
    rblk = RET_WIDTH // LANES
    kern = functools.partial(_retention_kernel, seq=seq)
    full = lambda shape: pl.BlockSpec(shape, lambda b: (0,) * len(shape))
    return pl.pallas_call(
        kern,
        grid=(batch,),
        in_specs=[
            pl.BlockSpec((seq, RET_WIDTH), lambda b: (b, COL_RQ // rblk)),
            pl.BlockSpec((seq, RET_WIDTH), lambda b: (b, COL_RK // rblk)),
            pl.BlockSpec((seq, RET_WIDTH), lambda b: (b, COL_RV // rblk)),
            pl.BlockSpec((seq, RET_WIDTH), lambda b: (b, COL_RG // rblk)),
            full(dec.shape), full(zeta.shape), full(xi.shape), full(dch.shape),
            full((1, RET_WIDTH)),
        ],
        out_specs=pl.BlockSpec((seq, RET_WIDTH), lambda b: (b, 0)),
        out_shape=jax.ShapeDtypeStruct((n, RET_WIDTH), BF16),
        scratch_shapes=[pltpu.VMEM((RET_HEADS, HEAD_DIM, HEAD_DIM), F32),
                        pltpu.VMEM((seq // RET_CHUNK, RET_HEADS, HEAD_DIM, HEAD_DIM), F32)],
        compiler_params=_cparams("parallel"),
        name="retention",
    )(proj, proj, proj, proj, dec, zeta, xi, dch, gn_g)


ROUTE_E1, ROUTE_E2, ROUTE_G1, ROUTE_G2 = 0, 1, 2, 3


def _top2_route(logits):
    lane = lax.broadcasted_iota(jnp.int32, logits.shape, 1)
    lg = jnp.where(lane < N_EXPERTS, logits, -jnp.inf)
    v1 = jnp.max(lg, axis=-1, keepdims=True)
    i1 = jnp.min(jnp.where(lg == v1, lane, LANES), axis=-1, keepdims=True)
    lg2 = jnp.where(lane == i1, -jnp.inf, lg)
    v2 = jnp.max(lg2, axis=-1, keepdims=True)
    i2 = jnp.min(jnp.where(lg2 == v2, lane, LANES), axis=-1, keepdims=True)
    e2 = jnp.exp(v2 - v1)
    den = 1.0 + e2
    out = jnp.where(lane == ROUTE_E1, i1.astype(F32), 0.0)
    out = jnp.where(lane == ROUTE_E2, i2.astype(F32), out)
    out = jnp.where(lane == ROUTE_G1, 1.0 / den, out)
    return jnp.where(lane == ROUTE_G2, e2 / den, out)


def _out_proj_kernel(*refs, router):
    if router:
        nsa_ref, gm_ref, ret_ref, h_ref, w_ref, g2_ref, wrh_ref, wrl_ref, br_ref, ho_ref, f_ref, rt_ref = refs
    else:
        nsa_ref, gm_ref, ret_ref, h_ref, w_ref, g2_ref, ho_ref, f_ref = refs
    acc = _dot(nsa_ref[...], w_ref[:NSA_WIDTH, :])
    acc = acc + _dot(gm_ref[...], w_ref[NSA_WIDTH:NSA_WIDTH + GMLP_WIDTH, :])
    acc = acc + _dot(ret_ref[...], w_ref[NSA_WIDTH + GMLP_WIDTH:, :])
    hn = h_ref[...] + acc
    ho_ref[...] = hn
    ms = jnp.mean(hn * hn, axis=-1, keepdims=True)
    f = hn * lax.rsqrt(ms + EPS) * g2_ref[...]
    f_hi = f.astype(BF16)
    if router:
        f_ref[...] = f
        f_lo = (f - f_hi.astype(F32)).astype(BF16)
        logits = (_dot(f_hi, wrh_ref[...]) + _dot(f_lo, wrh_ref[...]) + _dot(f_hi, wrl_ref[...])
                  + br_ref[...])
        rt_ref[...] = _top2_route(logits)
    else:
        f_ref[...] = f_hi


def out_proj(o_nsa, o_gm, o_ret, h2, w, g2, router_w=None, tm=512):
    n, d = h2.shape
    tm = min(tm, n)
    router = router_w is not None
    row = lambda width: pl.BlockSpec((tm, width), lambda i: (i, 0))
    full = lambda shape: pl.BlockSpec(shape, lambda i: (0,) * len(shape))
    in_specs = [row(NSA_WIDTH), row(GMLP_WIDTH), row(RET_WIDTH), row(d), full(w.shape), full((1, d))]
    args = [o_nsa, o_gm, o_ret, h2, w, g2]
    if router:
        in_specs += [full((d, LANES)), full((d, LANES)), full((1, LANES))]
        args += list(router_w)
        out_specs = [row(d), row(d), row(LANES)]
        out_shape = [jax.ShapeDtypeStruct((n, d), F32), jax.ShapeDtypeStruct((n, d), F32),
                     jax.ShapeDtypeStruct((n, LANES), F32)]
    else:
        out_specs = [row(d), row(d)]
        out_shape = [jax.ShapeDtypeStruct((n, d), F32), jax.ShapeDtypeStruct((n, d), BF16)]
    return pl.pallas_call(
        functools.partial(_out_proj_kernel, router=router),
        grid=(n // tm,),
        in_specs=in_specs,
        out_specs=out_specs,
        out_shape=out_shape,
        compiler_params=_cparams("parallel"),
        name="out_proj_router" if router else "out_proj",
    )(*args)


def _ffn_kernel(*refs, final):
    if final:
        x_ref, h_ref, w1_ref, w3_ref, w2_ref, fg_ref, o_ref = refs
    else:
        x_ref, h_ref, w1_ref, w3_ref, w2_ref, o_ref = refs
    k = pl.program_id(1)

    @pl.when(k == 0)
    def _():
        o_ref[...] = h_ref[...]

    x = x_ref[...]
    hid = jax.nn.silu(_dot(x, w1_ref[...])) * _dot(x, w3_ref[...])
    o_ref[...] += _dot(hid.astype(BF16), w2_ref[...])

    if final:
        @pl.when(k == pl.num_programs(1) - 1)
        def _():
            y = o_ref[...]
            ms = jnp.mean(y * y, axis=-1, keepdims=True)
            o_ref[...] = y * lax.rsqrt(ms + EPS) * fg_ref[...]


def ffn(f, h2, w1, w3, w2, final_g=None, tm=512, tf=512):
    n, d = h2.shape
    ff = w1.shape[1]
    tm = min(tm, n)
    tf = min(tf, ff)
    in_specs = [
        pl.BlockSpec((tm, d), lambda i, k: (i, 0)),
        pl.BlockSpec((tm, d), lambda i, k: (i, 0)),
        pl.BlockSpec((d, tf), lambda i, k: (0, k)),
        pl.BlockSpec((d, tf), lambda i, k: (0, k)),
        pl.BlockSpec((tf, d), lambda i, k: (k, 0)),
    ]
    args = [f, h2, w1, w3, w2]
    if final_g is not None:
        in_specs.append(pl.BlockSpec((1, d), lambda i, k: (0, 0)))
        args.append(final_g)
    return pl.pallas_call(
        functools.partial(_ffn_kernel, final=final_g is not None),
        grid=(n // tm, ff // tf),
        in_specs=in_specs,
        out_specs=pl.BlockSpec((tm, d), lambda i, k: (i, 0)),
        out_shape=jax.ShapeDtypeStruct((n, d), F32),
        compiler_params=_cparams("parallel", "arbitrary"),
        name="ffn",
    )(*args)


MOE_TM = 512


def _row_copy(src_ref, src_row, dst_ref, dst_row, sem):
    return pltpu.make_async_copy(src_ref.at[pl.ds(src_row, 1)], dst_ref.at[pl.ds(dst_row, 1)], sem)


def _moe_ffn_kernel(te_ref, nu_ref, src_ref, f_ref, w1_ref, w3_ref, w2_ref, o_ref, xin_ref, xb_ref, sem,
                    *, tm, chunk):
    del te_ref
    i = pl.program_id(0)
    k = pl.program_id(1)
    nt = pl.num_programs(0)
    nk = pl.num_programs(1)
    n_used = nu_ref[0]
    used = i < n_used
    slot = i % 2

    def start_chunk(tile, c, dst_slot):
        for u in range(chunk):
            row = c * chunk + u
            _row_copy(f_ref, src_ref[tile * tm + row], xin_ref.at[dst_slot], row, sem.at[dst_slot]).start()

    def wait_tile(dst_slot):
        def drain(c, carry):
            for u in range(chunk):
                _row_copy(f_ref, 0, xin_ref.at[dst_slot], 0, sem.at[dst_slot]).wait()
            return carry
        lax.fori_loop(0, nk, drain, 0)

    @pl.when((i == 0) & (k == 0))
    def _():
        def first(c, carry):
            start_chunk(0, c, 0)
            return carry
        lax.fori_loop(0, nk, first, 0)

    @pl.when((k == 0) & (i <= n_used))
    def _():
        wait_tile(slot)
        xb_ref[...] = xin_ref[slot, :tm, :].astype(BF16)

    @pl.when(k == 0)
    def _():
        o_ref[...] = jnp.zeros(o_ref.shape, F32)

    @pl.when(used)
    def _():
        start_chunk(i + 1, k, 1 - slot)
        x = xb_ref[...]
        hid = jax.nn.silu(_dot(x, w1_ref[0])) * _dot(x, w3_ref[0])
        o_ref[...] += _dot(hid.astype(BF16), w2_ref[0])

    @pl.when(used & (i == nt - 1) & (k == nk - 1))
    def _():
        wait_tile(1 - slot)


def moe_ffn(tile_expert, n_used, src, f, w1, w3, w2, rows, tm=MOE_TM, tf=256):
    n, d = f.shape
    ff = w1.shape[2]
    tf = min(tf, ff)
    nk = ff // tf
    chunk = pl.cdiv(tm, nk)
    buf_rows = ((chunk * nk + 7) // 8) * 8
    kk = lambda i, k, nu: jnp.where(i < nu[0], k, nk - 1)
    return pl.pallas_call(
        functools.partial(_moe_ffn_kernel, tm=tm, chunk=chunk),
        grid_spec=pltpu.PrefetchScalarGridSpec(
            num_scalar_prefetch=3,
            grid=(rows // tm, nk),
            in_specs=[
                pl.BlockSpec(memory_space=pl.ANY),
                pl.BlockSpec((1, d, tf), lambda i, k, te, nu, src: (te[i], 0, kk(i, k, nu))),
                pl.BlockSpec((1, d, tf), lambda i, k, te, nu, src: (te[i], 0, kk(i, k, nu))),
                pl.BlockSpec((1, tf, d), lambda i, k, te, nu, src: (te[i], kk(i, k, nu), 0)),
            ],
            out_specs=pl.BlockSpec((tm, d), lambda i, k, te, nu, src: (i, 0)),
            scratch_shapes=[pltpu.VMEM((2, buf_rows, d), F32), pltpu.VMEM((tm, d), BF16),
                            pltpu.SemaphoreType.DMA((2,))],
        ),
        out_shape=jax.ShapeDtypeStruct((rows, d), F32),
        compiler_params=_cparams("arbitrary", "arbitrary"),
        name="moe_ffn",
    )(tile_expert, n_used, src, f, w1, w3, w2)


def _combine_kernel(*refs, tm, final):
    if final:
        dest_ref, h_ref, rt_ref, yg_ref, fg_ref, o_ref, buf_ref, sem = refs
    else:
        dest_ref, h_ref, rt_ref, yg_ref, o_ref, buf_ref, sem = refs
    i = pl.program_id(0)
    nt = pl.num_programs(0)
    slot = i % 2

    def start_row(tile, j, dst_slot):
        for s in range(TOP_K):
            _row_copy(yg_ref, dest_ref[(tile * tm + j) * TOP_K + s], buf_ref.at[dst_slot, s], j,
                      sem.at[dst_slot]).start()

    def wait_tile(dst_slot):
        def drain(j, carry):
            for s in range(TOP_K):
                _row_copy(yg_ref, 0, buf_ref.at[dst_slot, s], 0, sem.at[dst_slot]).wait()
            return carry
        lax.fori_loop(0, tm, drain, 0, unroll=8)

    @pl.when(i == 0)
    def _():
        def first(j, carry):
            start_row(0, j, 0)
            return carry
        lax.fori_loop(0, tm, first, 0, unroll=8)

    wait_tile(slot)
    for j in range(tm):
        start_row(i + 1, j, 1 - slot)

    rt = rt_ref[...]
    y = (h_ref[...] + rt[:, ROUTE_G1:ROUTE_G1 + 1] * buf_ref[slot, 0]
         + rt[:, ROUTE_G2:ROUTE_G2 + 1] * buf_ref[slot, 1])
    if final:
        ms = jnp.mean(y * y, axis=-1, keepdims=True)
        y = y * lax.rsqrt(ms + EPS) * fg_ref[...]
    o_ref[...] = y

    @pl.when(i == nt - 1)
    def _():
        wait_tile(1 - slot)


def moe_combine(dest, h2, route, yg, final_g=None, tm=128):
    n, d = h2.shape
    tm = min(tm, n)
    dest = jnp.pad(dest, (0, tm * TOP_K))
    in_specs = [
        pl.BlockSpec((tm, d), lambda i, dest: (i, 0)),
        pl.BlockSpec((tm, LANES), lambda i, dest: (i, 0)),
        pl.BlockSpec(memory_space=pl.ANY),
    ]
    args = [dest, h2, route, yg]
    if final_g is not None:
        in_specs.append(pl.BlockSpec((1, d), lambda i, dest: (0, 0)))
        args.append(final_g)
    return pl.pallas_call(
        functools.partial(_combine_kernel, tm=tm, final=final_g is not None),
        grid_spec=pltpu.PrefetchScalarGridSpec(
            num_scalar_prefetch=1,
            grid=(n // tm,),
            in_specs=in_specs,
            out_specs=pl.BlockSpec((tm, d), lambda i, dest: (i, 0)),
            scratch_shapes=[pltpu.VMEM((2, TOP_K, tm, d), F32), pltpu.SemaphoreType.DMA((2,))],
        ),
        out_shape=jax.ShapeDtypeStruct((n, d), F32),
        compiler_params=_cparams("arbitrary"),
        name="moe_combine",
    )(*args)


def _routing_tables(route, tm):
    n = route.shape[0]
    flat_e = route[:, ROUTE_E1:ROUTE_E2 + 1].astype(jnp.int32).reshape(-1)
    onehot = (flat_e[:, None] == jnp.arange(N_EXPERTS, dtype=jnp.int32)[None, :]).astype(jnp.int32)
    csum = jnp.cumsum(onehot, axis=0)
    rank = jnp.sum(csum * onehot, axis=1) - 1
    counts = csum[-1]
    padded = ((counts + tm - 1) // tm) * tm
    off_end = jnp.cumsum(padded)
    dest = (off_end - padded)[flat_e] + rank
    n_tiles = (n * TOP_K) // tm + N_EXPERTS
    n_used = (off_end[-1] // tm).astype(jnp.int32)
    tile_start = jnp.arange(n_tiles, dtype=jnp.int32) * tm
    tile_start = jnp.minimum(tile_start, off_end[-1] - tm)
    tile_expert = jnp.sum((tile_start[:, None] >= off_end[None, :]).astype(jnp.int32), axis=1)
    rows = n_tiles * tm
    dest = dest.astype(jnp.int32)
    return dest, _invert_rows(dest, rows + tm + 8), tile_expert.astype(jnp.int32), n_used.reshape(1), rows


def _invert_kernel(dest_ref, src_ref):
    def clear(r, carry):
        src_ref[r] = 0
        return carry

    lax.fori_loop(0, src_ref.shape[0], clear, 0, unroll=8)

    def put(a, carry):
        src_ref[dest_ref[a]] = lax.shift_right_logical(a, 1)
        return carry

    lax.fori_loop(0, dest_ref.shape[0], put, 0, unroll=8)


def _invert_rows(dest, size):
    assert TOP_K == 2 and size % 8 == 0 and dest.shape[0] % 8 == 0
    return pl.pallas_call(
        _invert_kernel,
        in_specs=[pl.BlockSpec(memory_space=pltpu.SMEM)],
        out_specs=pl.BlockSpec(memory_space=pltpu.SMEM),
        out_shape=jax.ShapeDtypeStruct((size,), jnp.int32),
        name="moe_invert",
    )(dest)


def moe_layer(h2, f, route, w1, w3, w2, final_g=None):
    dest, src, tile_expert, n_used, rows = _routing_tables(route, MOE_TM)
    yg = moe_ffn(tile_expert, n_used, src, f, w1, w3, w2, rows)
    return moe_combine(dest, h2, route, yg, final_g=final_g)


def _reorder_w_in(w_in):
    d = w_in.shape[0]
    kvw = NSA_KV_WIDTH
    o_gate = NSA_WIDTH + 6 * kvw
    o_uv = o_gate + 3 * NSA_HEADS
    o_ret = o_uv + 2 * GMLP_WIDTH
    gate = w_in[:, o_gate:o_uv].reshape(d, NSA_KV_HEADS, 3 * NSA_GROUP)
    gate = jnp.pad(gate, ((0, 0), (0, 0), (0, LANES - 3 * NSA_GROUP))).reshape(d, NSA_KV_HEADS * LANES)
    parts = [w_in[:, :o_gate], w_in[:, o_uv:], gate]
    w = jnp.concatenate(parts, axis=1)
    w = jnp.pad(w, ((0, 0), (0, PROJ_COLS - w.shape[1])))
    scale = np.ones((1, PROJ_COLS), np.float32)
    scale[:, COL_Q * LANES:(COL_Q + NSA_HEADS) * LANES] = HEAD_DIM ** -0.5
    scale[:, COL_RK * LANES:(COL_RK + RET_HEADS) * LANES] = HEAD_DIM ** -0.5
    del o_ret
    return w.astype(BF16), jnp.asarray(scale)


def _key_extra_lanes(seq):
    pos = np.arange(seq)
    kx = np.zeros((seq, LANES), np.float32)
    kx[pos, pos // SLC_LEN] = 1.0
    kx[:, XL_HI] = pos // SLC_LEN
    kx[:, XL_LO] = pos % SLC_LEN
    kx[:, XL_ONE_A] = 1.0
    kx[:, XL_ONE_B] = 1.0
    return jnp.asarray(kx, BF16)


def _share_t(seq):
    nrow = seq // CMP_STRIDE
    nb = seq // SLC_LEN
    c0 = np.arange(nrow)[None, :] * CMP_STRIDE
    s0 = np.arange(nb)[:, None] * SLC_LEN
    overlap = np.minimum(c0 + CMP_LEN, s0 + SLC_LEN) - np.maximum(c0, s0)
    share = np.clip(overlap, 0, CMP_LEN).astype(np.float32) / CMP_LEN
    share[:, nrow - 1] = 0.0
    return jnp.asarray(share, BF16)


def _retention_tables():
    hh = np.arange(RET_HEADS, dtype=np.float64)
    lg = np.log1p(-np.exp2(-5.0 - hh))
    nn = np.arange(RET_CHUNK, dtype=np.float64)
    rel = nn[:, None] - nn[None, :]
    dec = np.where(rel >= 0, np.exp(lg[:, None, None] * np.maximum(rel, 0.0)), 0.0)
    zeta = np.exp(lg[:, None] * (RET_CHUNK - 1.0 - nn))[:, :, None]
    xi = np.exp(lg[:, None] * (nn + 1.0))[:, :, None]
    dch = np.broadcast_to(np.exp(lg * RET_CHUNK)[:, None, None], (RET_HEADS, 1, HEAD_DIM))
    return tuple(jnp.asarray(a, F32) for a in (dec, zeta, xi, dch))


def _token_mixer(h2, batch, seq, ln1_g, w_in, pe_k, w1_k, w2_k, pe_v, w1_v, w2_v,
                 g_ln_g, g_ln_b, g_ws, g_bs, ret_gn_g, consts):
    kx, sht, ret_tabs, cmp_tabs, axq, wbias = consts
    w_r, colscale = _reorder_w_in(w_in)
    proj = in_proj(h2, ln1_g[None, :], w_r, colscale)
    nrow = seq // CMP_STRIDE
    kv = proj[:, COL_KCMP * LANES:(COL_VCMP + NSA_KV_HEADS) * LANES]
    kv = kv.reshape(batch, nrow, CMP_STRIDE, 2, NSA_KV_HEADS, HEAD_DIM)
    rows = kv.transpose(3, 0, 4, 1, 2, 5).reshape(2, batch * NSA_KV_HEADS, nrow, CMP_STRIDE * HEAD_DIM)
    pe = jnp.stack([pe_k, pe_v]).reshape(2, 1, CMP_LEN * HEAD_DIM)
    kvc = compress(rows, pe, jnp.stack([w1_k, w1_v]).astype(BF16), jnp.stack([w2_k, w2_v]).astype(BF16))
    o_cmp, xsel, counts = cmp_attn(proj, kvc, sht, cmp_tabs, batch, seq)
    o_slc = sel_attn(proj, xsel, counts, kx, axq, batch, seq)
    o_nsa = win_attn(proj, kx, axq, wbias, o_cmp, o_slc, batch, seq)
    bsx = jnp.repeat(g_bs.T, HEAD_DIM, axis=1)
    o_gm = gmlp(proj, g_ln_g[None, :], g_ln_b[None, :], g_ws, bsx)
    o_ret = retention(proj, ret_tabs, ret_gn_g[None, :], batch, seq)
    return o_nsa, o_gm, o_ret


def kernel(x, ln1_g, w_in, cmp_pe_k, cmp_w1_k, cmp_w2_k, cmp_pe_v, cmp_w1_v, cmp_w2_v, gmlp_ln_g, gmlp_ln_b, gmlp_ws, gmlp_bs, ret_gn_g, w_out, ln2_g, ffn_w1, ffn_w3, ffn_w2, moe_wr, moe_br, moe_w1, moe_w3, moe_w2, final_g):
    batch, seq, d = x.shape
    depth = w_in.shape[0]
    consts = (_key_extra_lanes(seq), _share_t(seq), _retention_tables(), _cmp_tables(seq),
              _alibi_q_table(seq), _window_bias(WIN_TQ))
    h2 = x.reshape(batch * seq, d)
    for layer in range(depth):
        o_nsa, o_gm, o_ret = _token_mixer(
            h2, batch, seq, ln1_g[layer], w_in[layer], cmp_pe_k[layer], cmp_w1_k[layer], cmp_w2_k[layer],
            cmp_pe_v[layer], cmp_w1_v[layer], cmp_w2_v[layer], gmlp_ln_g[layer], gmlp_ln_b[layer],
            gmlp_ws[layer], gmlp_bs[layer], ret_gn_g[layer], consts)
        fin = final_g[None, :] if layer == depth - 1 else None
        i = layer // 2
        w_o = w_out[layer].astype(BF16)
        if layer % 2 == 0:
            h2, f = out_proj(o_nsa, o_gm, o_ret, h2, w_o, ln2_g[layer][None, :])
            h2 = ffn(f, h2, ffn_w1[i].astype(BF16), ffn_w3[i].astype(BF16), ffn_w2[i].astype(BF16),
                     final_g=fin)
        else:
            wr = jnp.pad(moe_wr[i], ((0, 0), (0, LANES - N_EXPERTS)))
            wr_hi = wr.astype(BF16)
            wr_lo = (wr - wr_hi.astype(F32)).astype(BF16)
            br = jnp.pad(moe_br[i], (0, LANES - N_EXPERTS))[None, :]
            h2, f, route = out_proj(o_nsa, o_gm, o_ret, h2, w_o, ln2_g[layer][None, :],
                                    router_w=(wr_hi, wr_lo, br))
            h2 = moe_layer(h2, f, route, moe_w1[i].astype(BF16), moe_w3[i].astype(BF16),
                           moe_w2[i].astype(BF16), final_g=fin)
    return h2.reshape(batch, seq, d)
```

```python
import functools

import numpy as np
import jax
import jax.numpy as jnp
from jax import lax
from jax.experimental import pallas as pl
from jax.experimental.pallas import tpu as pltpu

F32 = jnp.float32
BF16 = jnp.bfloat16

HEAD_DIM = 128
NSA_HEADS = 8
NSA_KV_HEADS = 2
NSA_GROUP = NSA_HEADS // NSA_KV_HEADS
NSA_WIDTH = NSA_HEADS * HEAD_DIM
NSA_KV_WIDTH = NSA_KV_HEADS * HEAD_DIM
CMP_LEN = 32
CMP_STRIDE = 16
SLC_LEN = 64
SLC_TOPK = 16
WIN = 512
GMLP_GROUPS = 4
GMLP_CHUNK = 128
GMLP_WIDTH = GMLP_GROUPS * HEAD_DIM
RET_HEADS = 4
RET_CHUNK = 128
RET_WIDTH = RET_HEADS * HEAD_DIM
N_EXPERTS = 8
TOP_K = 2
EPS = 1e-6
NEG_INF = -1e30
FORCE_SCORE = 1e4

LANES = 128
VMEM_LIMIT = 56 * 1024 * 1024

COL_Q = 0
COL_KCMP = 8
COL_VCMP = 10
COL_KSLC = 12
COL_VSLC = 14
COL_KWIN = 16
COL_VWIN = 18
COL_U = 20
COL_V = 24
COL_RQ = 28
COL_RK = 32
COL_RV = 36
COL_RG = 40
COL_GATE = 44
PROJ_COLS = 48 * LANES

XL_HI = 64
XL_LO = 65
XL_ONE_A = 66
XL_ONE_B = 67


def _cparams(*sem):
    return pltpu.CompilerParams(dimension_semantics=sem, vmem_limit_bytes=VMEM_LIMIT)


def _nt_dot(a, b):
    return lax.dot_general(a, b, (((1,), (1,)), ((), ())), preferred_element_type=F32)


def _dot(a, b):
    return jnp.dot(a, b, preferred_element_type=F32)


def _in_proj_kernel(x_ref, g_ref, w_ref, cs_ref, o_ref, xn_ref):
    @pl.when(pl.program_id(1) == 0)
    def _():
        x = x_ref[...]
        ms = jnp.mean(x * x, axis=-1, keepdims=True)
        xn_ref[...] = (x * lax.rsqrt(ms + EPS) * g_ref[...]).astype(BF16)

    acc = _dot(xn_ref[...], w_ref[...])
    o_ref[...] = (acc * cs_ref[...]).astype(o_ref.dtype)


def in_proj(h2, g, w, colscale, tm=1024, tn=1024):
    n, d = h2.shape
    cols = w.shape[1]
    tm = min(tm, n)
    return pl.pallas_call(
        _in_proj_kernel,
        grid=(n // tm, cols // tn),
        in_specs=[
            pl.BlockSpec((tm, d), lambda i, j: (i, 0)),
            pl.BlockSpec((1, d), lambda i, j: (0, 0)),
            pl.BlockSpec((d, tn), lambda i, j: (0, j)),
            pl.BlockSpec((1, tn), lambda i, j: (0, j)),
        ],
        out_specs=pl.BlockSpec((tm, tn), lambda i, j: (i, j)),
        out_shape=jax.ShapeDtypeStruct((n, cols), BF16),
        scratch_shapes=[pltpu.VMEM((tm, d), BF16)],
        compiler_params=_cparams("parallel", "arbitrary"),
        name="in_proj",
    )(h2, g, w, colscale)


def _compress_kernel(r_ref, pe_ref, w1_ref, w2_ref, o_ref):
    half = CMP_STRIDE * HEAD_DIM
    r = r_ref[0, 0].astype(F32)
    nrow = r.shape[0]
    pe = pe_ref[0]
    a = _dot((r + pe[:, :half]).astype(BF16), w1_ref[0, :half, :])
    b = _dot((r + pe[:, half:]).astype(BF16), w1_ref[0, half:, :])
    pre = a + pltpu.roll(b, nrow - 1, 0)
    y = _dot(jax.nn.gelu(pre).astype(BF16), w2_ref[0])
    row = lax.broadcasted_iota(jnp.int32, y.shape, 0)
    o_ref[0, 0] = jnp.where(row < nrow - 1, y, 0.0).astype(o_ref.dtype)


def compress(rows, pe, w1, w2):
    _, bg, nrow, width = rows.shape
    return pl.pallas_call(
        _compress_kernel,
        grid=(2, bg),
        in_specs=[
            pl.BlockSpec((1, 1, nrow, width), lambda s, i: (s, i, 0, 0)),
            pl.BlockSpec((1, 1, 2 * width), lambda s, i: (s, 0, 0)),
            pl.BlockSpec((1, 2 * width, HEAD_DIM), lambda s, i: (s, 0, 0)),
            pl.BlockSpec((1, HEAD_DIM, HEAD_DIM), lambda s, i: (s, 0, 0)),
        ],
        out_specs=pl.BlockSpec((1, 1, nrow, HEAD_DIM), lambda s, i: (s, i, 0, 0)),
        out_shape=jax.ShapeDtypeStruct((2, bg, nrow, HEAD_DIM), BF16),
        compiler_params=_cparams("arbitrary", "arbitrary"),
        name="compress",
    )(rows, pe, w1, w2)


def _group_slope(g, r):
    return jnp.where(g == 0, F32(2.0 ** -(r + 1)), F32(2.0 ** -(r + 1 + NSA_GROUP)))


SEL_TQ = 256
SEL_TK = 512
SEL_TILE_ROWS = 8


def _cmp_attn_kernel(q_ref, kc_ref, vc_ref, sht_ref, dist_ref, cb_ref, grp_ref, o_ref, x_ref, c_ref,
                     *, tq, nb, topk):
    g = pl.program_id(0) % NSA_KV_HEADS
    t0 = pl.program_id(1) * tq
    kc = kc_ref[0, 0]
    vc = vc_ref[0, 0]
    nrow = kc.shape[0]
    distf = dist_ref[...]
    cbias = cb_ref[...]
    row_ok = jnp.where(t0 + lax.broadcasted_iota(jnp.int32, (tq, 1), 0) >= CMP_LEN - 1, 1.0, 0.0)
    psum = jnp.zeros((tq, nrow), F32)
    for r in range(NSA_GROUP):
        q = q_ref[:, r * HEAD_DIM:(r + 1) * HEAD_DIM]
        s = _nt_dot(q, kc) - _group_slope(g, r) * distf + cbias
        e = jnp.exp(s - jnp.max(s, axis=-1, keepdims=True))
        p = e * (row_ok / jnp.sum(e, axis=-1, keepdims=True))
        o_ref[:, r * HEAD_DIM:(r + 1) * HEAD_DIM] = _dot(p.astype(BF16), vc).astype(o_ref.dtype)
        psum = psum + p
    p_hi = psum.astype(BF16)
    p_lo = (psum - p_hi.astype(F32)).astype(BF16)
    sht = sht_ref[...]
    imp = _nt_dot(sht, p_hi) + _nt_dot(sht, p_lo)
    j = lax.broadcasted_iota(jnp.int32, (nb, tq), 0)
    cur = (t0 + lax.broadcasted_iota(jnp.int32, (nb, tq), 1)) // SLC_LEN
    forced = (j == 0) | (j == cur) | (j == cur - 1)
    score = jnp.where(forced, FORCE_SCORE, imp)
    cand = jnp.where(j > cur, -3e38, score)
    jf = j.astype(F32)[:, :LANES]
    groups = [cand[:, c:c + LANES] for c in range(0, tq, LANES)]
    for _ in range(topk):
        for n, cg in enumerate(groups):
            best = jnp.max(cg, axis=0, keepdims=True)
            idx = jnp.min(jnp.where(cg == best, jf, float(nb)), axis=0, keepdims=True)
            groups[n] = jnp.where(jf == idx, -jnp.inf, cg)
    cand = jnp.concatenate(groups, axis=1)
    picked = cand == -jnp.inf
    grp = grp_ref[...]
    cnt = _dot(grp, jnp.where(picked, 1.0, 0.0).astype(BF16))
    c_ref[0] = jnp.broadcast_to(jnp.sum(cnt, axis=1, keepdims=True), c_ref.shape[1:])
    neg = jnp.where(picked, 0.0, NEG_INF)
    neg = jnp.concatenate([neg, jnp.zeros((LANES - nb, tq), F32)], axis=0)
    x_ref[...] = neg.T.astype(x_ref.dtype)


def _cmp_tables(seq):
    nrow = seq // CMP_STRIDE
    t = np.arange(seq)[:, None]
    c = np.arange(nrow)[None, :]
    dist = t - (c * CMP_STRIDE + (CMP_LEN - 1))
    valid = (dist >= 0) & (c < nrow - 1)
    return jnp.asarray(dist, F32), jnp.asarray(np.where(valid, 0.0, NEG_INF), F32)


def cmp_attn(proj, kvc, sht, tables, batch, seq, tq=256):
    n = proj.shape[0]
    nb = seq // SLC_LEN
    bg = batch * NSA_KV_HEADS
    nrow = kvc.shape[2]
    qt = seq // tq
    rowblk = lambda i, t: ((i // NSA_KV_HEADS) * qt + t, i % NSA_KV_HEADS)
    kern = functools.partial(_cmp_attn_kernel, tq=tq, nb=nb, topk=min(SLC_TOPK, nb))
    return pl.pallas_call(
        kern,
        grid=(bg, qt),
        in_specs=[
            pl.BlockSpec((tq, NSA_GROUP * HEAD_DIM), rowblk),
            pl.BlockSpec((1, 1, nrow, HEAD_DIM), lambda i, t: (0, i, 0, 0)),
            pl.BlockSpec((1, 1, nrow, HEAD_DIM), lambda i, t: (1, i, 0, 0)),
            pl.BlockSpec((nb, nrow), lambda i, t: (0, 0)),
            pl.BlockSpec((tq, nrow), lambda i, t: (t, 0)),
            pl.BlockSpec((tq, nrow), lambda i, t: (t, 0)),
            pl.BlockSpec((SEL_TILE_ROWS, nb), lambda i, t: (0, 0)),
        ],
        out_specs=[
            pl.BlockSpec((tq, NSA_GROUP * HEAD_DIM), rowblk),
            pl.BlockSpec((tq, LANES), rowblk),
            pl.BlockSpec((1, SEL_TILE_ROWS, LANES), lambda i, t: (i * qt + t, 0, 0)),
        ],
        out_shape=[
            jax.ShapeDtypeStruct((n, NSA_WIDTH), BF16),
            jax.ShapeDtypeStruct((n, NSA_KV_HEADS * LANES), BF16),
            jax.ShapeDtypeStruct((bg * qt, SEL_TILE_ROWS, LANES), F32),
        ],
        compiler_params=_cparams("parallel", "parallel"),
        name="cmp_attn",
    )(proj, kvc, kvc, sht, *tables, _tile_groups(seq))


def _tile_groups(seq):
    nb = seq // SLC_LEN
    assert seq // SEL_TK <= SEL_TILE_ROWS
    grp = np.zeros((SEL_TILE_ROWS, nb), np.float32)
    grp[np.arange(nb) // (SEL_TK // SLC_LEN), np.arange(nb)] = 1.0
    return jnp.asarray(grp, BF16)


def _active_key_tiles(counts, batch, seq, tq, tk):
    qt = seq // tq
    n_full = ((jnp.arange(qt, dtype=jnp.int32) * tq + tq - 1) // tk)[None, :, None]
    kt = jnp.arange(SEL_TILE_ROWS, dtype=jnp.int32)[None, None, :]
    cnt = counts[:, :, 0].reshape(batch * NSA_KV_HEADS, qt, SEL_TILE_ROWS)
    active = (cnt > 0) & (kt < n_full)
    order = jnp.argsort(jnp.where(active, 0, 1), axis=-1, stable=True).astype(jnp.int32)
    return order.reshape(-1), jnp.sum(active, axis=-1).astype(jnp.int32).reshape(-1)


def _alibi_q_table(seq):
    t = np.arange(seq)
    tab = np.zeros((NSA_HEADS, seq, LANES), np.float32)
    for h in range(NSA_HEADS):
        slope = 2.0 ** (-8.0 * (h + 1) / NSA_HEADS)
        tab[h, :, XL_HI] = slope * SLC_LEN
        tab[h, :, XL_LO] = slope
        tab[h, :, XL_ONE_A] = -(slope * SLC_LEN) * (t // SLC_LEN)
        tab[h, :, XL_ONE_B] = -slope * (t % SLC_LEN)
    return jnp.asarray(tab, BF16)


def _sel_attn_kernel(act_ref, nact_ref, q_ref, x_ref, ax_ref, k_ref, v_ref, kx_ref, o_ref,
                     q2_ref, sa_ref, sb_ref, m_ref, acc_ref, *, tq, tk):
    t0 = pl.program_id(1) * tq
    n_kt = (t0 + tq - 1) // tk + 1
    rows = NSA_GROUP * tq
    half = rows // 2
    tpos = t0 + (lax.broadcasted_iota(jnp.int32, (half, tk), 0) & (tq - 1))
    kcol = lax.broadcasted_iota(jnp.int32, (half, tk), 1)

    xsel = x_ref[...]
    q2_ref[...] = jnp.concatenate([
        jnp.concatenate([q_ref[:, r * HEAD_DIM:(r + 1) * HEAD_DIM], xsel + ax_ref[r]], axis=1)
        for r in range(NSA_GROUP)], axis=0)
    m_ref[...] = jnp.full(m_ref.shape, -jnp.inf, F32)
    acc_ref[...] = jnp.zeros(acc_ref.shape, F32)
    ones = jnp.ones((tk, LANES), BF16)

    def scores(kt, dst_ref):
        ks = pl.multiple_of(kt * tk, tk)
        k2 = jnp.concatenate([k_ref[pl.ds(ks, tk), :], kx_ref[pl.ds(ks, tk), :]], axis=1)
        dst_ref[...] = _nt_dot(q2_ref[...], k2)

    def update(kt, src_ref, causal):
        ks = pl.multiple_of(kt * tk, tk)
        v2 = jnp.concatenate([v_ref[pl.ds(ks, tk), :], ones], axis=1)
        step = half if causal else rows
        for r0 in range(0, rows, step):
            rs = slice(r0, r0 + step)
            s = src_ref[rs, :]
            if causal:
                s = jnp.where(ks + kcol <= tpos, s, NEG_INF)
            m_prev = m_ref[rs, :]
            m_new = jnp.maximum(m_prev, jnp.max(s, axis=-1, keepdims=True))
            alpha = jnp.exp(m_prev - m_new)
            p = jnp.exp(s - jnp.concatenate([m_new] * (tk // LANES), axis=1))
            acc_ref[rs, :] = (jnp.concatenate([alpha, alpha], axis=1) * acc_ref[rs, :]
                              + _dot(p.astype(BF16), v2))
            m_ref[rs, :] = m_new

    step_id = pl.program_id(0) * pl.num_programs(1) + pl.program_id(1)
    n_act = nact_ref[step_id]
    diag = n_kt - 1

    def visited(j):
        return jnp.where(j < n_act, act_ref[step_id * SEL_TILE_ROWS + jnp.minimum(j, SEL_TILE_ROWS - 1)], diag)

    scores(visited(0), sa_ref)

    def body(j, carry):
        scores(visited(2 * j + 1), sb_ref)
        update(visited(2 * j), sa_ref, False)
        scores(visited(2 * j + 2), sa_ref)
        update(visited(2 * j + 1), sb_ref, False)
        return carry

    lax.fori_loop(0, n_act // 2, body, 0)

    @pl.when(n_act % 2 == 1)
    def _():
        scores(diag, sb_ref)
        update(visited(n_act - 1), sa_ref, False)
        update(diag, sb_ref, True)

    @pl.when(n_act % 2 == 0)
    def _():
        update(diag, sa_ref, True)

    acc = acc_ref[...]
    o = acc[:, :HEAD_DIM] / acc[:, HEAD_DIM:]
    for r in range(NSA_GROUP):
        o_ref[:, r * HEAD_DIM:(r + 1) * HEAD_DIM] = o[r * tq:(r + 1) * tq].astype(o_ref.dtype)


def sel_attn(proj, xsel, counts, kx, axq, batch, seq, tq=SEL_TQ, tk=SEL_TK):
    n = proj.shape[0]
    bg = batch * NSA_KV_HEADS
    qt = seq // tq
    act, n_act = _active_key_tiles(counts, batch, seq, tq, tk)
    rowblk = lambda i, t, a, na: ((i // NSA_KV_HEADS) * qt + t, i % NSA_KV_HEADS)
    kern = functools.partial(_sel_attn_kernel, tq=tq, tk=tk)
    return pl.pallas_call(
        kern,
        grid_spec=pltpu.PrefetchScalarGridSpec(
            num_scalar_prefetch=2,
            grid=(bg, qt),
            in_specs=[
                pl.BlockSpec((tq, NSA_GROUP * HEAD_DIM), rowblk),
                pl.BlockSpec((tq, LANES), rowblk),
                pl.BlockSpec((NSA_GROUP, tq, LANES), lambda i, t, a, na: (i % NSA_KV_HEADS, t, 0)),
                pl.BlockSpec((seq, HEAD_DIM),
                             lambda i, t, a, na: (i // NSA_KV_HEADS, COL_KSLC + i % NSA_KV_HEADS)),
                pl.BlockSpec((seq, HEAD_DIM),
                             lambda i, t, a, na: (i // NSA_KV_HEADS, COL_VSLC + i % NSA_KV_HEADS)),
                pl.BlockSpec((seq, LANES), lambda i, t, a, na: (0, 0)),
            ],
            out_specs=pl.BlockSpec((tq, NSA_GROUP * HEAD_DIM), rowblk),
            scratch_shapes=[
                pltpu.VMEM((NSA_GROUP * tq, 2 * HEAD_DIM), BF16),
                pltpu.VMEM((NSA_GROUP * tq, tk), F32),
                pltpu.VMEM((NSA_GROUP * tq, tk), F32),
                pltpu.VMEM((NSA_GROUP * tq, LANES), F32),
                pltpu.VMEM((NSA_GROUP * tq, 2 * HEAD_DIM), F32),
            ],
        ),
        out_shape=jax.ShapeDtypeStruct((n, NSA_WIDTH), BF16),
        compiler_params=_cparams("parallel", "parallel"),
        name="sel_attn",
    )(act, n_act, proj, xsel, axq, proj, proj, kx)


def _win_attn_kernel(q_ref, ax_ref, wb_ref, k_ref, v_ref, kx_ref, gl_ref, oc_ref, os_ref, o_ref, *, tq):
    t0 = pl.program_id(1) * tq
    span = WIN + tq
    starts = [pl.multiple_of(jnp.maximum(t0 - WIN + c * tq, 0), LANES) for c in range(span // tq)]
    k2 = jnp.concatenate([
        jnp.concatenate([k_ref[pl.ds(st, tq), :], kx_ref[pl.ds(st, tq), :]], axis=1)
        for st in starts], axis=0)
    v2 = jnp.concatenate([
        jnp.concatenate([v_ref[pl.ds(st, tq), :] for st in starts], axis=0),
        jnp.ones((span, LANES), BF16)], axis=1)
    bias = wb_ref[0]
    gates = jax.nn.sigmoid(gl_ref[...].astype(F32))
    pair = 1
    for h0 in range(0, NSA_GROUP, pair):
        heads = range(h0, h0 + pair)
        q2 = jnp.concatenate([
            jnp.concatenate([q_ref[:, r * HEAD_DIM:(r + 1) * HEAD_DIM], ax_ref[r]], axis=1)
            for r in heads], axis=0)
        s = _nt_dot(q2, k2)
        parts = []
        for n, r in enumerate(heads):
            sr = s[n * tq:(n + 1) * tq] + bias
            parts.append(jnp.exp(sr - jnp.max(sr, axis=-1, keepdims=True)).astype(BF16))
        ov = _dot(jnp.concatenate(parts, axis=0), v2)
        o_win = ov[:, :HEAD_DIM] / ov[:, HEAD_DIM:]
        for n, r in enumerate(heads):
            cols = slice(r * HEAD_DIM, (r + 1) * HEAD_DIM)
            o = (gates[:, 3 * r:3 * r + 1] * oc_ref[:, cols].astype(F32)
                 + gates[:, 3 * r + 1:3 * r + 2] * os_ref[:, cols].astype(F32)
                 + gates[:, 3 * r + 2:3 * r + 3] * o_win[n * tq:(n + 1) * tq])
            o_ref[:, cols] = o.astype(o_ref.dtype)


WIN_TQ = 256


def _window_bias(tq):
    nvar = WIN // tq + 1
    r = np.arange(tq)[None, :, None]
    c = np.arange(WIN + tq)[None, None, :]
    t0 = (np.arange(nvar) * tq)[:, None, None]
    kpos = t0 - WIN + c
    dist = (t0 + r) - kpos
    ok = (kpos >= 0) & (dist >= 0) & (dist < WIN)
    return jnp.asarray(np.where(ok, 0.0, NEG_INF), F32)


def win_attn(proj, kx, axq, wbias, o_cmp, o_slc, batch, seq, tq=WIN_TQ):
    n = proj.shape[0]
    bg = batch * NSA_KV_HEADS
    qt = seq // tq
    nvar = wbias.shape[0]
    rowblk = lambda i, t: ((i // NSA_KV_HEADS) * qt + t, i % NSA_KV_HEADS)
    kern = functools.partial(_win_attn_kernel, tq=tq)
    return pl.pallas_call(
        kern,
        grid=(bg, qt),
        in_specs=[
            pl.BlockSpec((tq, NSA_GROUP * HEAD_DIM), rowblk),
            pl.BlockSpec((NSA_GROUP, tq, LANES), lambda i, t: (i % NSA_KV_HEADS, t, 0)),
            pl.BlockSpec((1, tq, WIN + tq), lambda i, t: (jnp.minimum(t, nvar - 1), 0, 0)),
            pl.BlockSpec((seq, HEAD_DIM), lambda i, t: (i // NSA_KV_HEADS, COL_KWIN + i % NSA_KV_HEADS)),
            pl.BlockSpec((seq, HEAD_DIM), lambda i, t: (i // NSA_KV_HEADS, COL_VWIN + i % NSA_KV_HEADS)),
            pl.BlockSpec((seq, LANES), lambda i, t: (0, 0)),
            pl.BlockSpec((tq, LANES), lambda i, t: ((i // NSA_KV_HEADS) * qt + t, COL_GATE + i % NSA_KV_HEADS)),
            pl.BlockSpec((tq, NSA_GROUP * HEAD_DIM), rowblk),
            pl.BlockSpec((tq, NSA_GROUP * HEAD_DIM), rowblk),
        ],
        out_specs=pl.BlockSpec((tq, NSA_GROUP * HEAD_DIM), rowblk),
        out_shape=jax.ShapeDtypeStruct((n, NSA_WIDTH), BF16),
        compiler_params=_cparams("parallel", "parallel"),
        name="win_attn",
    )(proj, axq, wbias, proj, proj, kx, proj, o_cmp, o_slc)


def _gmlp_kernel(u_ref, v_ref, lg_ref, lb_ref, ws_ref, bsx_ref, o_ref, *, tm):
    u = jax.nn.gelu(u_ref[...].astype(F32))
    v = jax.nn.gelu(v_ref[...].astype(F32))
    mu = jnp.mean(v, axis=-1, keepdims=True)
    vc = v - mu
    var = jnp.mean(vc * vc, axis=-1, keepdims=True)
    vn = (vc * lax.rsqrt(var + EPS) * lg_ref[...] + lb_ref[...]).astype(BF16)
    c = GMLP_CHUNK
    tri = (lax.broadcasted_iota(jnp.int32, (c, c), 0) >= lax.broadcasted_iota(jnp.int32, (c, c), 1))
    bsx = bsx_ref[...]
    for gi in range(GMLP_GROUPS):
        w = jnp.where(tri, ws_ref[gi], 0.0).astype(BF16)
        cols = slice(gi * HEAD_DIM, (gi + 1) * HEAD_DIM)
        for ci in range(tm // c):
            rows = slice(ci * c, (ci + 1) * c)
            s = _dot(w, vn[rows, cols]) + bsx[:, cols]
            o_ref[rows, cols] = (u[rows, cols] * s).astype(o_ref.dtype)


def gmlp(proj, ln_g, ln_b, ws, bsx, tm=512):
    n = proj.shape[0]
    ublk = GMLP_WIDTH // LANES
    kern = functools.partial(_gmlp_kernel, tm=tm)
    return pl.pallas_call(
        kern,
        grid=(n // tm,),
        in_specs=[
            pl.BlockSpec((tm, GMLP_WIDTH), lambda i: (i, COL_U // ublk)),
            pl.BlockSpec((tm, GMLP_WIDTH), lambda i: (i, COL_V // ublk)),
            pl.BlockSpec((1, GMLP_WIDTH), lambda i: (0, 0)),
            pl.BlockSpec((1, GMLP_WIDTH), lambda i: (0, 0)),
            pl.BlockSpec((GMLP_GROUPS, GMLP_CHUNK, GMLP_CHUNK), lambda i: (0, 0, 0)),
            pl.BlockSpec((GMLP_CHUNK, GMLP_WIDTH), lambda i: (0, 0)),
        ],
        out_specs=pl.BlockSpec((tm, GMLP_WIDTH), lambda i: (i, 0)),
        out_shape=jax.ShapeDtypeStruct((n, GMLP_WIDTH), BF16),
        compiler_params=_cparams("parallel"),
        name="gmlp",
    )(proj, proj, ln_g, ln_b, ws, bsx)


def _retention_kernel(q_ref, k_ref, v_ref, g_ref, dec_ref, zeta_ref, xi_ref, dch_ref, gn_ref,
                      o_ref, s_ref, st_ref, *, seq):
    c = RET_CHUNK
    nc = seq // c
    heads = [slice(hh * HEAD_DIM, (hh + 1) * HEAD_DIM) for hh in range(RET_HEADS)]

    def kv_body(ci, carry):
        rows = pl.ds(pl.multiple_of(ci * c, c), c)
        for hh, cols in enumerate(heads):
            kz = (k_ref[rows, cols].astype(F32) * zeta_ref[hh]).astype(BF16)
            st_ref[ci, hh] = lax.dot_general(kz, v_ref[rows, cols], (((0,), (0,)), ((), ())),
                                             preferred_element_type=F32)
        return carry

    lax.fori_loop(0, nc, kv_body, 0, unroll=2)

    s_ref[...] = jnp.zeros(s_ref.shape, F32)

    def scan_body(ci, carry):
        for hh in range(RET_HEADS):
            state = s_ref[hh]
            s_ref[hh] = dch_ref[hh] * state + st_ref[ci, hh]
            st_ref[ci, hh] = state
        return carry

    lax.fori_loop(0, nc, scan_body, 0)

    def out_body(ci, carry):
        rows = pl.ds(pl.multiple_of(ci * c, c), c)
        for hh, cols in enumerate(heads):
            q = q_ref[rows, cols]
            v = v_ref[rows, cols]
            scores = _nt_dot(q, k_ref[rows, cols]) * dec_ref[hh]
            y = _dot(scores.astype(BF16), v) + _dot(q, st_ref[ci, hh].astype(BF16)) * xi_ref[hh]
            mu = jnp.mean(y, axis=-1, keepdims=True)
            yc = y - mu
            var = jnp.mean(yc * yc, axis=-1, keepdims=True)
            yn = yc * lax.rsqrt(var + EPS) * gn_ref[:, cols]
            o_ref[rows, cols] = (jax.nn.silu(g_ref[rows, cols].astype(F32)) * yn).astype(o_ref.dtype)
        return carry

    lax.fori_loop(0, nc, out_body, 0, unroll=2)


def retention(proj, tabs, gn_g, batch, seq):
    n = proj.shape[0]
    dec, zeta, xi, dch = tabs
    rblk = RET_WIDTH // LANES
    kern = functools.partial(_retention_kernel, seq=seq)
    full = lambda shape: pl.BlockSpec(shape, lambda b: (0,) * len(shape))
    return pl.pallas_call(
        kern,
        grid=(batch,),
        in_specs=[
            pl.BlockSpec((seq, RET_WIDTH), lambda b: (b, COL_RQ // rblk)),
            pl.BlockSpec((seq, RET_WIDTH), lambda b: (b, COL_RK // rblk)),
            pl.BlockSpec((seq, RET_WIDTH), lambda b: (b, COL_RV // rblk)),
            pl.BlockSpec((seq, RET_WIDTH), lambda b: (b, COL_RG // rblk)),
            full(dec.shape), full(zeta.shape), full(xi.shape), full(dch.shape),
            full((1, RET_WIDTH)),
        ],
        out_specs=pl.BlockSpec((seq, RET_WIDTH), lambda b: (b, 0)),
        out_shape=jax.ShapeDtypeStruct((n, RET_WIDTH), BF16),
        scratch_shapes=[pltpu.VMEM((RET_HEADS, HEAD_DIM, HEAD_DIM), F32),
                        pltpu.VMEM((seq // RET_CHUNK, RET_HEADS, HEAD_DIM, HEAD_DIM), F32)],
        compiler_params=_cparams("parallel"),
        name="retention",
    )(proj, proj, proj, proj, dec, zeta, xi, dch, gn_g)


ROUTE_E1, ROUTE_E2, ROUTE_G1, ROUTE_G2 = 0, 1, 2, 3


def _top2_route(logits):
    lane = lax.broadcasted_iota(jnp.int32, logits.shape, 1)
    lg = jnp.where(lane < N_EXPERTS, logits, -jnp.inf)
    v1 = jnp.max(lg, axis=-1, keepdims=True)
    i1 = jnp.min(jnp.where(lg == v1, lane, LANES), axis=-1, keepdims=True)
    lg2 = jnp.where(lane == i1, -jnp.inf, lg)
    v2 = jnp.max(lg2, axis=-1, keepdims=True)
    i2 = jnp.min(jnp.where(lg2 == v2, lane, LANES), axis=-1, keepdims=True)
    e2 = jnp.exp(v2 - v1)
    den = 1.0 + e2
    out = jnp.where(lane == ROUTE_E1, i1.astype(F32), 0.0)
    out = jnp.where(lane == ROUTE_E2, i2.astype(F32), out)
    out = jnp.where(lane == ROUTE_G1, 1.0 / den, out)
    return jnp.where(lane == ROUTE_G2, e2 / den, out)


def _out_proj_kernel(*refs, router):
    if router:
        nsa_ref, gm_ref, ret_ref, h_ref, w_ref, g2_ref, wrh_ref, wrl_ref, br_ref, ho_ref, f_ref, rt_ref = refs
    else:
        nsa_ref, gm_ref, ret_ref, h_ref, w_ref, g2_ref, ho_ref, f_ref = refs
    acc = _dot(nsa_ref[...], w_ref[:NSA_WIDTH, :])
    acc = acc + _dot(gm_ref[...], w_ref[NSA_WIDTH:NSA_WIDTH + GMLP_WIDTH, :])
    acc = acc + _dot(ret_ref[...], w_ref[NSA_WIDTH + GMLP_WIDTH:, :])
    hn = h_ref[...] + acc
    ho_ref[...] = hn
    ms = jnp.mean(hn * hn, axis=-1, keepdims=True)
    f = hn * lax.rsqrt(ms + EPS) * g2_ref[...]
    f_hi = f.astype(BF16)
    if router:
        f_ref[...] = f
        f_lo = (f - f_hi.astype(F32)).astype(BF16)
        logits = (_dot(f_hi, wrh_ref[...]) + _dot(f_lo, wrh_ref[...]) + _dot(f_hi, wrl_ref[...])
                  + br_ref[...])
        rt_ref[...] = _top2_route(logits)
    else:
        f_ref[...] = f_hi


def out_proj(o_nsa, o_gm, o_ret, h2, w, g2, router_w=None, tm=512):
    n, d = h2.shape
    tm = min(tm, n)
    router = router_w is not None
    row = lambda width: pl.BlockSpec((tm, width), lambda i: (i, 0))
    full = lambda shape: pl.BlockSpec(shape, lambda i: (0,) * len(shape))
    in_specs = [row(NSA_WIDTH), row(GMLP_WIDTH), row(RET_WIDTH), row(d), full(w.shape), full((1, d))]
    args = [o_nsa, o_gm, o_ret, h2, w, g2]
    if router:
        in_specs += [full((d, LANES)), full((d, LANES)), full((1, LANES))]
        args += list(router_w)
        out_specs = [row(d), row(d), row(LANES)]
        out_shape = [jax.ShapeDtypeStruct((n, d), F32), jax.ShapeDtypeStruct((n, d), F32),
                     jax.ShapeDtypeStruct((n, LANES), F32)]
    else:
        out_specs = [row(d), row(d)]
        out_shape = [jax.ShapeDtypeStruct((n, d), F32), jax.ShapeDtypeStruct((n, d), BF16)]
    return pl.pallas_call(
        functools.partial(_out_proj_kernel, router=router),
        grid=(n // tm,),
        in_specs=in_specs,
        out_specs=out_specs,
        out_shape=out_shape,
        compiler_params=_cparams("parallel"),
        name="out_proj_router" if router else "out_proj",
    )(*args)


def _ffn_kernel(*refs, final):
    if final:
        x_ref, h_ref, w1_ref, w3_ref, w2_ref, fg_ref, o_ref = refs
    else:
        x_ref, h_ref, w1_ref, w3_ref, w2_ref, o_ref = refs
    k = pl.program_id(1)

    @pl.when(k == 0)
    def _():
        o_ref[...] = h_ref[...]

    x = x_ref[...]
    hid = jax.nn.silu(_dot(x, w1_ref[...])) * _dot(x, w3_ref[...])
    o_ref[...] += _dot(hid.astype(BF16), w2_ref[...])

    if final:
        @pl.when(k == pl.num_programs(1) - 1)
        def _():
            y = o_ref[...]
            ms = jnp.mean(y * y, axis=-1, keepdims=True)
            o_ref[...] = y * lax.rsqrt(ms + EPS) * fg_ref[...]


def ffn(f, h2, w1, w3, w2, final_g=None, tm=512, tf=512):
    n, d = h2.shape
    ff = w1.shape[1]
    tm = min(tm, n)
    tf = min(tf, ff)
    in_specs = [
        pl.BlockSpec((tm, d), lambda i, k: (i, 0)),
        pl.BlockSpec((tm, d), lambda i, k: (i, 0)),
        pl.BlockSpec((d, tf), lambda i, k: (0, k)),
        pl.BlockSpec((d, tf), lambda i, k: (0, k)),
        pl.BlockSpec((tf, d), lambda i, k: (k, 0)),
    ]
    args = [f, h2, w1, w3, w2]
    if final_g is not None:
        in_specs.append(pl.BlockSpec((1, d), lambda i, k: (0, 0)))
        args.append(final_g)
    return pl.pallas_call(
        functools.partial(_ffn_kernel, final=final_g is not None),
        grid=(n // tm, ff // tf),
        in_specs=in_specs,
        out_specs=pl.BlockSpec((tm, d), lambda i, k: (i, 0)),
        out_shape=jax.ShapeDtypeStruct((n, d), F32),
        compiler_params=_cparams("parallel", "arbitrary"),
        name="ffn",
    )(*args)


MOE_TM = 512


def _row_copy(src_ref, src_row, dst_ref, dst_row, sem):
    return pltpu.make_async_copy(src_ref.at[pl.ds(src_row, 1)], dst_ref.at[pl.ds(dst_row, 1)], sem)


def _moe_ffn_kernel(te_ref, nu_ref, src_ref, f_ref, w1_ref, w3_ref, w2_ref, o_ref, xin_ref, xb_ref, sem,
                    *, tm, chunk):
    del te_ref
    i = pl.program_id(0)
    k = pl.program_id(1)
    nt = pl.num_programs(0)
    nk = pl.num_programs(1)
    n_used = nu_ref[0]
    used = i < n_used
    slot = i % 2

    def start_chunk(tile, c, dst_slot):
        for u in range(chunk):
            row = c * chunk + u
            _row_copy(f_ref, src_ref[tile * tm + row], xin_ref.at[dst_slot], row, sem.at[dst_slot]).start()

    def wait_tile(dst_slot):
        def drain(c, carry):
            for u in range(chunk):
                _row_copy(f_ref, 0, xin_ref.at[dst_slot], 0, sem.at[dst_slot]).wait()
            return carry
        lax.fori_loop(0, nk, drain, 0)

    @pl.when((i == 0) & (k == 0))
    def _():
        def first(c, carry):
            start_chunk(0, c, 0)
            return carry
        lax.fori_loop(0, nk, first, 0)

    @pl.when((k == 0) & (i <= n_used))
    def _():
        wait_tile(slot)
        xb_ref[...] = xin_ref[slot, :tm, :].astype(BF16)

    @pl.when(k == 0)
    def _():
        o_ref[...] = jnp.zeros(o_ref.shape, F32)

    @pl.when(used)
    def _():
        start_chunk(i + 1, k, 1 - slot)
        x = xb_ref[...]
        hid = jax.nn.silu(_dot(x, w1_ref[0])) * _dot(x, w3_ref[0])
        o_ref[...] += _dot(hid.astype(BF16), w2_ref[0])

    @pl.when(used & (i == nt - 1) & (k == nk - 1))
    def _():
        wait_tile(1 - slot)


def moe_ffn(tile_expert, n_used, src, f, w1, w3, w2, rows, tm=MOE_TM, tf=256):
    n, d = f.shape
    ff = w1.shape[2]
    tf = min(tf, ff)
    nk = ff // tf
    chunk = pl.cdiv(tm, nk)
    buf_rows = ((chunk * nk + 7) // 8) * 8
    kk = lambda i, k, nu: jnp.where(i < nu[0], k, nk - 1)
    return pl.pallas_call(
        functools.partial(_moe_ffn_kernel, tm=tm, chunk=chunk),
        grid_spec=pltpu.PrefetchScalarGridSpec(
            num_scalar_prefetch=3,
            grid=(rows // tm, nk),
            in_specs=[
                pl.BlockSpec(memory_space=pl.ANY),
                pl.BlockSpec((1, d, tf), lambda i, k, te, nu, src: (te[i], 0, kk(i, k, nu))),
                pl.BlockSpec((1, d, tf), lambda i, k, te, nu, src: (te[i], 0, kk(i, k, nu))),
                pl.BlockSpec((1, tf, d), lambda i, k, te, nu, src: (te[i], kk(i, k, nu), 0)),
            ],
            out_specs=pl.BlockSpec((tm, d), lambda i, k, te, nu, src: (i, 0)),
            scratch_shapes=[pltpu.VMEM((2, buf_rows, d), F32), pltpu.VMEM((tm, d), BF16),
                            pltpu.SemaphoreType.DMA((2,))],
        ),
        out_shape=jax.ShapeDtypeStruct((rows, d), F32),
        compiler_params=_cparams("arbitrary", "arbitrary"),
        name="moe_ffn",
    )(tile_expert, n_used, src, f, w1, w3, w2)


def _combine_kernel(*refs, tm, final):
    if final:
        dest_ref, h_ref, rt_ref, yg_ref, fg_ref, o_ref, buf_ref, sem = refs
    else:
        dest_ref, h_ref, rt_ref, yg_ref, o_ref, buf_ref, sem = refs
    i = pl.program_id(0)
    nt = pl.num_programs(0)
    slot = i % 2

    def start_row(tile, j, dst_slot):
        for s in range(TOP_K):
            _row_copy(yg_ref, dest_ref[(tile * tm + j) * TOP_K + s], buf_ref.at[dst_slot, s], j,
                      sem.at[dst_slot]).start()

    def wait_tile(dst_slot):
        def drain(j, carry):
            for s in range(TOP_K):
                _row_copy(yg_ref, 0, buf_ref.at[dst_slot, s], 0, sem.at[dst_slot]).wait()
            return carry
        lax.fori_loop(0, tm, drain, 0, unroll=8)

    @pl.when(i == 0)
    def _():
        def first(j, carry):
            start_row(0, j, 0)
            return carry
        lax.fori_loop(0, tm, first, 0, unroll=8)

    wait_tile(slot)
    for j in range(tm):
        start_row(i + 1, j, 1 - slot)

    rt = rt_ref[...]
    y = (h_ref[...] + rt[:, ROUTE_G1:ROUTE_G1 + 1] * buf_ref[slot, 0]
         + rt[:, ROUTE_G2:ROUTE_G2 + 1] * buf_ref[slot, 1])
    if final:
        ms = jnp.mean(y * y, axis=-1, keepdims=True)
        y = y * lax.rsqrt(ms + EPS) * fg_ref[...]
    o_ref[...] = y

    @pl.when(i == nt - 1)
    def _():
        wait_tile(1 - slot)


def moe_combine(dest, h2, route, yg, final_g=None, tm=128):
    n, d = h2.shape
    tm = min(tm, n)
    dest = jnp.pad(dest, (0, tm * TOP_K))
    in_specs = [
        pl.BlockSpec((tm, d), lambda i, dest: (i, 0)),
        pl.BlockSpec((tm, LANES), lambda i, dest: (i, 0)),
        pl.BlockSpec(memory_space=pl.ANY),
    ]
    args = [dest, h2, route, yg]
    if final_g is not None:
        in_specs.append(pl.BlockSpec((1, d), lambda i, dest: (0, 0)))
        args.append(final_g)
    return pl.pallas_call(
        functools.partial(_combine_kernel, tm=tm, final=final_g is not None),
        grid_spec=pltpu.PrefetchScalarGridSpec(
            num_scalar_prefetch=1,
            grid=(n // tm,),
            in_specs=in_specs,
            out_specs=pl.BlockSpec((tm, d), lambda i, dest: (i, 0)),
            scratch_shapes=[pltpu.VMEM((2, TOP_K, tm, d), F32), pltpu.SemaphoreType.DMA((2,))],
        ),
        out_shape=jax.ShapeDtypeStruct((n, d), F32),
        compiler_params=_cparams("arbitrary"),
        name="moe_combine",
    )(*args)


def _routing_tables(route, tm):
    n = route.shape[0]
    flat_e = route[:, ROUTE_E1:ROUTE_E2 + 1].astype(jnp.int32).reshape(-1)
    onehot = (flat_e[:, None] == jnp.arange(N_EXPERTS, dtype=jnp.int32)[None, :]).astype(jnp.int32)
    csum = jnp.cumsum(onehot, axis=0)
    rank = jnp.sum(csum * onehot, axis=1) - 1
    counts = csum[-1]
    padded = ((counts + tm - 1) // tm) * tm
    off_end = jnp.cumsum(padded)
    dest = (off_end - padded)[flat_e] + rank
    n_tiles = (n * TOP_K) // tm + N_EXPERTS
    n_used = (off_end[-1] // tm).astype(jnp.int32)
    tile_start = jnp.arange(n_tiles, dtype=jnp.int32) * tm
    tile_start = jnp.minimum(tile_start, off_end[-1] - tm)
    tile_expert = jnp.sum((tile_start[:, None] >= off_end[None, :]).astype(jnp.int32), axis=1)
    rows = n_tiles * tm
    src = jnp.zeros((rows + tm + 8,), jnp.int32).at[dest].set(jnp.arange(n * TOP_K, dtype=jnp.int32) // TOP_K)
    return dest.astype(jnp.int32), src, tile_expert.astype(jnp.int32), n_used.reshape(1), rows


def moe_layer(h2, f, route, w1, w3, w2, final_g=None):
    dest, src, tile_expert, n_used, rows = _routing_tables(route, MOE_TM)
    yg = moe_ffn(tile_expert, n_used, src, f, w1, w3, w2, rows)
    return moe_combine(dest, h2, route, yg, final_g=final_g)


def _reorder_w_in(w_in):
    d = w_in.shape[0]
    kvw = NSA_KV_WIDTH
    o_gate = NSA_WIDTH + 6 * kvw
    o_uv = o_gate + 3 * NSA_HEADS
    o_ret = o_uv + 2 * GMLP_WIDTH
    gate = w_in[:, o_gate:o_uv].reshape(d, NSA_KV_HEADS, 3 * NSA_GROUP)
    gate = jnp.pad(gate, ((0, 0), (0, 0), (0, LANES - 3 * NSA_GROUP))).reshape(d, NSA_KV_HEADS * LANES)
    parts = [w_in[:, :o_gate], w_in[:, o_uv:], gate]
    w = jnp.concatenate(parts, axis=1)
    w = jnp.pad(w, ((0, 0), (0, PROJ_COLS - w.shape[1])))
    scale = np.ones((1, PROJ_COLS), np.float32)
    scale[:, COL_Q * LANES:(COL_Q + NSA_HEADS) * LANES] = HEAD_DIM ** -0.5
    scale[:, COL_RK * LANES:(COL_RK + RET_HEADS) * LANES] = HEAD_DIM ** -0.5
    del o_ret
    return w.astype(BF16), jnp.asarray(scale)


def _key_extra_lanes(seq):
    pos = np.arange(seq)
    kx = np.zeros((seq, LANES), np.float32)
    kx[pos, pos // SLC_LEN] = 1.0
    kx[:, XL_HI] = pos // SLC_LEN
    kx[:, XL_LO] = pos % SLC_LEN
    kx[:, XL_ONE_A] = 1.0
    kx[:, XL_ONE_B] = 1.0
    return jnp.asarray(kx, BF16)


def _share_t(seq):
    nrow = seq // CMP_STRIDE
    nb = seq // SLC_LEN
    c0 = np.arange(nrow)[None, :] * CMP_STRIDE
    s0 = np.arange(nb)[:, None] * SLC_LEN
    overlap = np.minimum(c0 + CMP_LEN, s0 + SLC_LEN) - np.maximum(c0, s0)
    share = np.clip(overlap, 0, CMP_LEN).astype(np.float32) / CMP_LEN
    share[:, nrow - 1] = 0.0
    return jnp.asarray(share, BF16)


def _retention_tables():
    hh = np.arange(RET_HEADS, dtype=np.float64)
    lg = np.log1p(-np.exp2(-5.0 - hh))
    nn = np.arange(RET_CHUNK, dtype=np.float64)
    rel = nn[:, None] - nn[None, :]
    dec = np.where(rel >= 0, np.exp(lg[:, None, None] * np.maximum(rel, 0.0)), 0.0)
    zeta = np.exp(lg[:, None] * (RET_CHUNK - 1.0 - nn))[:, :, None]
    xi = np.exp(lg[:, None] * (nn + 1.0))[:, :, None]
    dch = np.broadcast_to(np.exp(lg * RET_CHUNK)[:, None, None], (RET_HEADS, 1, HEAD_DIM))
    return tuple(jnp.asarray(a, F32) for a in (dec, zeta, xi, dch))


def _token_mixer(h2, batch, seq, ln1_g, w_in, pe_k, w1_k, w2_k, pe_v, w1_v, w2_v,
                 g_ln_g, g_ln_b, g_ws, g_bs, ret_gn_g, consts):
    kx, sht, ret_tabs, cmp_tabs, axq, wbias = consts
    w_r, colscale = _reorder_w_in(w_in)
    proj = in_proj(h2, ln1_g[None, :], w_r, colscale)
    nrow = seq // CMP_STRIDE
    kv = proj[:, COL_KCMP * LANES:(COL_VCMP + NSA_KV_HEADS) * LANES]
    kv = kv.reshape(batch, nrow, CMP_STRIDE, 2, NSA_KV_HEADS, HEAD_DIM)
    rows = kv.transpose(3, 0, 4, 1, 2, 5).reshape(2, batch * NSA_KV_HEADS, nrow, CMP_STRIDE * HEAD_DIM)
    pe = jnp.stack([pe_k, pe_v]).reshape(2, 1, CMP_LEN * HEAD_DIM)
    kvc = compress(rows, pe, jnp.stack([w1_k, w1_v]).astype(BF16), jnp.stack([w2_k, w2_v]).astype(BF16))
    o_cmp, xsel, counts = cmp_attn(proj, kvc, sht, cmp_tabs, batch, seq)
    o_slc = sel_attn(proj, xsel, counts, kx, axq, batch, seq)
    o_nsa = win_attn(proj, kx, axq, wbias, o_cmp, o_slc, batch, seq)
    bsx = jnp.repeat(g_bs.T, HEAD_DIM, axis=1)
    o_gm = gmlp(proj, g_ln_g[None, :], g_ln_b[None, :], g_ws, bsx)
    o_ret = retention(proj, ret_tabs, ret_gn_g[None, :], batch, seq)
    return o_nsa, o_gm, o_ret


def kernel(x, ln1_g, w_in, cmp_pe_k, cmp_w1_k, cmp_w2_k, cmp_pe_v, cmp_w1_v, cmp_w2_v, gmlp_ln_g, gmlp_ln_b, gmlp_ws, gmlp_bs, ret_gn_g, w_out, ln2_g, ffn_w1, ffn_w3, ffn_w2, moe_wr, moe_br, moe_w1, moe_w3, moe_w2, final_g):
    batch, seq, d = x.shape
    depth = w_in.shape[0]
    consts = (_key_extra_lanes(seq), _share_t(seq), _retention_tables(), _cmp_tables(seq),
              _alibi_q_table(seq), _window_bias(WIN_TQ))
    h2 = x.reshape(batch * seq, d)
    for layer in range(depth):
        o_nsa, o_gm, o_ret = _token_mixer(
            h2, batch, seq, ln1_g[layer], w_in[layer], cmp_pe_k[layer], cmp_w1_k[layer], cmp_w2_k[layer],
            cmp_pe_v[layer], cmp_w1_v[layer], cmp_w2_v[layer], gmlp_ln_g[layer], gmlp_ln_b[layer],
            gmlp_ws[layer], gmlp_bs[layer], ret_gn_g[layer], consts)
        fin = final_g[None, :] if layer == depth - 1 else None
        i = layer // 2
        w_o = w_out[layer].astype(BF16)
        if layer % 2 == 0:
            h2, f = out_proj(o_nsa, o_gm, o_ret, h2, w_o, ln2_g[layer][None, :])
            h2 = ffn(f, h2, ffn_w1[i].astype(BF16), ffn_w3[i].astype(BF16), ffn_w2[i].astype(BF16),
                     final_g=fin)
        else:
            wr = jnp.pad(moe_wr[i], ((0, 0), (0, LANES - N_EXPERTS)))
            wr_hi = wr.astype(BF16)
            wr_lo = (wr - wr_hi.astype(F32)).astype(BF16)
            br = jnp.pad(moe_br[i], (0, LANES - N_EXPERTS))[None, :]
            h2, f, route = out_proj(o_nsa, o_gm, o_ret, h2, w_o, ln2_g[layer][None, :],
                                    router_w=(wr_hi, wr_lo, br))
            h2 = moe_layer(h2, f, route, moe_w1[i].astype(BF16), moe_w3[i].astype(BF16),
                           moe_w2[i].astype(BF16), final_g=fin)
    return h2.reshape(batch, seq, d)
```

```python
import functools

import numpy as np
import jax
import jax.numpy as jnp
from jax import lax
from jax.experimental import pallas as pl
from jax.experimental.pallas import tpu as pltpu

F32 = jnp.float32
BF16 = jnp.bfloat16

HEAD_DIM = 128
NSA_HEADS = 8
NSA_KV_HEADS = 2
NSA_GROUP = NSA_HEADS // NSA_KV_HEADS
NSA_WIDTH = NSA_HEADS * HEAD_DIM
NSA_KV_WIDTH = NSA_KV_HEADS * HEAD_DIM
CMP_LEN = 32
CMP_STRIDE = 16
SLC_LEN = 64
SLC_TOPK = 16
WIN = 512
GMLP_GROUPS = 4
GMLP_CHUNK = 128
GMLP_WIDTH = GMLP_GROUPS * HEAD_DIM
RET_HEADS = 4
RET_CHUNK = 128
RET_WIDTH = RET_HEADS * HEAD_DIM
N_EXPERTS = 8
TOP_K = 2
EPS = 1e-6
NEG_INF = -1e30
FORCE_SCORE = 1e4

LANES = 128
VMEM_LIMIT = 56 * 1024 * 1024

COL_Q = 0
COL_KCMP = 8
COL_VCMP = 10
COL_KSLC = 12
COL_VSLC = 14
COL_KWIN = 16
COL_VWIN = 18
COL_U = 20
COL_V = 24
COL_RQ = 28
COL_RK = 32
COL_RV = 36
COL_RG = 40
COL_GATE = 44
PROJ_COLS = 48 * LANES

XL_HI = 64
XL_LO = 65
XL_ONE_A = 66
XL_ONE_B = 67


def _cparams(*sem):
    return pltpu.CompilerParams(dimension_semantics=sem, vmem_limit_bytes=VMEM_LIMIT)


def _nt_dot(a, b):
    return lax.dot_general(a, b, (((1,), (1,)), ((), ())), preferred_element_type=F32)


def _dot(a, b):
    return jnp.dot(a, b, preferred_element_type=F32)


def _in_proj_kernel(x_ref, g_ref, w_ref, cs_ref, o_ref, xn_ref):
    @pl.when(pl.program_id(1) == 0)
    def _():
        x = x_ref[...]
        ms = jnp.mean(x * x, axis=-1, keepdims=True)
        xn_ref[...] = (x * lax.rsqrt(ms + EPS) * g_ref[...]).astype(BF16)

    acc = _dot(xn_ref[...], w_ref[...])
    o_ref[...] = (acc * cs_ref[...]).astype(o_ref.dtype)


def in_proj(h2, g, w, colscale, tm=1024, tn=1024):
    n, d = h2.shape
    cols = w.shape[1]
    tm = min(tm, n)
    return pl.pallas_call(
        _in_proj_kernel,
        grid=(n // tm, cols // tn),
        in_specs=[
            pl.BlockSpec((tm, d), lambda i, j: (i, 0)),
            pl.BlockSpec((1, d), lambda i, j: (0, 0)),
            pl.BlockSpec((d, tn), lambda i, j: (0, j)),
            pl.BlockSpec((1, tn), lambda i, j: (0, j)),
        ],
        out_specs=pl.BlockSpec((tm, tn), lambda i, j: (i, j)),
        out_shape=jax.ShapeDtypeStruct((n, cols), BF16),
        scratch_shapes=[pltpu.VMEM((tm, d), BF16)],
        compiler_params=_cparams("parallel", "arbitrary"),
        name="in_proj",
    )(h2, g, w, colscale)


def _compress_kernel(r_ref, pe_ref, w1_ref, w2_ref, o_ref):
    half = CMP_STRIDE * HEAD_DIM
    r = r_ref[0, 0].astype(F32)
    nrow = r.shape[0]
    pe = pe_ref[0]
    a = _dot((r + pe[:, :half]).astype(BF16), w1_ref[0, :half, :])
    b = _dot((r + pe[:, half:]).astype(BF16), w1_ref[0, half:, :])
    pre = a + pltpu.roll(b, nrow - 1, 0)
    y = _dot(jax.nn.gelu(pre).astype(BF16), w2_ref[0])
    row = lax.broadcasted_iota(jnp.int32, y.shape, 0)
    o_ref[0, 0] = jnp.where(row < nrow - 1, y, 0.0).astype(o_ref.dtype)


def compress(rows, pe, w1, w2):
    _, bg, nrow, width = rows.shape
    return pl.pallas_call(
        _compress_kernel,
        grid=(2, bg),
        in_specs=[
            pl.BlockSpec((1, 1, nrow, width), lambda s, i: (s, i, 0, 0)),
            pl.BlockSpec((1, 1, 2 * width), lambda s, i: (s, 0, 0)),
            pl.BlockSpec((1, 2 * width, HEAD_DIM), lambda s, i: (s, 0, 0)),
            pl.BlockSpec((1, HEAD_DIM, HEAD_DIM), lambda s, i: (s, 0, 0)),
        ],
        out_specs=pl.BlockSpec((1, 1, nrow, HEAD_DIM), lambda s, i: (s, i, 0, 0)),
        out_shape=jax.ShapeDtypeStruct((2, bg, nrow, HEAD_DIM), BF16),
        compiler_params=_cparams("arbitrary", "arbitrary"),
        name="compress",
    )(rows, pe, w1, w2)


def _group_slope(g, r):
    return jnp.where(g == 0, F32(2.0 ** -(r + 1)), F32(2.0 ** -(r + 1 + NSA_GROUP)))


SEL_TQ = 256
SEL_TK = 512
SEL_TILE_ROWS = 8


def _cmp_attn_kernel(q_ref, kc_ref, vc_ref, sht_ref, dist_ref, cb_ref, grp_ref, o_ref, x_ref, c_ref,
                     *, tq, nb, topk):
    g = pl.program_id(0) % NSA_KV_HEADS
    t0 = pl.program_id(1) * tq
    kc = kc_ref[0, 0]
    vc = vc_ref[0, 0]
    nrow = kc.shape[0]
    distf = dist_ref[...]
    cbias = cb_ref[...]
    row_ok = jnp.where(t0 + lax.broadcasted_iota(jnp.int32, (tq, 1), 0) >= CMP_LEN - 1, 1.0, 0.0)
    psum = jnp.zeros((tq, nrow), F32)
    for r in range(NSA_GROUP):
        q = q_ref[:, r * HEAD_DIM:(r + 1) * HEAD_DIM]
        s = _nt_dot(q, kc) - _group_slope(g, r) * distf + cbias
        e = jnp.exp(s - jnp.max(s, axis=-1, keepdims=True))
        p = e * (row_ok / jnp.sum(e, axis=-1, keepdims=True))
        o_ref[:, r * HEAD_DIM:(r + 1) * HEAD_DIM] = _dot(p.astype(BF16), vc).astype(o_ref.dtype)
        psum = psum + p
    p_hi = psum.astype(BF16)
    p_lo = (psum - p_hi.astype(F32)).astype(BF16)
    sht = sht_ref[...]
    imp = _nt_dot(sht, p_hi) + _nt_dot(sht, p_lo)
    j = lax.broadcasted_iota(jnp.int32, (nb, tq), 0)
    cur = (t0 + lax.broadcasted_iota(jnp.int32, (nb, tq), 1)) // SLC_LEN
    forced = (j == 0) | (j == cur) | (j == cur - 1)
    score = jnp.where(forced, FORCE_SCORE, imp)
    cand = jnp.where(j > cur, -3e38, score)
    jf = j.astype(F32)[:, :LANES]
    groups = [cand[:, c:c + LANES] for c in range(0, tq, LANES)]
    for _ in range(topk):
        for n, cg in enumerate(groups):
            best = jnp.max(cg, axis=0, keepdims=True)
            idx = jnp.min(jnp.where(cg == best, jf, float(nb)), axis=0, keepdims=True)
            groups[n] = jnp.where(jf == idx, -jnp.inf, cg)
    cand = jnp.concatenate(groups, axis=1)
    picked = cand == -jnp.inf
    grp = grp_ref[...]
    cnt = _dot(grp, jnp.where(picked, 1.0, 0.0).astype(BF16))
    c_ref[0] = jnp.broadcast_to(jnp.sum(cnt, axis=1, keepdims=True), c_ref.shape[1:])
    neg = jnp.where(picked, 0.0, NEG_INF)
    neg = jnp.concatenate([neg, jnp.zeros((LANES - nb, tq), F32)], axis=0)
    x_ref[...] = neg.T.astype(x_ref.dtype)


def _cmp_tables(seq):
    nrow = seq // CMP_STRIDE
    t = np.arange(seq)[:, None]
    c = np.arange(nrow)[None, :]
    dist = t - (c * CMP_STRIDE + (CMP_LEN - 1))
    valid = (dist >= 0) & (c < nrow - 1)
    return jnp.asarray(dist, F32), jnp.asarray(np.where(valid, 0.0, NEG_INF), F32)


def cmp_attn(proj, kvc, sht, tables, batch, seq, tq=256):
    n = proj.shape[0]
    nb = seq // SLC_LEN
    bg = batch * NSA_KV_HEADS
    nrow = kvc.shape[2]
    qt = seq // tq
    rowblk = lambda i, t: ((i // NSA_KV_HEADS) * qt + t, i % NSA_KV_HEADS)
    kern = functools.partial(_cmp_attn_kernel, tq=tq, nb=nb, topk=min(SLC_TOPK, nb))
    return pl.pallas_call(
        kern,
        grid=(bg, qt),
        in_specs=[
            pl.BlockSpec((tq, NSA_GROUP * HEAD_DIM), rowblk),
            pl.BlockSpec((1, 1, nrow, HEAD_DIM), lambda i, t: (0, i, 0, 0)),
            pl.BlockSpec((1, 1, nrow, HEAD_DIM), lambda i, t: (1, i, 0, 0)),
            pl.BlockSpec((nb, nrow), lambda i, t: (0, 0)),
            pl.BlockSpec((tq, nrow), lambda i, t: (t, 0)),
            pl.BlockSpec((tq, nrow), lambda i, t: (t, 0)),
            pl.BlockSpec((SEL_TILE_ROWS, nb), lambda i, t: (0, 0)),
        ],
        out_specs=[
            pl.BlockSpec((tq, NSA_GROUP * HEAD_DIM), rowblk),
            pl.BlockSpec((tq, LANES), rowblk),
            pl.BlockSpec((1, SEL_TILE_ROWS, LANES), lambda i, t: (i * qt + t, 0, 0)),
        ],
        out_shape=[
            jax.ShapeDtypeStruct((n, NSA_WIDTH), BF16),
            jax.ShapeDtypeStruct((n, NSA_KV_HEADS * LANES), BF16),
            jax.ShapeDtypeStruct((bg * qt, SEL_TILE_ROWS, LANES), F32),
        ],
        compiler_params=_cparams("parallel", "parallel"),
        name="cmp_attn",
    )(proj, kvc, kvc, sht, *tables, _tile_groups(seq))


def _tile_groups(seq):
    nb = seq // SLC_LEN
    assert seq // SEL_TK <= SEL_TILE_ROWS
    grp = np.zeros((SEL_TILE_ROWS, nb), np.float32)
    grp[np.arange(nb) // (SEL_TK // SLC_LEN), np.arange(nb)] = 1.0
    return jnp.asarray(grp, BF16)


def _active_key_tiles(counts, batch, seq, tq, tk):
    qt = seq // tq
    n_full = ((jnp.arange(qt, dtype=jnp.int32) * tq + tq - 1) // tk)[None, :, None]
    kt = jnp.arange(SEL_TILE_ROWS, dtype=jnp.int32)[None, None, :]
    cnt = counts[:, :, 0].reshape(batch * NSA_KV_HEADS, qt, SEL_TILE_ROWS)
    active = (cnt > 0) & (kt < n_full)
    order = jnp.argsort(jnp.where(active, 0, 1), axis=-1, stable=True).astype(jnp.int32)
    return order.reshape(-1), jnp.sum(active, axis=-1).astype(jnp.int32).reshape(-1)


def _alibi_q_table(seq):
    t = np.arange(seq)
    tab = np.zeros((NSA_HEADS, seq, LANES), np.float32)
    for h in range(NSA_HEADS):
        slope = 2.0 ** (-8.0 * (h + 1) / NSA_HEADS)
        tab[h, :, XL_HI] = slope * SLC_LEN
        tab[h, :, XL_LO] = slope
        tab[h, :, XL_ONE_A] = -(slope * SLC_LEN) * (t // SLC_LEN)
        tab[h, :, XL_ONE_B] = -slope * (t % SLC_LEN)
    return jnp.asarray(tab, BF16)


def _sel_attn_kernel(act_ref, nact_ref, q_ref, x_ref, ax_ref, k_ref, v_ref, kx_ref, o_ref,
                     q2_ref, sa_ref, sb_ref, m_ref, acc_ref, *, tq, tk):
    t0 = pl.program_id(1) * tq
    n_kt = (t0 + tq - 1) // tk + 1
    rows = NSA_GROUP * tq
    half = rows // 2
    tpos = t0 + (lax.broadcasted_iota(jnp.int32, (half, tk), 0) & (tq - 1))
    kcol = lax.broadcasted_iota(jnp.int32, (half, tk), 1)

    xsel = x_ref[...]
    q2_ref[...] = jnp.concatenate([
        jnp.concatenate([q_ref[:, r * HEAD_DIM:(r + 1) * HEAD_DIM], xsel + ax_ref[r]], axis=1)
        for r in range(NSA_GROUP)], axis=0)
    m_ref[...] = jnp.full(m_ref.shape, -jnp.inf, F32)
    acc_ref[...] = jnp.zeros(acc_ref.shape, F32)
    ones = jnp.ones((tk, LANES), BF16)

    def scores(kt, dst_ref):
        ks = pl.multiple_of(kt * tk, tk)
        k2 = jnp.concatenate([k_ref[pl.ds(ks, tk), :], kx_ref[pl.ds(ks, tk), :]], axis=1)
        dst_ref[...] = _nt_dot(q2_ref[...], k2)

    def update(kt, src_ref, causal):
        ks = pl.multiple_of(kt * tk, tk)
        v2 = jnp.concatenate([v_ref[pl.ds(ks, tk), :], ones], axis=1)
        step = half if causal else rows
        for r0 in range(0, rows, step):
            rs = slice(r0, r0 + step)
            s = src_ref[rs, :]
            if causal:
                s = jnp.where(ks + kcol <= tpos, s, NEG_INF)
            m_prev = m_ref[rs, :]
            m_new = jnp.maximum(m_prev, jnp.max(s, axis=-1, keepdims=True))
            alpha = jnp.exp(m_prev - m_new)
            p = jnp.exp(s - jnp.concatenate([m_new] * (tk // LANES), axis=1))
            acc_ref[rs, :] = (jnp.concatenate([alpha, alpha], axis=1) * acc_ref[rs, :]
                              + _dot(p.astype(BF16), v2))
            m_ref[rs, :] = m_new

    step_id = pl.program_id(0) * pl.num_programs(1) + pl.program_id(1)
    n_act = nact_ref[step_id]
    diag = n_kt - 1

    def visited(j):
        return jnp.where(j < n_act, act_ref[step_id * SEL_TILE_ROWS + jnp.minimum(j, SEL_TILE_ROWS - 1)], diag)

    scores(visited(0), sa_ref)

    def body(j, carry):
        scores(visited(2 * j + 1), sb_ref)
        update(visited(2 * j), sa_ref, False)
        scores(visited(2 * j + 2), sa_ref)
        update(visited(2 * j + 1), sb_ref, False)
        return carry

    lax.fori_loop(0, n_act // 2, body, 0)

    @pl.when(n_act % 2 == 1)
    def _():
        scores(diag, sb_ref)
        update(visited(n_act - 1), sa_ref, False)
        update(diag, sb_ref, True)

    @pl.when(n_act % 2 == 0)
    def _():
        update(diag, sa_ref, True)

    acc = acc_ref[...]
    o = acc[:, :HEAD_DIM] / acc[:, HEAD_DIM:]
    for r in range(NSA_GROUP):
        o_ref[:, r * HEAD_DIM:(r + 1) * HEAD_DIM] = o[r * tq:(r + 1) * tq].astype(o_ref.dtype)


def sel_attn(proj, xsel, counts, kx, axq, batch, seq, tq=SEL_TQ, tk=SEL_TK):
    n = proj.shape[0]
    bg = batch * NSA_KV_HEADS
    qt = seq // tq
    act, n_act = _active_key_tiles(counts, batch, seq, tq, tk)
    rowblk = lambda i, t, a, na: ((i // NSA_KV_HEADS) * qt + t, i % NSA_KV_HEADS)
    kern = functools.partial(_sel_attn_kernel, tq=tq, tk=tk)
    return pl.pallas_call(
        kern,
        grid_spec=pltpu.PrefetchScalarGridSpec(
            num_scalar_prefetch=2,
            grid=(bg, qt),
            in_specs=[
                pl.BlockSpec((tq, NSA_GROUP * HEAD_DIM), rowblk),
                pl.BlockSpec((tq, LANES), rowblk),
                pl.BlockSpec((NSA_GROUP, tq, LANES), lambda i, t, a, na: (i % NSA_KV_HEADS, t, 0)),
                pl.BlockSpec((seq, HEAD_DIM),
                             lambda i, t, a, na: (i // NSA_KV_HEADS, COL_KSLC + i % NSA_KV_HEADS)),
                pl.BlockSpec((seq, HEAD_DIM),
                             lambda i, t, a, na: (i // NSA_KV_HEADS, COL_VSLC + i % NSA_KV_HEADS)),
                pl.BlockSpec((seq, LANES), lambda i, t, a, na: (0, 0)),
            ],
            out_specs=pl.BlockSpec((tq, NSA_GROUP * HEAD_DIM), rowblk),
            scratch_shapes=[
                pltpu.VMEM((NSA_GROUP * tq, 2 * HEAD_DIM), BF16),
                pltpu.VMEM((NSA_GROUP * tq, tk), F32),
                pltpu.VMEM((NSA_GROUP * tq, tk), F32),
                pltpu.VMEM((NSA_GROUP * tq, LANES), F32),
                pltpu.VMEM((NSA_GROUP * tq, 2 * HEAD_DIM), F32),
            ],
        ),
        out_shape=jax.ShapeDtypeStruct((n, NSA_WIDTH), BF16),
        compiler_params=_cparams("parallel", "parallel"),
        name="sel_attn",
    )(act, n_act, proj, xsel, axq, proj, proj, kx)


def _win_attn_kernel(q_ref, ax_ref, wb_ref, k_ref, v_ref, kx_ref, gl_ref, oc_ref, os_ref, o_ref, *, tq):
    t0 = pl.program_id(1) * tq
    span = WIN + tq
    starts = [pl.multiple_of(jnp.maximum(t0 - WIN + c * tq, 0), LANES) for c in range(span // tq)]
    k2 = jnp.concatenate([
        jnp.concatenate([k_ref[pl.ds(st, tq), :], kx_ref[pl.ds(st, tq), :]], axis=1)
        for st in starts], axis=0)
    v2 = jnp.concatenate([
        jnp.concatenate([v_ref[pl.ds(st, tq), :] for st in starts], axis=0),
        jnp.ones((span, LANES), BF16)], axis=1)
    bias = wb_ref[0]
    gates = jax.nn.sigmoid(gl_ref[...].astype(F32))
    pair = 1
    for h0 in range(0, NSA_GROUP, pair):
        heads = range(h0, h0 + pair)
        q2 = jnp.concatenate([
            jnp.concatenate([q_ref[:, r * HEAD_DIM:(r + 1) * HEAD_DIM], ax_ref[r]], axis=1)
            for r in heads], axis=0)
        s = _nt_dot(q2, k2)
        parts = []
        for n, r in enumerate(heads):
            sr = s[n * tq:(n + 1) * tq] + bias
            parts.append(jnp.exp(sr - jnp.max(sr, axis=-1, keepdims=True)).astype(BF16))
        ov = _dot(jnp.concatenate(parts, axis=0), v2)
        o_win = ov[:, :HEAD_DIM] / ov[:, HEAD_DIM:]
        for n, r in enumerate(heads):
            cols = slice(r * HEAD_DIM, (r + 1) * HEAD_DIM)
            o = (gates[:, 3 * r:3 * r + 1] * oc_ref[:, cols].astype(F32)
                 + gates[:, 3 * r + 1:3 * r + 2] * os_ref[:, cols].astype(F32)
                 + gates[:, 3 * r + 2:3 * r + 3] * o_win[n * tq:(n + 1) * tq])
            o_ref[:, cols] = o.astype(o_ref.dtype)


WIN_TQ = 256


def _window_bias(tq):
    nvar = WIN // tq + 1
    r = np.arange(tq)[None, :, None]
    c = np.arange(WIN + tq)[None, None, :]
    t0 = (np.arange(nvar) * tq)[:, None, None]
    kpos = t0 - WIN + c
    dist = (t0 + r) - kpos
    ok = (kpos >= 0) & (dist >= 0) & (dist < WIN)
    return jnp.asarray(np.where(ok, 0.0, NEG_INF), F32)


def win_attn(proj, kx, axq, wbias, o_cmp, o_slc, batch, seq, tq=WIN_TQ):
    n = proj.shape[0]
    bg = batch * NSA_KV_HEADS
    qt = seq // tq
    nvar = wbias.shape[0]
    rowblk = lambda i, t: ((i // NSA_KV_HEADS) * qt + t, i % NSA_KV_HEADS)
    kern = functools.partial(_win_attn_kernel, tq=tq)
    return pl.pallas_call(
        kern,
        grid=(bg, qt),
        in_specs=[
            pl.BlockSpec((tq, NSA_GROUP * HEAD_DIM), rowblk),
            pl.BlockSpec((NSA_GROUP, tq, LANES), lambda i, t: (i % NSA_KV_HEADS, t, 0)),
            pl.BlockSpec((1, tq, WIN + tq), lambda i, t: (jnp.minimum(t, nvar - 1), 0, 0)),
            pl.BlockSpec((seq, HEAD_DIM), lambda i, t: (i // NSA_KV_HEADS, COL_KWIN + i % NSA_KV_HEADS)),
            pl.BlockSpec((seq, HEAD_DIM), lambda i, t: (i // NSA_KV_HEADS, COL_VWIN + i % NSA_KV_HEADS)),
            pl.BlockSpec((seq, LANES), lambda i, t: (0, 0)),
            pl.BlockSpec((tq, LANES), lambda i, t: ((i // NSA_KV_HEADS) * qt + t, COL_GATE + i % NSA_KV_HEADS)),
            pl.BlockSpec((tq, NSA_GROUP * HEAD_DIM), rowblk),
            pl.BlockSpec((tq, NSA_GROUP * HEAD_DIM), rowblk),
        ],
        out_specs=pl.BlockSpec((tq, NSA_GROUP * HEAD_DIM), rowblk),
        out_shape=jax.ShapeDtypeStruct((n, NSA_WIDTH), BF16),
        compiler_params=_cparams("parallel", "parallel"),
        name="win_attn",
    )(proj, axq, wbias, proj, proj, kx, proj, o_cmp, o_slc)


def _gmlp_kernel(u_ref, v_ref, lg_ref, lb_ref, ws_ref, bsx_ref, o_ref, *, tm):
    u = jax.nn.gelu(u_ref[...].astype(F32))
    v = jax.nn.gelu(v_ref[...].astype(F32))
    mu = jnp.mean(v, axis=-1, keepdims=True)
    vc = v - mu
    var = jnp.mean(vc * vc, axis=-1, keepdims=True)
    vn = (vc * lax.rsqrt(var + EPS) * lg_ref[...] + lb_ref[...]).astype(BF16)
    c = GMLP_CHUNK
    tri = (lax.broadcasted_iota(jnp.int32, (c, c), 0) >= lax.broadcasted_iota(jnp.int32, (c, c), 1))
    bsx = bsx_ref[...]
    for gi in range(GMLP_GROUPS):
        w = jnp.where(tri, ws_ref[gi], 0.0).astype(BF16)
        cols = slice(gi * HEAD_DIM, (gi + 1) * HEAD_DIM)
        for ci in range(tm // c):
            rows = slice(ci * c, (ci + 1) * c)
            s = _dot(w, vn[rows, cols]) + bsx[:, cols]
            o_ref[rows, cols] = (u[rows, cols] * s).astype(o_ref.dtype)


def gmlp(proj, ln_g, ln_b, ws, bsx, tm=512):
    n = proj.shape[0]
    ublk = GMLP_WIDTH // LANES
    kern = functools.partial(_gmlp_kernel, tm=tm)
    return pl.pallas_call(
        kern,
        grid=(n // tm,),
        in_specs=[
            pl.BlockSpec((tm, GMLP_WIDTH), lambda i: (i, COL_U // ublk)),
            pl.BlockSpec((tm, GMLP_WIDTH), lambda i: (i, COL_V // ublk)),
            pl.BlockSpec((1, GMLP_WIDTH), lambda i: (0, 0)),
            pl.BlockSpec((1, GMLP_WIDTH), lambda i: (0, 0)),
            pl.BlockSpec((GMLP_GROUPS, GMLP_CHUNK, GMLP_CHUNK), lambda i: (0, 0, 0)),
            pl.BlockSpec((GMLP_CHUNK, GMLP_WIDTH), lambda i: (0, 0)),
        ],
        out_specs=pl.BlockSpec((tm, GMLP_WIDTH), lambda i: (i, 0)),
        out_shape=jax.ShapeDtypeStruct((n, GMLP_WIDTH), BF16),
        compiler_params=_cparams("parallel"),
        name="gmlp",
    )(proj, proj, ln_g, ln_b, ws, bsx)


def _retention_kernel(q_ref, k_ref, v_ref, g_ref, dec_ref, zeta_ref, xi_ref, dch_ref, gn_ref,
                      o_ref, s_ref, st_ref, *, seq):
    c = RET_CHUNK
    nc = seq // c
    heads = [slice(hh * HEAD_DIM, (hh + 1) * HEAD_DIM) for hh in range(RET_HEADS)]

    def kv_body(ci, carry):
        rows = pl.ds(pl.multiple_of(ci * c, c), c)
        for hh, cols in enumerate(heads):
            kz = (k_ref[rows, cols].astype(F32) * zeta_ref[hh]).astype(BF16)
            st_ref[ci, hh] = lax.dot_general(kz, v_ref[rows, cols], (((0,), (0,)), ((), ())),
                                             preferred_element_type=F32)
        return carry

    lax.fori_loop(0, nc, kv_body, 0, unroll=2)

    s_ref[...] = jnp.zeros(s_ref.shape, F32)

    def scan_body(ci, carry):
        for hh in range(RET_HEADS):
            state = s_ref[hh]
            s_ref[hh] = dch_ref[hh] * state + st_ref[ci, hh]
            st_ref[ci, hh] = state
        return carry

    lax.fori_loop(0, nc, scan_body, 0)

    def out_body(ci, carry):
        rows = pl.ds(pl.multiple_of(ci * c, c), c)
        for hh, cols in enumerate(heads):
            q = q_ref[rows, cols]
            v = v_ref[rows, cols]
            scores = _nt_dot(q, k_ref[rows, cols]) * dec_ref[hh]
            y = _dot(scores.astype(BF16), v) + _dot(q, st_ref[ci, hh].astype(BF16)) * xi_ref[hh]
            mu = jnp.mean(y, axis=-1, keepdims=True)
            yc = y - mu
            var = jnp.mean(yc * yc, axis=-1, keepdims=True)
            yn = yc * lax.rsqrt(var + EPS) * gn_ref[:, cols]
            o_ref[rows, cols] = (jax.nn.silu(g_ref[rows, cols].astype(F32)) * yn).astype(o_ref.dtype)
        return carry

    lax.fori_loop(0, nc, out_body, 0, unroll=2)


def retention(proj, tabs, gn_g, batch, seq):
    n = proj.shape[0]
    dec, zeta, xi, dch = tabs
    rblk = RET_WIDTH // LANES
    kern = functools.partial(_retention_kernel, seq=seq)
    full = lambda shape: pl.BlockSpec(shape, lambda b: (0,) * len(shape))
    return pl.pallas_call(
        kern,
        grid=(batch,),
        in_specs=[
            pl.BlockSpec((seq, RET_WIDTH), lambda b: (b, COL_RQ // rblk)),
            pl.BlockSpec((seq, RET_WIDTH), lambda b: (b, COL_RK // rblk)),
            pl.BlockSpec((seq, RET_WIDTH), lambda b: (b, COL_RV // rblk)),
            pl.BlockSpec((seq, RET_WIDTH), lambda b: (b, COL_RG // rblk)),
            full(dec.shape), full(zeta.shape), full(xi.shape), full(dch.shape),
            full((1, RET_WIDTH)),
        ],
        out_specs=pl.BlockSpec((seq, RET_WIDTH), lambda b: (b, 0)),
        out_shape=jax.ShapeDtypeStruct((n, RET_WIDTH), BF16),
        scratch_shapes=[pltpu.VMEM((RET_HEADS, HEAD_DIM, HEAD_DIM), F32),
                        pltpu.VMEM((seq // RET_CHUNK, RET_HEADS, HEAD_DIM, HEAD_DIM), F32)],
        compiler_params=_cparams("parallel"),
        name="retention",
    )(proj, proj, proj, proj, dec, zeta, xi, dch, gn_g)


ROUTE_E1, ROUTE_E2, ROUTE_G1, ROUTE_G2 = 0, 1, 2, 3


def _top2_route(logits):
    lane = lax.broadcasted_iota(jnp.int32, logits.shape, 1)
    lg = jnp.where(lane < N_EXPERTS, logits, -jnp.inf)
    v1 = jnp.max(lg, axis=-1, keepdims=True)
    i1 = jnp.min(jnp.where(lg == v1, lane, LANES), axis=-1, keepdims=True)
    lg2 = jnp.where(lane == i1, -jnp.inf, lg)
    v2 = jnp.max(lg2, axis=-1, keepdims=True)
    i2 = jnp.min(jnp.where(lg2 == v2, lane, LANES), axis=-1, keepdims=True)
    e2 = jnp.exp(v2 - v1)
    den = 1.0 + e2
    out = jnp.where(lane == ROUTE_E1, i1.astype(F32), 0.0)
    out = jnp.where(lane == ROUTE_E2, i2.astype(F32), out)
    out = jnp.where(lane == ROUTE_G1, 1.0 / den, out)
    return jnp.where(lane == ROUTE_G2, e2 / den, out)


def _out_proj_kernel(*refs, router):
    if router:
        nsa_ref, gm_ref, ret_ref, h_ref, w_ref, g2_ref, wrh_ref, wrl_ref, br_ref, ho_ref, f_ref, rt_ref = refs
    else:
        nsa_ref, gm_ref, ret_ref, h_ref, w_ref, g2_ref, ho_ref, f_ref = refs
    acc = _dot(nsa_ref[...], w_ref[:NSA_WIDTH, :])
    acc = acc + _dot(gm_ref[...], w_ref[NSA_WIDTH:NSA_WIDTH + GMLP_WIDTH, :])
    acc = acc + _dot(ret_ref[...], w_ref[NSA_WIDTH + GMLP_WIDTH:, :])
    hn = h_ref[...] + acc
    ho_ref[...] = hn
    ms = jnp.mean(hn * hn, axis=-1, keepdims=True)
    f = hn * lax.rsqrt(ms + EPS) * g2_ref[...]
    f_hi = f.astype(BF16)
    if router:
        f_ref[...] = f
        f_lo = (f - f_hi.astype(F32)).astype(BF16)
        logits = (_dot(f_hi, wrh_ref[...]) + _dot(f_lo, wrh_ref[...]) + _dot(f_hi, wrl_ref[...])
                  + br_ref[...])
        rt_ref[...] = _top2_route(logits)
    else:
        f_ref[...] = f_hi


def out_proj(o_nsa, o_gm, o_ret, h2, w, g2, router_w=None, tm=512):
    n, d = h2.shape
    tm = min(tm, n)
    router = router_w is not None
    row = lambda width: pl.BlockSpec((tm, width), lambda i: (i, 0))
    full = lambda shape: pl.BlockSpec(shape, lambda i: (0,) * len(shape))
    in_specs = [row(NSA_WIDTH), row(GMLP_WIDTH), row(RET_WIDTH), row(d), full(w.shape), full((1, d))]
    args = [o_nsa, o_gm, o_ret, h2, w, g2]
    if router:
        in_specs += [full((d, LANES)), full((d, LANES)), full((1, LANES))]
        args += list(router_w)
        out_specs = [row(d), row(d), row(LANES)]
        out_shape = [jax.ShapeDtypeStruct((n, d), F32), jax.ShapeDtypeStruct((n, d), F32),
                     jax.ShapeDtypeStruct((n, LANES), F32)]
    else:
        out_specs = [row(d), row(d)]
        out_shape = [jax.ShapeDtypeStruct((n, d), F32), jax.ShapeDtypeStruct((n, d), BF16)]
    return pl.pallas_call(
        functools.partial(_out_proj_kernel, router=router),
        grid=(n // tm,),
        in_specs=in_specs,
        out_specs=out_specs,
        out_shape=out_shape,
        compiler_params=_cparams("parallel"),
        name="out_proj_router" if router else "out_proj",
    )(*args)


def _ffn_kernel(*refs, final):
    if final:
        x_ref, h_ref, w1_ref, w3_ref, w2_ref, fg_ref, o_ref = refs
    else:
        x_ref, h_ref, w1_ref, w3_ref, w2_ref, o_ref = refs
    k = pl.program_id(1)

    @pl.when(k == 0)
    def _():
        o_ref[...] = h_ref[...]

    x = x_ref[...]
    hid = jax.nn.silu(_dot(x, w1_ref[...])) * _dot(x, w3_ref[...])
    o_ref[...] += _dot(hid.astype(BF16), w2_ref[...])

    if final:
        @pl.when(k == pl.num_programs(1) - 1)
        def _():
            y = o_ref[...]
            ms = jnp.mean(y * y, axis=-1, keepdims=True)
            o_ref[...] = y * lax.rsqrt(ms + EPS) * fg_ref[...]


def ffn(f, h2, w1, w3, w2, final_g=None, tm=512, tf=512):
    n, d = h2.shape
    ff = w1.shape[1]
    tm = min(tm, n)
    tf = min(tf, ff)
    in_specs = [
        pl.BlockSpec((tm, d), lambda i, k: (i, 0)),
        pl.BlockSpec((tm, d), lambda i, k: (i, 0)),
        pl.BlockSpec((d, tf), lambda i, k: (0, k)),
        pl.BlockSpec((d, tf), lambda i, k: (0, k)),
        pl.BlockSpec((tf, d), lambda i, k: (k, 0)),
    ]
    args = [f, h2, w1, w3, w2]
    if final_g is not None:
        in_specs.append(pl.BlockSpec((1, d), lambda i, k: (0, 0)))
        args.append(final_g)
    return pl.pallas_call(
        functools.partial(_ffn_kernel, final=final_g is not None),
        grid=(n // tm, ff // tf),
        in_specs=in_specs,
        out_specs=pl.BlockSpec((tm, d), lambda i, k: (i, 0)),
        out_shape=jax.ShapeDtypeStruct((n, d), F32),
        compiler_params=_cparams("parallel", "arbitrary"),
        name="ffn",
    )(*args)


MOE_TM = 512


def _row_copy(src_ref, src_row, dst_ref, dst_row, sem):
    return pltpu.make_async_copy(src_ref.at[pl.ds(src_row, 1)], dst_ref.at[pl.ds(dst_row, 1)], sem)


def _moe_ffn_kernel(te_ref, nu_ref, src_ref, f_ref, w1_ref, w3_ref, w2_ref, o_ref, xin_ref, xb_ref, sem,
                    *, tm, chunk):
    del te_ref
    i = pl.program_id(0)
    k = pl.program_id(1)
    nt = pl.num_programs(0)
    nk = pl.num_programs(1)
    n_used = nu_ref[0]
    used = i < n_used
    slot = i % 2

    def start_chunk(tile, c, dst_slot):
        for u in range(chunk):
            row = c * chunk + u
            _row_copy(f_ref, src_ref[tile * tm + row], xin_ref.at[dst_slot], row, sem.at[dst_slot]).start()

    def wait_tile(dst_slot):
        def drain(c, carry):
            for u in range(chunk):
                _row_copy(f_ref, 0, xin_ref.at[dst_slot], 0, sem.at[dst_slot]).wait()
            return carry
        lax.fori_loop(0, nk, drain, 0)

    @pl.when((i == 0) & (k == 0))
    def _():
        def first(c, carry):
            start_chunk(0, c, 0)
            return carry
        lax.fori_loop(0, nk, first, 0)

    @pl.when((k == 0) & (i <= n_used))
    def _():
        wait_tile(slot)
        xb_ref[...] = xin_ref[slot, :tm, :].astype(BF16)

    @pl.when(k == 0)
    def _():
        o_ref[...] = jnp.zeros(o_ref.shape, F32)

    @pl.when(used)
    def _():
        start_chunk(i + 1, k, 1 - slot)
        x = xb_ref[...]
        hid = jax.nn.silu(_dot(x, w1_ref[0])) * _dot(x, w3_ref[0])
        o_ref[...] += _dot(hid.astype(BF16), w2_ref[0])

    @pl.when(used & (i == nt - 1) & (k == nk - 1))
    def _():
        wait_tile(1 - slot)


def moe_ffn(tile_expert, n_used, src, f, w1, w3, w2, rows, tm=MOE_TM, tf=256):
    n, d = f.shape
    ff = w1.shape[2]
    tf = min(tf, ff)
    nk = ff // tf
    chunk = pl.cdiv(tm, nk)
    buf_rows = ((chunk * nk + 7) // 8) * 8
    kk = lambda i, k, nu: jnp.where(i < nu[0], k, nk - 1)
    return pl.pallas_call(
        functools.partial(_moe_ffn_kernel, tm=tm, chunk=chunk),
        grid_spec=pltpu.PrefetchScalarGridSpec(
            num_scalar_prefetch=3,
            grid=(rows // tm, nk),
            in_specs=[
                pl.BlockSpec(memory_space=pl.ANY),
                pl.BlockSpec((1, d, tf), lambda i, k, te, nu, src: (te[i], 0, kk(i, k, nu))),
                pl.BlockSpec((1, d, tf), lambda i, k, te, nu, src: (te[i], 0, kk(i, k, nu))),
                pl.BlockSpec((1, tf, d), lambda i, k, te, nu, src: (te[i], kk(i, k, nu), 0)),
            ],
            out_specs=pl.BlockSpec((tm, d), lambda i, k, te, nu, src: (i, 0)),
            scratch_shapes=[pltpu.VMEM((2, buf_rows, d), F32), pltpu.VMEM((tm, d), BF16),
                            pltpu.SemaphoreType.DMA((2,))],
        ),
        out_shape=jax.ShapeDtypeStruct((rows, d), F32),
        compiler_params=_cparams("arbitrary", "arbitrary"),
        name="moe_ffn",
    )(tile_expert, n_used, src, f, w1, w3, w2)


def _combine_kernel(*refs, tm, final):
    if final:
        dest_ref, h_ref, rt_ref, yg_ref, fg_ref, o_ref, buf_ref, sem = refs
    else:
        dest_ref, h_ref, rt_ref, yg_ref, o_ref, buf_ref, sem = refs
    i = pl.program_id(0)
    nt = pl.num_programs(0)
    slot = i % 2

    def start_row(tile, j, dst_slot):
        for s in range(TOP_K):
            _row_copy(yg_ref, dest_ref[(tile * tm + j) * TOP_K + s], buf_ref.at[dst_slot, s], j,
                      sem.at[dst_slot]).start(priority=s)

    def wait_tile(dst_slot):
        def drain(j, carry):
            for s in range(TOP_K):
                _row_copy(yg_ref, 0, buf_ref.at[dst_slot, s], 0, sem.at[dst_slot]).wait()
            return carry
        lax.fori_loop(0, tm, drain, 0, unroll=8)

    @pl.when(i == 0)
    def _():
        def first(j, carry):
            start_row(0, j, 0)
            return carry
        lax.fori_loop(0, tm, first, 0, unroll=8)

    wait_tile(slot)
    for j in range(tm):
        start_row(i + 1, j, 1 - slot)

    rt = rt_ref[...]
    y = (h_ref[...] + rt[:, ROUTE_G1:ROUTE_G1 + 1] * buf_ref[slot, 0]
         + rt[:, ROUTE_G2:ROUTE_G2 + 1] * buf_ref[slot, 1])
    if final:
        ms = jnp.mean(y * y, axis=-1, keepdims=True)
        y = y * lax.rsqrt(ms + EPS) * fg_ref[...]
    o_ref[...] = y

    @pl.when(i == nt - 1)
    def _():
        wait_tile(1 - slot)


def moe_combine(dest, h2, route, yg, final_g=None, tm=128):
    n, d = h2.shape
    tm = min(tm, n)
    dest = jnp.pad(dest, (0, tm * TOP_K))
    in_specs = [
        pl.BlockSpec((tm, d), lambda i, dest: (i, 0)),
        pl.BlockSpec((tm, LANES), lambda i, dest: (i, 0)),
        pl.BlockSpec(memory_space=pl.ANY),
    ]
    args = [dest, h2, route, yg]
    if final_g is not None:
        in_specs.append(pl.BlockSpec((1, d), lambda i, dest: (0, 0)))
        args.append(final_g)
    return pl.pallas_call(
        functools.partial(_combine_kernel, tm=tm, final=final_g is not None),
        grid_spec=pltpu.PrefetchScalarGridSpec(
            num_scalar_prefetch=1,
            grid=(n // tm,),
            in_specs=in_specs,
            out_specs=pl.BlockSpec((tm, d), lambda i, dest: (i, 0)),
            scratch_shapes=[pltpu.VMEM((2, TOP_K, tm, d), F32), pltpu.SemaphoreType.DMA((2,))],
        ),
        out_shape=jax.ShapeDtypeStruct((n, d), F32),
        compiler_params=_cparams("arbitrary"),
        name="moe_combine",
    )(*args)


def _routing_tables(route, tm):
    n = route.shape[0]
    flat_e = route[:, ROUTE_E1:ROUTE_E2 + 1].astype(jnp.int32).reshape(-1)
    onehot = (flat_e[:, None] == jnp.arange(N_EXPERTS, dtype=jnp.int32)[None, :]).astype(jnp.int32)
    csum = jnp.cumsum(onehot, axis=0)
    rank = jnp.sum(csum * onehot, axis=1) - 1
    counts = csum[-1]
    padded = ((counts + tm - 1) // tm) * tm
    off_end = jnp.cumsum(padded)
    dest = (off_end - padded)[flat_e] + rank
    n_tiles = (n * TOP_K) // tm + N_EXPERTS
    n_used = (off_end[-1] // tm).astype(jnp.int32)
    tile_start = jnp.arange(n_tiles, dtype=jnp.int32) * tm
    tile_start = jnp.minimum(tile_start, off_end[-1] - tm)
    tile_expert = jnp.sum((tile_start[:, None] >= off_end[None, :]).astype(jnp.int32), axis=1)
    rows = n_tiles * tm
    src = jnp.zeros((rows + tm + 8,), jnp.int32).at[dest].set(jnp.arange(n * TOP_K, dtype=jnp.int32) // TOP_K)
    return dest.astype(jnp.int32), src, tile_expert.astype(jnp.int32), n_used.reshape(1), rows


def moe_layer(h2, f, route, w1, w3, w2, final_g=None):
    dest, src, tile_expert, n_used, rows = _routing_tables(route, MOE_TM)
    yg = moe_ffn(tile_expert, n_used, src, f, w1, w3, w2, rows)
    return moe_combine(dest, h2, route, yg, final_g=final_g)


def _reorder_w_in(w_in):
    d = w_in.shape[0]
    kvw = NSA_KV_WIDTH
    o_gate = NSA_WIDTH + 6 * kvw
    o_uv = o_gate + 3 * NSA_HEADS
    o_ret = o_uv + 2 * GMLP_WIDTH
    gate = w_in[:, o_gate:o_uv].reshape(d, NSA_KV_HEADS, 3 * NSA_GROUP)
    gate = jnp.pad(gate, ((0, 0), (0, 0), (0, LANES - 3 * NSA_GROUP))).reshape(d, NSA_KV_HEADS * LANES)
    parts = [w_in[:, :o_gate], w_in[:, o_uv:], gate]
    w = jnp.concatenate(parts, axis=1)
    w = jnp.pad(w, ((0, 0), (0, PROJ_COLS - w.shape[1])))
    scale = np.ones((1, PROJ_COLS), np.float32)
    scale[:, COL_Q * LANES:(COL_Q + NSA_HEADS) * LANES] = HEAD_DIM ** -0.5
    scale[:, COL_RK * LANES:(COL_RK + RET_HEADS) * LANES] = HEAD_DIM ** -0.5
    del o_ret
    return w.astype(BF16), jnp.asarray(scale)


def _key_extra_lanes(seq):
    pos = np.arange(seq)
    kx = np.zeros((seq, LANES), np.float32)
    kx[pos, pos // SLC_LEN] = 1.0
    kx[:, XL_HI] = pos // SLC_LEN
    kx[:, XL_LO] = pos % SLC_LEN
    kx[:, XL_ONE_A] = 1.0
    kx[:, XL_ONE_B] = 1.0
    return jnp.asarray(kx, BF16)


def _share_t(seq):
    nrow = seq // CMP_STRIDE
    nb = seq // SLC_LEN
    c0 = np.arange(nrow)[None, :] * CMP_STRIDE
    s0 = np.arange(nb)[:, None] * SLC_LEN
    overlap = np.minimum(c0 + CMP_LEN, s0 + SLC_LEN) - np.maximum(c0, s0)
    share = np.clip(overlap, 0, CMP_LEN).astype(np.float32) / CMP_LEN
    share[:, nrow - 1] = 0.0
    return jnp.asarray(share, BF16)


def _retention_tables():
    hh = np.arange(RET_HEADS, dtype=np.float64)
    lg = np.log1p(-np.exp2(-5.0 - hh))
    nn = np.arange(RET_CHUNK, dtype=np.float64)
    rel = nn[:, None] - nn[None, :]
    dec = np.where(rel >= 0, np.exp(lg[:, None, None] * np.maximum(rel, 0.0)), 0.0)
    zeta = np.exp(lg[:, None] * (RET_CHUNK - 1.0 - nn))[:, :, None]
    xi = np.exp(lg[:, None] * (nn + 1.0))[:, :, None]
    dch = np.broadcast_to(np.exp(lg * RET_CHUNK)[:, None, None], (RET_HEADS, 1, HEAD_DIM))
    return tuple(jnp.asarray(a, F32) for a in (dec, zeta, xi, dch))


def _token_mixer(h2, batch, seq, ln1_g, w_in, pe_k, w1_k, w2_k, pe_v, w1_v, w2_v,
                 g_ln_g, g_ln_b, g_ws, g_bs, ret_gn_g, consts):
    kx, sht, ret_tabs, cmp_tabs, axq, wbias = consts
    w_r, colscale = _reorder_w_in(w_in)
    proj = in_proj(h2, ln1_g[None, :], w_r, colscale)
    nrow = seq // CMP_STRIDE
    kv = proj[:, COL_KCMP * LANES:(COL_VCMP + NSA_KV_HEADS) * LANES]
    kv = kv.reshape(batch, nrow, CMP_STRIDE, 2, NSA_KV_HEADS, HEAD_DIM)
    rows = kv.transpose(3, 0, 4, 1, 2, 5).reshape(2, batch * NSA_KV_HEADS, nrow, CMP_STRIDE * HEAD_DIM)
    pe = jnp.stack([pe_k, pe_v]).reshape(2, 1, CMP_LEN * HEAD_DIM)
    kvc = compress(rows, pe, jnp.stack([w1_k, w1_v]).astype(BF16), jnp.stack([w2_k, w2_v]).astype(BF16))
    o_cmp, xsel, counts = cmp_attn(proj, kvc, sht, cmp_tabs, batch, seq)
    o_slc = sel_attn(proj, xsel, counts, kx, axq, batch, seq)
    o_nsa = win_attn(proj, kx, axq, wbias, o_cmp, o_slc, batch, seq)
    bsx = jnp.repeat(g_bs.T, HEAD_DIM, axis=1)
    o_gm = gmlp(proj, g_ln_g[None, :], g_ln_b[None, :], g_ws, bsx)
    o_ret = retention(proj, ret_tabs, ret_gn_g[None, :], batch, seq)
    return o_nsa, o_gm, o_ret


def kernel(x, ln1_g, w_in, cmp_pe_k, cmp_w1_k, cmp_w2_k, cmp_pe_v, cmp_w1_v, cmp_w2_v, gmlp_ln_g, gmlp_ln_b, gmlp_ws, gmlp_bs, ret_gn_g, w_out, ln2_g, ffn_w1, ffn_w3, ffn_w2, moe_wr, moe_br, moe_w1, moe_w3, moe_w2, final_g):
    batch, seq, d = x.shape
    depth = w_in.shape[0]
    consts = (_key_extra_lanes(seq), _share_t(seq), _retention_tables(), _cmp_tables(seq),
              _alibi_q_table(seq), _window_bias(WIN_TQ))
    h2 = x.reshape(batch * seq, d)
    for layer in range(depth):
        o_nsa, o_gm, o_ret = _token_mixer(
            h2, batch, seq, ln1_g[layer], w_in[layer], cmp_pe_k[layer], cmp_w1_k[layer], cmp_w2_k[layer],
            cmp_pe_v[layer], cmp_w1_v[layer], cmp_w2_v[layer], gmlp_ln_g[layer], gmlp_ln_b[layer],
            gmlp_ws[layer], gmlp_bs[layer], ret_gn_g[layer], consts)
        fin = final_g[None, :] if layer == depth - 1 else None
        i = layer // 2
        w_o = w_out[layer].astype(BF16)
        if layer % 2 == 0:
            h2, f = out_proj(o_nsa, o_gm, o_ret, h2, w_o, ln2_g[layer][None, :])
            h2 = ffn(f, h2, ffn_w1[i].astype(BF16), ffn_w3[i].astype(BF16), ffn_w2[i].astype(BF16),
                     final_g=fin)
        else:
            wr = jnp.pad(moe_wr[i], ((0, 0), (0, LANES - N_EXPERTS)))
            wr_hi = wr.astype(BF16)
            wr_lo = (wr - wr_hi.astype(F32)).astype(BF16)
            br = jnp.pad(moe_br[i], (0, LANES - N_EXPERTS))[None, :]
            h2, f, route = out_proj(o_nsa, o_gm, o_ret, h2, w_o, ln2_g[layer][None, :],
                                    router_w=(wr_hi, wr_lo, br))
            h2 = moe_layer(h2, f, route, moe_w1[i].astype(BF16), moe_w3[i].astype(BF16),
                           moe_w2[i].astype(BF16), final_g=fin)
    return h2.reshape(batch, seq, d)
```
